```python
import jax, jax.numpy as jnp
from jax import lax
import numpy as np

D_MODEL = 1024
BATCH = 4
SEQ = 4096
DEPTH = 1

EPS = 1e-6
N_SUB = 3
D_FF = 2816
CHUNK = 64
HG_HEADS = 4
HG_DK = 128
HG_DV = 128
HG_QK = HG_HEADS * HG_DK
HG_WIDTH = HG_HEADS * HG_DV
GLA_HEADS = 4
GLA_DK = 64
GLA_DV = 128
GLA_K = GLA_HEADS * GLA_DK
GLA_V = GLA_HEADS * GLA_DV
GLA_RANK = 16
GLA_NORMALIZER = 16.0
IN_SPLITS = (HG_QK, HG_QK, HG_WIDTH, HG_WIDTH,
             GLA_K, GLA_K, GLA_V, GLA_V, GLA_RANK,
             D_MODEL, D_MODEL)
IN_WIDTH = sum(IN_SPLITS)
IN_OFFSETS = tuple(int(o) for o in np.cumsum(IN_SPLITS)[:-1])

kernel_name = 'hybrid_hgrn2_gla_macaron_adaln'


def rms_norm(x, gain):
    xf = x.astype(jnp.float32)
    y = xf * lax.rsqrt(jnp.mean(xf * xf, axis=-1, keepdims=True) + EPS)
    return (y * gain.astype(jnp.float32)).astype(x.dtype)


def modulate(h, shift, scale):
    return h * (1 + scale[:, None, :]) + shift[:, None, :]


def swiglu(h, w_in, w_down):
    gate, up = jnp.split(h @ w_in, 2, axis=-1)
    return (jax.nn.silu(gate) * up) @ w_down


def to_heads(t, n_heads):
    b, s, w = t.shape
    return t.reshape(b, s, n_heads, w // n_heads).transpose(0, 2, 1, 3)


def from_heads(t):
    b, h, s, d = t.shape
    return t.transpose(0, 2, 1, 3).reshape(b, s, h * d)


def chunked_gated_linear_attention(q, k, v, log_a):
    b_, h_, t_, dk = q.shape
    dv = v.shape[-1]
    n = t_ // CHUNK

    def to_chunks(t):
        return jnp.moveaxis(t.astype(jnp.float32).reshape(b_, h_, n, CHUNK, t.shape[-1]), 2, 0)

    qc, kc, vc, gc = to_chunks(q), to_chunks(k), to_chunks(v), to_chunks(log_a)
    causal = jnp.tril(jnp.ones((CHUNK, CHUNK), dtype=bool))[:, :, None]

    def step(state, inp):
        qi, ki, vi, gi = inp
        b = jnp.cumsum(gi, axis=-2)
        o_inter = jnp.einsum('bhtk,bhkv->bhtv', qi * jnp.exp(b), state)
        diff = b[:, :, :, None, :] - b[:, :, None, :, :]
        decay = jnp.exp(jnp.where(causal, diff, -jnp.inf))
        scores = jnp.einsum('bhtk,bhsk,bhtsk->bhts', qi, ki, decay)
        o = o_inter + jnp.einsum('bhts,bhsv->bhtv', scores, vi)
        b_last = b[:, :, -1:, :]
        k_dec = ki * jnp.exp(b_last - b)
        state = (jnp.exp(b_last[:, :, 0, :])[..., None] * state
                 + jnp.einsum('bhsk,bhsv->bhkv', k_dec, vi))
        return state, o

    s0 = jnp.zeros((b_, h_, dk, dv), jnp.float32)
    _, o = lax.scan(step, s0, (qc, kc, vc, gc))
    return jnp.moveaxis(o, 0, 2).reshape(b_, h_, t_, dv)


def hgrn2_branch(q, f, i, g, lower_bound, norm_gain):
    dtype = i.dtype
    qf = jax.nn.silu(q.astype(jnp.float32))
    forget = lower_bound + (1 - lower_bound) * jax.nn.sigmoid(f.astype(jnp.float32))
    o = chunked_gated_linear_attention(to_heads(qf, HG_HEADS), to_heads(1 - forget, HG_HEADS),
                                       to_heads(i, HG_HEADS), to_heads(jnp.log(forget), HG_HEADS))
    o = rms_norm(o, norm_gain) * jax.nn.silu(to_heads(g, HG_HEADS).astype(jnp.float32))
    return from_heads(o).astype(dtype)


def gla_branch(q, k, v, g, gk_low, w_gk_up, b_gk_up, norm_gain):
    dtype = v.dtype
    gk = gk_low @ w_gk_up + b_gk_up
    log_a = jax.nn.log_sigmoid(gk.astype(jnp.float32)) / GLA_NORMALIZER
    qs = q.astype(jnp.float32) * (GLA_DK ** -0.5)
    o = chunked_gated_linear_attention(to_heads(qs, GLA_HEADS), to_heads(k, GLA_HEADS),
                                       to_heads(v, GLA_HEADS), to_heads(log_a, GLA_HEADS))
    o = rms_norm(o, norm_gain) * jax.nn.silu(to_heads(g, GLA_HEADS).astype(jnp.float32))
    return from_heads(o).astype(dtype)


def setup_inputs(seed: int = 0) -> dict:
    key = jax.random.key(seed)
    ks = jax.random.split(key, 20)
    D = D_MODEL

    def w(k, shape, fan_in, s=1.0):
        return jax.random.normal(k, shape, jnp.float32) * (s * fan_in ** -0.5)

    return {
        'x': jax.random.normal(ks[0], (BATCH, SEQ, D), jnp.float32),
        'c': jax.random.normal(ks[1], (BATCH, D), jnp.float32),
        'w_ada': w(ks[2], (DEPTH, D, N_SUB * 3 * D), D, 0.5),
        'b_ada': 0.02 * jax.random.normal(ks[3], (DEPTH, N_SUB * 3 * D), jnp.float32),
        'norm_gains': 1.0 + 0.02 * jax.random.normal(ks[4], (DEPTH, N_SUB, D), jnp.float32),
        'ffn1_w_in': w(ks[5], (DEPTH, D, 2 * D_FF), D),
        'ffn1_w_down': w(ks[6], (DEPTH, D_FF, D), D_FF),
        'w_in_mix': w(ks[7], (DEPTH, D, IN_WIDTH), D),
        'w_gk_up': w(ks[8], (DEPTH, GLA_RANK, GLA_K), GLA_RANK),
        'b_gk_up': 0.1 * jax.random.normal(ks[9], (DEPTH, GLA_K), jnp.float32),
        'lb_logits': 0.5 * jax.random.normal(ks[10], (DEPTH + 1, HG_QK), jnp.float32),
        'hg_norm': 1.0 + 0.02 * jax.random.normal(ks[11], (DEPTH, HG_DV), jnp.float32),
        'gla_norm': 1.0 + 0.02 * jax.random.normal(ks[12], (DEPTH, GLA_DV), jnp.float32),
        'w_up_hg': w(ks[13], (DEPTH, HG_WIDTH, D), HG_WIDTH),
        'w_up_gla': w(ks[14], (DEPTH, GLA_V, D), GLA_V),
        'w_out': w(ks[15], (DEPTH, D, D), D),
        'ffn2_w_in': w(ks[16], (DEPTH, D, 2 * D_FF), D),
        'ffn2_w_down': w(ks[17], (DEPTH, D_FF, D), D_FF),
        'final_norm': 1.0 + 0.02 * jax.random.normal(ks[18], (D,), jnp.float32),
    }


def reference(x, c, w_ada, b_ada, norm_gains, ffn1_w_in, ffn1_w_down, w_in_mix,
              w_gk_up, b_gk_up, lb_logits, hg_norm, gla_norm, w_up_hg, w_up_gla,
              w_out, ffn2_w_in, ffn2_w_down, final_norm):
    b_, _, d = x.shape
    lower_bounds = jnp.cumsum(jax.nn.softmax(lb_logits.astype(jnp.float32), axis=0), axis=0)
    cond = jax.nn.silu(c)
    for l in range(DEPTH):
        mod = (cond @ w_ada[l] + b_ada[l]).reshape(b_, N_SUB, 3, d)

        h = modulate(rms_norm(x, norm_gains[l, 0]), mod[:, 0, 0], mod[:, 0, 1])
        x = x + 0.5 * mod[:, 0, 2][:, None, :] * swiglu(h, ffn1_w_in[l], ffn1_w_down[l])

        h = modulate(rms_norm(x, norm_gains[l, 1]), mod[:, 1, 0], mod[:, 1, 1])
        proj = h @ w_in_mix[l]
        (hq, hf, hi, hg, gq, gk, gv, gg, g_low, r_hg, r_gla) = jnp.split(proj, IN_OFFSETS, axis=-1)
        y_hg = hgrn2_branch(hq, hf, hi, hg, lower_bounds[l], hg_norm[l])
        y_gla = gla_branch(gq, gk, gv, gg, g_low, w_gk_up[l], b_gk_up[l], gla_norm[l])
        merged = (jax.nn.sigmoid(r_hg) * (y_hg @ w_up_hg[l])
                  + jax.nn.sigmoid(r_gla) * (y_gla @ w_up_gla[l]))
        x = x + mod[:, 1, 2][:, None, :] * (merged @ w_out[l])

        h = modulate(rms_norm(x, norm_gains[l, 2]), mod[:, 2, 0], mod[:, 2, 1])
        x = x + 0.5 * mod[:, 2, 2][:, None, :] * swiglu(h, ffn2_w_in[l], ffn2_w_down[l])
    return rms_norm(x, final_norm)
```

```python
import functools

import jax
import jax.numpy as jnp
from jax import lax
from jax.experimental import pallas as pl
from jax.experimental.pallas import tpu as pltpu

F32 = jnp.float32
BF16 = jnp.bfloat16

EPS = 1e-6
N_SUB = 3
HG_HEADS = 4
HG_DK = 128
HG_DV = 128
GLA_HEADS = 4
GLA_DK = 64
GLA_DV = 128
GLA_RANK = 16
GLA_NORMALIZER = 16.0

LANES = 128
SUBLANES = 8
MIB = 1024 * 1024

REC_CHUNK = 64
PERM_STRIDE = 4
PERM_GROUP = PERM_STRIDE * SUBLANES
N_GROUPS = REC_CHUNK // PERM_GROUP
SAFE_EXP = 70.0
SAFE_MAG = 1e7


def _dot(a, b):
    return jnp.dot(a, b, preferred_element_type=F32)


def _dot_nt(a, b):
    return lax.dot_general(a, b, (((1,), (1,)), ((), ())), preferred_element_type=F32)


def _dot_tn(a, b):
    return lax.dot_general(a, b, (((0,), (0,)), ((), ())), preferred_element_type=F32)


def _silu(v):
    return v * jax.nn.sigmoid(v)


def _norm_mod(x, gain, shift, scale):
    y = x * lax.rsqrt(jnp.mean(x * x, axis=-1, keepdims=True) + EPS) * gain
    return y * (1.0 + scale) + shift


def _ada_kernel(c_ref, w_ref, b_ref, o_ref):
    cond = _silu(c_ref[...]).astype(BF16)
    o_ref[...] = _dot(cond, w_ref[...].astype(BF16)) + b_ref[...]


def _ada(c_pad, w, b):
    rows, d = c_pad.shape
    n = w.shape[1]
    bn = n // 8
    return pl.pallas_call(
        _ada_kernel,
        grid=(n // bn,),
        in_specs=[
            pl.BlockSpec((rows, d), lambda j: (0, 0)),
            pl.BlockSpec((d, bn), lambda j: (0, j)),
            pl.BlockSpec((1, bn), lambda j: (0, j)),
        ],
        out_specs=pl.BlockSpec((rows, bn), lambda j: (0, j)),
        out_shape=jax.ShapeDtypeStruct((rows, n), F32),
        name="adaln",
    )(c_pad, w, b.reshape(1, n))


def _ffn_kernel(x_ref, mod_ref, gain_ref, w_in_ref, w_down_ref, fgain_ref, o_ref, *, sub, d_ff, n_split, final):
    x = x_ref[...]
    shift = mod_ref[0, 3 * sub + 0:3 * sub + 1, :]
    scale = mod_ref[0, 3 * sub + 1:3 * sub + 2, :]
    gate = mod_ref[0, 3 * sub + 2:3 * sub + 3, :]
    hb = _norm_mod(x, gain_ref[...], shift, scale).astype(BF16)
    fc = d_ff // n_split
    acc = None
    for j in range(n_split):
        g = _dot(hb, w_in_ref[:, j * fc:(j + 1) * fc])
        u = _dot(hb, w_in_ref[:, d_ff + j * fc:d_ff + (j + 1) * fc])
        act = (_silu(g) * u).astype(BF16)
        part = _dot(act, w_down_ref[j * fc:(j + 1) * fc, :])
        acc = part if acc is None else acc + part
    xn = x + (0.5 * gate) * acc
    if final:
        xn = xn * lax.rsqrt(jnp.mean(xn * xn, axis=-1, keepdims=True) + EPS) * fgain_ref[...]
    o_ref[...] = xn


def _ffn(x2d, mod3, gain, w_in, w_down, fgain, *, sub, tokens_per_batch, final, tm=256):
    n, d = x2d.shape
    d_ff = w_down.shape[0]
    tiles_per_batch = tokens_per_batch // tm
    n_split = 2 if d_ff % (2 * LANES) == 0 else 1
    const = dict(pipeline_mode=pl.Buffered(1))
    return pl.pallas_call(
        functools.partial(_ffn_kernel, sub=sub, d_ff=d_ff, n_split=n_split, final=final),
        grid=(n // tm,),
        in_specs=[
            pl.BlockSpec((tm, d), lambda i: (i, 0)),
            pl.BlockSpec((1, 3 * N_SUB, d), lambda i: (i // tiles_per_batch, 0, 0)),
            pl.BlockSpec((1, d), lambda i: (0, 0)),
            pl.BlockSpec((d, 2 * d_ff), lambda i: (0, 0), **const),
            pl.BlockSpec((d_ff, d), lambda i: (0, 0), **const),
            pl.BlockSpec((1, d), lambda i: (0, 0)),
        ],
        out_specs=pl.BlockSpec((tm, d), lambda i: (i, 0)),
        out_shape=jax.ShapeDtypeStruct((n, d), F32),
        compiler_params=pltpu.CompilerParams(vmem_limit_bytes=48 * MIB),
        name="ffn_final" if final else "ffn",
    )(x2d, mod3, gain.reshape(1, d), w_in, w_down, fgain.reshape(1, d))


def _inproj_kernel(x_ref, mod_ref, gain_ref, w_ref, mix_ref, r_ref, *, sub, n_mix):
    shift = mod_ref[0, 3 * sub + 0:3 * sub + 1, :]
    scale = mod_ref[0, 3 * sub + 1:3 * sub + 2, :]
    hb = _norm_mod(x_ref[...], gain_ref[...], shift, scale).astype(BF16)
    mix_ref[...] = _dot(hb, w_ref[:, :n_mix])
    r_ref[...] = _dot(hb, w_ref[:, n_mix:])


def _inproj(x2d, mod3, gain, w, n_mix, *, sub, tokens_per_batch, tm=256):
    n, d = x2d.shape
    n_all = w.shape[1]
    tiles_per_batch = tokens_per_batch // tm
    return pl.pallas_call(
        functools.partial(_inproj_kernel, sub=sub, n_mix=n_mix),
        grid=(n // tm,),
        in_specs=[
            pl.BlockSpec((tm, d), lambda i: (i, 0)),
            pl.BlockSpec((1, 3 * N_SUB, d), lambda i: (i // tiles_per_batch, 0, 0)),
            pl.BlockSpec((1, d), lambda i: (0, 0)),
            pl.BlockSpec((d, n_all), lambda i: (0, 0), pipeline_mode=pl.Buffered(1)),
        ],
        out_specs=[
            pl.BlockSpec((tm, n_mix), lambda i: (i, 0)),
            pl.BlockSpec((tm, n_all - n_mix), lambda i: (i, 0)),
        ],
        out_shape=[
            jax.ShapeDtypeStruct((n, n_mix), F32),
            jax.ShapeDtypeStruct((n, n_all - n_mix), F32),
        ],
        compiler_params=pltpu.CompilerParams(vmem_limit_bytes=48 * MIB),
        name="inproj",
    )(x2d, mod3, gain.reshape(1, d), w)


def _load_perm(ref, base):
    return [ref[pl.ds(base + PERM_GROUP * c + a, SUBLANES, stride=PERM_STRIDE), :]
            for c in range(N_GROUPS) for a in range(PERM_STRIDE)]


def _store_perm(ref, base, val):
    for c in range(N_GROUPS):
        for a in range(PERM_STRIDE):
            u = PERM_STRIDE * c + a
            ref[pl.ds(base + PERM_GROUP * c + a, SUBLANES, stride=PERM_STRIDE), :] = (
                val[SUBLANES * u:SUBLANES * (u + 1), :])


def _shift_down(v, d, sub_iota):
    return jnp.where(sub_iota >= d, pltpu.roll(v, d, 0), 0.0)


def _cumsum_perm(g, sub_iota):
    out = []
    carry = None
    for c in range(N_GROUPS):
        pre = [g[PERM_STRIDE * c]]
        for a in range(1, PERM_STRIDE):
            pre.append(pre[-1] + g[PERM_STRIDE * c + a])
        inc = pre[-1]
        inc = inc + _shift_down(inc, 1, sub_iota)
        inc = inc + _shift_down(inc, 2, sub_iota)
        inc = inc + _shift_down(inc, 4, sub_iota)
        exc = _shift_down(inc, 1, sub_iota)
        if carry is not None:
            exc = exc + carry
        out.extend(p + exc for p in pre)
        tot = inc[SUBLANES - 1:SUBLANES, :]
        carry = tot if carry is None else carry + tot
    return jnp.concatenate(out, axis=0), carry


def _perm_token_ids(shape, dim):
    r = lax.broadcasted_iota(jnp.int32, shape, dim)
    u = r >> 3
    return PERM_GROUP * (u >> 2) + PERM_STRIDE * (r & 7) + (u & 3)


def _finish(o, gain, gate):
    y = o * lax.rsqrt(jnp.mean(o * o, axis=-1, keepdims=True) + EPS) * gain
    return y * _silu(gate)


def _rec_kernel(hq0, hq1, hf0, hf1, hi0, hi1, hg0, hg1, gq, gk, gv0, gv1, gg0, gg1, glow,
                lbl_ref, hgn_ref, glan_ref, wup_ref, bup_ref,
                yh_ref, yg_ref,
                q_s, k_s, g_s, st_s, fb_s, oi_s, *, tt, layer):
    nc = tt // REC_CHUNK

    @pl.when(pl.program_id(2) == 0)
    def _():
        st_s[...] = jnp.zeros_like(st_s)

    def lower_bound(e):
        lbl = lbl_ref[:, e * LANES:(e + 1) * LANES]
        ex = jnp.exp(lbl - jnp.max(lbl, axis=0, keepdims=True))
        return jnp.sum(ex[:layer + 1], axis=0, keepdims=True) / jnp.sum(ex, axis=0, keepdims=True)

    def stats(q, k, g):
        tot = jnp.sum(g.reshape(nc, REC_CHUNK, LANES), axis=1)
        return jnp.min(tot), jnp.max(jnp.maximum(jnp.abs(q), jnp.abs(k)))

    mins, maxs = [], []
    for e, (hq, hf) in enumerate(((hq0, hf0), (hq1, hf1))):
        q = _silu(hq[...])
        lb = lower_bound(e)
        forget = lb + (1.0 - lb) * jax.nn.sigmoid(hf[...])
        g = jnp.log(forget)
        k = 1.0 - forget
        q_s[e] = q
        k_s[e] = k
        g_s[e] = g
        mn, mx = stats(q, k, g)
        mins.append(mn)
        maxs.append(mx)
    logits = _dot(glow[...].astype(BF16), wup_ref[...].astype(BF16)) + bup_ref[...]
    g = (jnp.minimum(logits, 0.0) - jnp.log(1.0 + jnp.exp(-jnp.abs(logits)))) * (1.0 / GLA_NORMALIZER)
    q = gq[...] * (GLA_DK ** -0.5)
    k = gk[...]
    q_s[2] = q
    k_s[2] = k
    g_s[2] = g
    mn, mx = stats(q, k, g)
    mins.append(mn)
    maxs.append(mx)
    min_tot = functools.reduce(jnp.minimum, mins)
    max_mag = functools.reduce(jnp.maximum, maxs)
    safe = jnp.logical_and(min_tot >= -2.0 * SAFE_EXP, max_mag <= SAFE_MAG)

    sub_iota = lax.broadcasted_iota(jnp.int32, (SUBLANES, LANES), 0)
    lane = lax.broadcasted_iota(jnp.int32, (REC_CHUNK, LANES), 1)
    tok_row = _perm_token_ids((REC_CHUNK, REC_CHUNK), 0)
    tok_col = _perm_token_ids((REC_CHUNK, REC_CHUNK), 1)
    causal = tok_col <= tok_row
    tok_of_row = _perm_token_ids((REC_CHUNK, LANES), 0)
    hgn = hgn_ref[...]
    glan = glan_ref[...]

    bodies = (
        (0, None, hi0, hg0, hgn, yh_ref.at[0], 0),
        (1, None, hi1, hg1, hgn, yh_ref.at[1], 1),
        (2, lane < GLA_DK, gv0, gg0, glan, yg_ref.at[0], 2),
        (2, lane >= GLA_DK, gv1, gg1, glan, yg_ref.at[1], 3),
    )

    def load_qkb(slot, base):
        b, b_last = _cumsum_perm(_load_perm(g_s.at[slot], base), sub_iota)
        q = jnp.concatenate(_load_perm(q_s.at[slot], base), axis=0)
        k = jnp.concatenate(_load_perm(k_s.at[slot], base), axis=0)
        return q, k, b, b_last

    def fast_chunk(ci, carry):
        base = pl.multiple_of(ci * REC_CHUNK, REC_CHUNK)
        shared = {}
        for slot, qmask, v_ref, gate_ref, gain, y_view, st in bodies:
            if slot not in shared:
                q, k, b, b_last = load_qkb(slot, base)
                half = 0.5 * b_last
                d = b - half
                em = jnp.exp(half)
                qt = q * jnp.exp(d)
                kt = k * jnp.exp(-d)
                shared[slot] = (qt.astype(BF16), (qt * em).astype(BF16), kt.astype(BF16),
                                (kt * em).astype(BF16), em * em)
            qt_b, qi_b, kt_b, kd_b, dec = shared[slot]
            if qmask is not None:
                qt_b = jnp.where(qmask, qt_b, jnp.zeros_like(qt_b))
                qi_b = jnp.where(qmask, qi_b, jnp.zeros_like(qi_b))
            vb = jnp.concatenate(_load_perm(v_ref, base), axis=0).astype(BF16)
            gate = jnp.concatenate(_load_perm(gate_ref, base), axis=0)
            s = jnp.where(causal, _dot_nt(qt_b, kt_b), 0.0).astype(BF16)
            state = st_s[st]
            o = _dot_nt(qi_b, state.astype(BF16)) + _dot(s, vb)
            st_s[st] = state * dec + _dot_tn(vb, kd_b)
            _store_perm(y_view, base, _finish(o, gain, gate))
        return carry

    def exact_chunk(ci, carry):
        base = pl.multiple_of(ci * REC_CHUNK, REC_CHUNK)
        for slot, qmask, v_ref, gate_ref, gain, y_view, st in bodies:
            q, k, b, b_last = load_qkb(slot, base)
            if qmask is not None:
                q = jnp.where(qmask, q, 0.0)
            v = jnp.concatenate(_load_perm(v_ref, base), axis=0)
            gate = jnp.concatenate(_load_perm(gate_ref, base), axis=0)
            fb_s[0] = b
            fb_s[1] = q
            fb_s[2] = k
            fb_s[3] = v

            def row(r, c2):
                u = r >> 3
                tr = PERM_GROUP * (u >> 2) + PERM_STRIDE * (r & 7) + (u & 3)
                bt = fb_s[0, pl.ds(r, 1), :]
                qt = fb_s[1, pl.ds(r, 1), :]
                w = jnp.exp(jnp.where(tok_of_row <= tr, bt - fb_s[0], -jnp.inf))
                sc = jnp.sum(qt * w * fb_s[2], axis=-1, keepdims=True)
                oi_s[pl.ds(r, 1), :] = jnp.sum(sc * fb_s[3], axis=0, keepdims=True)
                return c2

            lax.fori_loop(0, REC_CHUNK, row, 0)
            state = st_s[st]
            vb = v.astype(BF16)
            o = _dot_nt((q * jnp.exp(b)).astype(BF16), state.astype(BF16)) + oi_s[...]
            kd_b = (k * jnp.exp(b_last - b)).astype(BF16)
            st_s[st] = state * jnp.exp(b_last) + _dot_tn(vb, kd_b)
            _store_perm(y_view, base, _finish(o, gain, gate))
        return carry

    @pl.when(safe)
    def _():
        lax.fori_loop(0, nc, fast_chunk, 0)

    @pl.when(jnp.logical_not(safe))
    def _():
        lax.fori_loop(0, nc, exact_chunk, 0)


def _recurrence(mix, lb_logits, hg_norm, gla_norm, wup_pad, bup, *, batch, tokens_per_batch, layer, tt=512):
    n = mix.shape[0]
    nt = tokens_per_batch // tt
    pairs = HG_HEADS // 2
    off = {"hq": 0, "hf": 4, "hi": 8, "hg": 12, "gq": 16, "gk": 18, "gv": 20, "gg": 24, "gl": 28}

    def head(name, e):
        return pl.BlockSpec((tt, LANES), lambda b, p, i: (b * nt + i, off[name] + 2 * p + e))

    def pair(name):
        return pl.BlockSpec((tt, LANES), lambda b, p, i: (b * nt + i, off[name] + p))

    in_specs = [
        head("hq", 0), head("hq", 1), head("hf", 0), head("hf", 1),
        head("hi", 0), head("hi", 1), head("hg", 0), head("hg", 1),
        pair("gq"), pair("gk"),
        head("gv", 0), head("gv", 1), head("gg", 0), head("gg", 1),
        pl.BlockSpec((tt, LANES), lambda b, p, i: (b * nt + i, off["gl"])),
    ]
    in_specs_params = [
        pl.BlockSpec((lb_logits.shape[0], 2 * LANES), lambda b, p, i: (0, p)),
        pl.BlockSpec((1, LANES), lambda b, p, i: (0, 0)),
        pl.BlockSpec((1, LANES), lambda b, p, i: (0, 0)),
        pl.BlockSpec((LANES, LANES), lambda b, p, i: (0, p)),
        pl.BlockSpec((1, LANES), lambda b, p, i: (0, p)),
    ]
    out_spec = pl.BlockSpec((2, tt, LANES), lambda b, p, i: (p, b * nt + i, 0))
    return pl.pallas_call(
        functools.partial(_rec_kernel, tt=tt, layer=layer),
        grid=(batch, pairs, nt),
        in_specs=in_specs + in_specs_params,
        out_specs=[out_spec, out_spec],
        out_shape=[jax.ShapeDtypeStruct((HG_HEADS, n, HG_DV), F32),
                   jax.ShapeDtypeStruct((GLA_HEADS, n, GLA_DV), F32)],
        scratch_shapes=[
            pltpu.VMEM((3, tt, LANES), F32),
            pltpu.VMEM((3, tt, LANES), F32),
            pltpu.VMEM((3, tt, LANES), F32),
            pltpu.VMEM((4, LANES, LANES), F32),
            pltpu.VMEM((4, REC_CHUNK, LANES), F32),
            pltpu.VMEM((REC_CHUNK, LANES), F32),
        ],
        compiler_params=pltpu.CompilerParams(
            dimension_semantics=("arbitrary", "arbitrary", "arbitrary")),
        name="recurrence",
    )(*([mix] * 15), lb_logits, hg_norm.reshape(1, HG_DV), gla_norm.reshape(1, GLA_DV), wup_pad,
      bup.reshape(1, -1))


def _merge_kernel(x_ref, mod_ref, yh_ref, yg_ref, rh_ref, rg_ref, wuh_ref, wug_ref, wo_ref, o_ref, *, sub):
    gate = mod_ref[0, 3 * sub + 2:3 * sub + 3, :]

    def up(y_ref, w_ref):
        acc = None
        for h in range(y_ref.shape[0]):
            part = _dot(y_ref[h].astype(BF16), w_ref[h * LANES:(h + 1) * LANES, :])
            acc = part if acc is None else acc + part
        return acc

    merged = (jax.nn.sigmoid(rh_ref[...]) * up(yh_ref, wuh_ref)
              + jax.nn.sigmoid(rg_ref[...]) * up(yg_ref, wug_ref))
    o_ref[...] = x_ref[...] + gate * _dot(merged.astype(BF16), wo_ref[...])


def _merge(x2d, mod3, y_hg, y_gla, r, w_up_hg, w_up_gla, w_out, *, sub, tokens_per_batch, tm=512):
    n, d = x2d.shape
    tiles_per_batch = tokens_per_batch // tm
    const = dict(pipeline_mode=pl.Buffered(1))
    return pl.pallas_call(
        functools.partial(_merge_kernel, sub=sub),
        grid=(n // tm,),
        in_specs=[
            pl.BlockSpec((tm, d), lambda i: (i, 0)),
            pl.BlockSpec((1, 3 * N_SUB, d), lambda i: (i // tiles_per_batch, 0, 0)),
            pl.BlockSpec((HG_HEADS, tm, HG_DV), lambda i: (0, i, 0)),
            pl.BlockSpec((GLA_HEADS, tm, GLA_DV), lambda i: (0, i, 0)),
            pl.BlockSpec((tm, d), lambda i: (i, 0)),
            pl.BlockSpec((tm, d), lambda i: (i, 1)),
            pl.BlockSpec(w_up_hg.shape, lambda i: (0, 0), **const),
            pl.BlockSpec(w_up_gla.shape, lambda i: (0, 0), **const),
            pl.BlockSpec(w_out.shape, lambda i: (0, 0), **const),
        ],
        out_specs=pl.BlockSpec((tm, d), lambda i: (i, 0)),
        out_shape=jax.ShapeDtypeStruct((n, d), F32),
        compiler_params=pltpu.CompilerParams(vmem_limit_bytes=48 * MIB),
        name="merge_out",
    )(x2d, mod3, y_hg, y_gla, r, r, w_up_hg, w_up_gla, w_out)


def kernel(x, c, w_ada, b_ada, norm_gains, ffn1_w_in, ffn1_w_down, w_in_mix, w_gk_up, b_gk_up, lb_logits,
           hg_norm, gla_norm, w_up_hg, w_up_gla, w_out, ffn2_w_in, ffn2_w_down, final_norm):
    batch, seq, d = x.shape
    depth = w_ada.shape[0]
    hg_qk = HG_HEADS * HG_DK
    hg_w = HG_HEADS * HG_DV
    gla_k = GLA_HEADS * GLA_DK
    gla_v = GLA_HEADS * GLA_DV
    n_rec = 2 * hg_qk + 2 * hg_w + 2 * gla_k + 2 * gla_v
    n_mix = n_rec + LANES

    x2d = x.reshape(batch * seq, d)
    c_pad = jnp.pad(c, ((0, SUBLANES - batch % SUBLANES if batch % SUBLANES else 0), (0, 0)))
    for l in range(depth):
        mod = _ada(c_pad, w_ada[l], b_ada[l])[:batch].reshape(batch, 3 * N_SUB, d)

        x2d = _ffn(x2d, mod, norm_gains[l, 0], ffn1_w_in[l].astype(BF16), ffn1_w_down[l].astype(BF16),
                   final_norm, sub=0, tokens_per_batch=seq, final=False)

        w_mix = w_in_mix[l]
        w_r = jnp.concatenate([
            w_mix[:, :n_rec],
            w_mix[:, n_rec:n_rec + GLA_RANK],
            jnp.zeros((d, LANES - GLA_RANK), w_mix.dtype),
            w_mix[:, n_rec + GLA_RANK:],
        ], axis=1).astype(BF16)
        mix, r = _inproj(x2d, mod, norm_gains[l, 1], w_r, n_mix, sub=1, tokens_per_batch=seq)

        wup_pad = jnp.pad(w_gk_up[l], ((0, LANES - GLA_RANK), (0, 0)))
        y_hg, y_gla = _recurrence(mix, lb_logits, hg_norm[l], gla_norm[l], wup_pad, b_gk_up[l],
                                  batch=batch, tokens_per_batch=seq, layer=l)

        x2d = _merge(x2d, mod, y_hg, y_gla, r, w_up_hg[l].astype(BF16), w_up_gla[l].astype(BF16),
                     w_out[l].astype(BF16), sub=1, tokens_per_batch=seq)

        last = l == depth - 1
        x2d = _ffn(x2d, mod, norm_gains[l, 2], ffn2_w_in[l].astype(BF16), ffn2_w_down[l].astype(BF16),
                   final_norm, sub=2, tokens_per_batch=seq, final=last)
    if depth == 0:
        raise ValueError("depth must be >= 1")
    return x2d.reshape(batch, seq, d)
```

```python
import functools

import jax
import jax.numpy as jnp
from jax import lax
from jax.experimental import pallas as pl
from jax.experimental.pallas import tpu as pltpu

F32 = jnp.float32
BF16 = jnp.bfloat16

EPS = 1e-6
N_SUB = 3
HG_HEADS = 4
HG_DK = 128
HG_DV = 128
GLA_HEADS = 4
GLA_DK = 64
GLA_DV = 128
GLA_RANK = 16
GLA_NORMALIZER = 16.0

LANES = 128
SUBLANES = 8
MIB = 1024 * 1024

REC_CHUNK = 64
PERM_STRIDE = 4
PERM_GROUP = PERM_STRIDE * SUBLANES
N_GROUPS = REC_CHUNK // PERM_GROUP
SAFE_EXP = 70.0
SAFE_MAG = 1e7


def _dot(a, b):
    return jnp.dot(a, b, preferred_element_type=F32)


def _dot_nt(a, b):
    return lax.dot_general(a, b, (((1,), (1,)), ((), ())), preferred_element_type=F32)


def _dot_tn(a, b):
    return lax.dot_general(a, b, (((0,), (0,)), ((), ())), preferred_element_type=F32)


def _silu(v):
    return v * jax.nn.sigmoid(v)


def _norm_mod(x, gain, shift, scale):
    y = x * lax.rsqrt(jnp.mean(x * x, axis=-1, keepdims=True) + EPS) * gain
    return y * (1.0 + scale) + shift


def _ada_kernel(c_ref, w_ref, b_ref, o_ref):
    cond = _silu(c_ref[...]).astype(BF16)
    o_ref[...] = _dot(cond, w_ref[...].astype(BF16)) + b_ref[...]


def _ada(c_pad, w, b):
    rows, d = c_pad.shape
    n = w.shape[1]
    bn = n // 8
    return pl.pallas_call(
        _ada_kernel,
        grid=(n // bn,),
        in_specs=[
            pl.BlockSpec((rows, d), lambda j: (0, 0)),
            pl.BlockSpec((d, bn), lambda j: (0, j)),
            pl.BlockSpec((1, bn), lambda j: (0, j)),
        ],
        out_specs=pl.BlockSpec((rows, bn), lambda j: (0, j)),
        out_shape=jax.ShapeDtypeStruct((rows, n), F32),
        name="adaln",
    )(c_pad, w, b.reshape(1, n))


def _ffn_kernel(x_ref, mod_ref, gain_ref, w_in_ref, w_down_ref, fgain_ref, o_ref, *, sub, d_ff, n_split, final):
    x = x_ref[...]
    shift = mod_ref[0, 3 * sub + 0:3 * sub + 1, :]
    scale = mod_ref[0, 3 * sub + 1:3 * sub + 2, :]
    gate = mod_ref[0, 3 * sub + 2:3 * sub + 3, :]
    hb = _norm_mod(x, gain_ref[...], shift, scale).astype(BF16)
    fc = d_ff // n_split
    acc = None
    for j in range(n_split):
        g = _dot(hb, w_in_ref[:, j * fc:(j + 1) * fc])
        u = _dot(hb, w_in_ref[:, d_ff + j * fc:d_ff + (j + 1) * fc])
        act = (_silu(g) * u).astype(BF16)
        part = _dot(act, w_down_ref[j * fc:(j + 1) * fc, :])
        acc = part if acc is None else acc + part
    xn = x + (0.5 * gate) * acc
    if final:
        xn = xn * lax.rsqrt(jnp.mean(xn * xn, axis=-1, keepdims=True) + EPS) * fgain_ref[...]
    o_ref[...] = xn


def _ffn(x2d, mod3, gain, w_in, w_down, fgain, *, sub, tokens_per_batch, final, tm=256):
    n, d = x2d.shape
    d_ff = w_down.shape[0]
    tiles_per_batch = tokens_per_batch // tm
    n_split = 2 if d_ff % (2 * LANES) == 0 else 1
    const = dict(pipeline_mode=pl.Buffered(1))
    return pl.pallas_call(
        functools.partial(_ffn_kernel, sub=sub, d_ff=d_ff, n_split=n_split, final=final),
        grid=(n // tm,),
        in_specs=[
            pl.BlockSpec((tm, d), lambda i: (i, 0)),
            pl.BlockSpec((1, 3 * N_SUB, d), lambda i: (i // tiles_per_batch, 0, 0)),
            pl.BlockSpec((1, d), lambda i: (0, 0)),
            pl.BlockSpec((d, 2 * d_ff), lambda i: (0, 0), **const),
            pl.BlockSpec((d_ff, d), lambda i: (0, 0), **const),
            pl.BlockSpec((1, d), lambda i: (0, 0)),
        ],
        out_specs=pl.BlockSpec((tm, d), lambda i: (i, 0)),
        out_shape=jax.ShapeDtypeStruct((n, d), F32),
        compiler_params=pltpu.CompilerParams(vmem_limit_bytes=48 * MIB),
        name="ffn_final" if final else "ffn",
    )(x2d, mod3, gain.reshape(1, d), w_in, w_down, fgain.reshape(1, d))


def _inproj_kernel(x_ref, mod_ref, gain_ref, w_ref, mix_ref, r_ref, *, sub, n_mix):
    shift = mod_ref[0, 3 * sub + 0:3 * sub + 1, :]
    scale = mod_ref[0, 3 * sub + 1:3 * sub + 2, :]
    hb = _norm_mod(x_ref[...], gain_ref[...], shift, scale).astype(BF16)
    mix_ref[...] = _dot(hb, w_ref[:, :n_mix])
    r_ref[...] = _dot(hb, w_ref[:, n_mix:])


def _inproj(x2d, mod3, gain, w, n_mix, *, sub, tokens_per_batch, tm=256):
    n, d = x2d.shape
    n_all = w.shape[1]
    tiles_per_batch = tokens_per_batch // tm
    return pl.pallas_call(
        functools.partial(_inproj_kernel, sub=sub, n_mix=n_mix),
        grid=(n // tm,),
        in_specs=[
            pl.BlockSpec((tm, d), lambda i: (i, 0)),
            pl.BlockSpec((1, 3 * N_SUB, d), lambda i: (i // tiles_per_batch, 0, 0)),
            pl.BlockSpec((1, d), lambda i: (0, 0)),
            pl.BlockSpec((d, n_all), lambda i: (0, 0), pipeline_mode=pl.Buffered(1)),
        ],
        out_specs=[
            pl.BlockSpec((tm, n_mix), lambda i: (i, 0)),
            pl.BlockSpec((tm, n_all - n_mix), lambda i: (i, 0)),
        ],
        out_shape=[
            jax.ShapeDtypeStruct((n, n_mix), F32),
            jax.ShapeDtypeStruct((n, n_all - n_mix), F32),
        ],
        compiler_params=pltpu.CompilerParams(vmem_limit_bytes=48 * MIB),
        name="inproj",
    )(x2d, mod3, gain.reshape(1, d), w)


def _load_perm(ref, base):
    return [ref[pl.ds(base + PERM_GROUP * c + a, SUBLANES, stride=PERM_STRIDE), :]
            for c in range(N_GROUPS) for a in range(PERM_STRIDE)]


def _store_perm(ref, base, val):
    for c in range(N_GROUPS):
        for a in range(PERM_STRIDE):
            u = PERM_STRIDE * c + a
            ref[pl.ds(base + PERM_GROUP * c + a, SUBLANES, stride=PERM_STRIDE), :] = (
                val[SUBLANES * u:SUBLANES * (u + 1), :])


def _shift_down(v, d, sub_iota):
    return jnp.where(sub_iota >= d, pltpu.roll(v, d, 0), 0.0)


def _cumsum_perm(g, sub_iota):
    out = []
    carry = None
    for c in range(N_GROUPS):
        pre = [g[PERM_STRIDE * c]]
        for a in range(1, PERM_STRIDE):
            pre.append(pre[-1] + g[PERM_STRIDE * c + a])
        inc = pre[-1]
        inc = inc + _shift_down(inc, 1, sub_iota)
        inc = inc + _shift_down(inc, 2, sub_iota)
        inc = inc + _shift_down(inc, 4, sub_iota)
        exc = _shift_down(inc, 1, sub_iota)
        if carry is not None:
            exc = exc + carry
        out.extend(p + exc for p in pre)
        tot = inc[SUBLANES - 1:SUBLANES, :]
        carry = tot if carry is None else carry + tot
    return jnp.concatenate(out, axis=0), carry


def _perm_token_ids(shape, dim):
    r = lax.broadcasted_iota(jnp.int32, shape, dim)
    u = r >> 3
    return PERM_GROUP * (u >> 2) + PERM_STRIDE * (r & 7) + (u & 3)


def _finish(o, gain, gate):
    y = o * lax.rsqrt(jnp.mean(o * o, axis=-1, keepdims=True) + EPS) * gain
    return y * _silu(gate)


def _rec_kernel(hq0, hq1, hf0, hf1, hi0, hi1, hg0, hg1, gq, gk, gv0, gv1, gg0, gg1, glow,
                lbl_ref, hgn_ref, glan_ref, wup_ref, bup_ref,
                yh_ref, yg_ref,
                q_s, k_s, g_s, st_s, fb_s, oi_s, *, tt, layer):
    nc = tt // REC_CHUNK
    nb = hq0.shape[0]

    @pl.when(pl.program_id(1) == 0)
    def _():
        st_s[...] = jnp.zeros_like(st_s)

    def lower_bound(e):
        lbl = lbl_ref[:, e * LANES:(e + 1) * LANES]
        ex = jnp.exp(lbl - jnp.max(lbl, axis=0, keepdims=True))
        return jnp.sum(ex[:layer + 1], axis=0, keepdims=True) / jnp.sum(ex, axis=0, keepdims=True)

    def stats(q, k, g):
        tot = jnp.sum(g.reshape(nc, REC_CHUNK, LANES), axis=1)
        return jnp.min(tot), jnp.max(jnp.maximum(jnp.abs(q), jnp.abs(k)))

    mins, maxs = [], []
    wup = wup_ref[...].astype(BF16)
    for bi in range(nb):
        for e, (hq, hf) in enumerate(((hq0, hf0), (hq1, hf1))):
            q = _silu(hq[bi])
            lb = lower_bound(e)
            forget = lb + (1.0 - lb) * jax.nn.sigmoid(hf[bi])
            g = jnp.log(forget)
            k = 1.0 - forget
            q_s[bi, e] = q
            k_s[bi, e] = k
            g_s[bi, e] = g
            mn, mx = stats(q, k, g)
            mins.append(mn)
            maxs.append(mx)
        logits = _dot(glow[bi].astype(BF16), wup) + bup_ref[...]
        g = (jnp.minimum(logits, 0.0) - jnp.log(1.0 + jnp.exp(-jnp.abs(logits)))) * (1.0 / GLA_NORMALIZER)
        q = gq[bi] * (GLA_DK ** -0.5)
        k = gk[bi]
        q_s[bi, 2] = q
        k_s[bi, 2] = k
        g_s[bi, 2] = g
        mn, mx = stats(q, k, g)
        mins.append(mn)
        maxs.append(mx)
    min_tot = functools.reduce(jnp.minimum, mins)
    max_mag = functools.reduce(jnp.maximum, maxs)
    safe = jnp.logical_and(min_tot >= -2.0 * SAFE_EXP, max_mag <= SAFE_MAG)

    sub_iota = lax.broadcasted_iota(jnp.int32, (SUBLANES, LANES), 0)
    lane = lax.broadcasted_iota(jnp.int32, (REC_CHUNK, LANES), 1)
    tok_row = _perm_token_ids((REC_CHUNK, REC_CHUNK), 0)
    tok_col = _perm_token_ids((REC_CHUNK, REC_CHUNK), 1)
    causal = tok_col <= tok_row
    tok_of_row = _perm_token_ids((REC_CHUNK, LANES), 0)
    hgn = hgn_ref[...]
    glan = glan_ref[...]

    bodies = []
    for bi in range(nb):
        bodies += [
            ((bi, 0), None, hi0.at[bi], hg0.at[bi], hgn, yh_ref.at[0, bi], st_s.at[bi, 0]),
            ((bi, 1), None, hi1.at[bi], hg1.at[bi], hgn, yh_ref.at[1, bi], st_s.at[bi, 1]),
            ((bi, 2), lane < GLA_DK, gv0.at[bi], gg0.at[bi], glan, yg_ref.at[0, bi], st_s.at[bi, 2]),
            ((bi, 2), lane >= GLA_DK, gv1.at[bi], gg1.at[bi], glan, yg_ref.at[1, bi], st_s.at[bi, 3]),
        ]

    def load_qkb(slot, base):
        b, b_last = _cumsum_perm(_load_perm(g_s.at[slot], base), sub_iota)
        q = jnp.concatenate(_load_perm(q_s.at[slot], base), axis=0)
        k = jnp.concatenate(_load_perm(k_s.at[slot], base), axis=0)
        return q, k, b, b_last

    def fast_chunk(ci, carry):
        base = pl.multiple_of(ci * REC_CHUNK, REC_CHUNK)
        shared = {}
        staged = []
        for slot, qmask, v_ref, gate_ref, gain, y_view, st_view in bodies:
            if slot not in shared:
                q, k, b, b_last = load_qkb(slot, base)
                half = 0.5 * b_last
                d = b - half
                em = jnp.exp(half)
                qt = q * jnp.exp(d)
                kt = k * jnp.exp(-d)
                shared[slot] = (qt.astype(BF16), (qt * em).astype(BF16), kt.astype(BF16),
                                (kt * em).astype(BF16), em * em)
            qt_b, qi_b, kt_b, kd_b, dec = shared[slot]
            if qmask is not None:
                qt_b = jnp.where(qmask, qt_b, jnp.zeros_like(qt_b))
                qi_b = jnp.where(qmask, qi_b, jnp.zeros_like(qi_b))
            vb = jnp.concatenate(_load_perm(v_ref, base), axis=0).astype(BF16)
            state = st_view[...]
            scores = _dot_nt(qt_b, kt_b)
            o_inter = _dot_nt(qi_b, state.astype(BF16))
            st_view[...] = state * dec + _dot_tn(vb, kd_b)
            staged.append((scores, o_inter, vb))
        for (scores, o_inter, vb), (_, _, _, gate_ref, gain, y_view, _) in zip(staged, bodies):
            s = jnp.where(causal, scores, 0.0).astype(BF16)
            o = o_inter + _dot(s, vb)
            gate = jnp.concatenate(_load_perm(gate_ref, base), axis=0)
            _store_perm(y_view, base, _finish(o, gain, gate))
        return carry

    def exact_chunk(ci, carry):
        base = pl.multiple_of(ci * REC_CHUNK, REC_CHUNK)
        for slot, qmask, v_ref, gate_ref, gain, y_view, st_view in bodies:
            q, k, b, b_last = load_qkb(slot, base)
            if qmask is not None:
                q = jnp.where(qmask, q, 0.0)
            v = jnp.concatenate(_load_perm(v_ref, base), axis=0)
            gate = jnp.concatenate(_load_perm(gate_ref, base), axis=0)
            fb_s[0] = b
            fb_s[1] = q
            fb_s[2] = k
            fb_s[3] = v

            def row(r, c2):
                u = r >> 3
                tr = PERM_GROUP * (u >> 2) + PERM_STRIDE * (r & 7) + (u & 3)
                bt = fb_s[0, pl.ds(r, 1), :]
                qt = fb_s[1, pl.ds(r, 1), :]
                w = jnp.exp(jnp.where(tok_of_row <= tr, bt - fb_s[0], -jnp.inf))
                sc = jnp.sum(qt * w * fb_s[2], axis=-1, keepdims=True)
                oi_s[pl.ds(r, 1), :] = jnp.sum(sc * fb_s[3], axis=0, keepdims=True)
                return c2

            lax.fori_loop(0, REC_CHUNK, row, 0)
            state = st_view[...]
            vb = v.astype(BF16)
            o = _dot_nt((q * jnp.exp(b)).astype(BF16), state.astype(BF16)) + oi_s[...]
            kd_b = (k * jnp.exp(b_last - b)).astype(BF16)
            st_view[...] = state * jnp.exp(b_last) + _dot_tn(vb, kd_b)
            _store_perm(y_view, base, _finish(o, gain, gate))
        return carry

    @pl.when(safe)
    def _():
        lax.fori_loop(0, nc, fast_chunk, 0)

    @pl.when(jnp.logical_not(safe))
    def _():
        lax.fori_loop(0, nc, exact_chunk, 0)


def _recurrence(mix, lb_logits, hg_norm, gla_norm, wup_pad, bup, *, batch, tokens_per_batch, layer, tt=256):
    n, n_cols = mix.shape
    nt = tokens_per_batch // tt
    pairs = HG_HEADS // 2
    mix3 = mix.reshape(batch, tokens_per_batch, n_cols)
    off = {"hq": 0, "hf": 4, "hi": 8, "hg": 12, "gq": 16, "gk": 18, "gv": 20, "gg": 24, "gl": 28}

    def head(name, e):
        return pl.BlockSpec((batch, tt, LANES), lambda p, i: (0, i, off[name] + 2 * p + e))

    def pair(name):
        return pl.BlockSpec((batch, tt, LANES), lambda p, i: (0, i, off[name] + p))

    in_specs = [
        head("hq", 0), head("hq", 1), head("hf", 0), head("hf", 1),
        head("hi", 0), head("hi", 1), head("hg", 0), head("hg", 1),
        pair("gq"), pair("gk"),
        head("gv", 0), head("gv", 1), head("gg", 0), head("gg", 1),
        pl.BlockSpec((batch, tt, LANES), lambda p, i: (0, i, off["gl"])),
    ]
    in_specs_params = [
        pl.BlockSpec((lb_logits.shape[0], 2 * LANES), lambda p, i: (0, p)),
        pl.BlockSpec((1, LANES), lambda p, i: (0, 0)),
        pl.BlockSpec((1, LANES), lambda p, i: (0, 0)),
        pl.BlockSpec((LANES, LANES), lambda p, i: (0, p)),
        pl.BlockSpec((1, LANES), lambda p, i: (0, p)),
    ]
    out_spec = pl.BlockSpec((2, batch, tt, LANES), lambda p, i: (p, 0, i, 0))
    y_hg, y_gla = pl.pallas_call(
        functools.partial(_rec_kernel, tt=tt, layer=layer),
        grid=(pairs, nt),
        in_specs=in_specs + in_specs_params,
        out_specs=[out_spec, out_spec],
        out_shape=[jax.ShapeDtypeStruct((HG_HEADS, batch, tokens_per_batch, HG_DV), F32),
                   jax.ShapeDtypeStruct((GLA_HEADS, batch, tokens_per_batch, GLA_DV), F32)],
        scratch_shapes=[
            pltpu.VMEM((batch, 3, tt, LANES), F32),
            pltpu.VMEM((batch, 3, tt, LANES), F32),
            pltpu.VMEM((batch, 3, tt, LANES), F32),
            pltpu.VMEM((batch, 4, LANES, LANES), F32),
            pltpu.VMEM((4, REC_CHUNK, LANES), F32),
            pltpu.VMEM((REC_CHUNK, LANES), F32),
        ],
        compiler_params=pltpu.CompilerParams(
            dimension_semantics=("arbitrary", "arbitrary"), vmem_limit_bytes=48 * MIB),
        name="recurrence",
    )(*([mix3] * 15), lb_logits, hg_norm.reshape(1, HG_DV), gla_norm.reshape(1, GLA_DV), wup_pad,
      bup.reshape(1, -1))
    return y_hg.reshape(HG_HEADS, n, HG_DV), y_gla.reshape(GLA_HEADS, n, GLA_DV)


def _merge_kernel(x_ref, mod_ref, yh_ref, yg_ref, rh_ref, rg_ref, wuh_ref, wug_ref, wo_ref, o_ref, *, sub):
    gate = mod_ref[0, 3 * sub + 2:3 * sub + 3, :]

    def up(y_ref, w_ref):
        acc = None
        for h in range(y_ref.shape[0]):
            part = _dot(y_ref[h].astype(BF16), w_ref[h * LANES:(h + 1) * LANES, :])
            acc = part if acc is None else acc + part
        return acc

    merged = (jax.nn.sigmoid(rh_ref[...]) * up(yh_ref, wuh_ref)
              + jax.nn.sigmoid(rg_ref[...]) * up(yg_ref, wug_ref))
    o_ref[...] = x_ref[...] + gate * _dot(merged.astype(BF16), wo_ref[...])


def _merge(x2d, mod3, y_hg, y_gla, r, w_up_hg, w_up_gla, w_out, *, sub, tokens_per_batch, tm=512):
    n, d = x2d.shape
    tiles_per_batch = tokens_per_batch // tm
    const = dict(pipeline_mode=pl.Buffered(1))
    return pl.pallas_call(
        functools.partial(_merge_kernel, sub=sub),
        grid=(n // tm,),
        in_specs=[
            pl.BlockSpec((tm, d), lambda i: (i, 0)),
            pl.BlockSpec((1, 3 * N_SUB, d), lambda i: (i // tiles_per_batch, 0, 0)),
            pl.BlockSpec((HG_HEADS, tm, HG_DV), lambda i: (0, i, 0)),
            pl.BlockSpec((GLA_HEADS, tm, GLA_DV), lambda i: (0, i, 0)),
            pl.BlockSpec((tm, d), lambda i: (i, 0)),
            pl.BlockSpec((tm, d), lambda i: (i, 1)),
            pl.BlockSpec(w_up_hg.shape, lambda i: (0, 0), **const),
            pl.BlockSpec(w_up_gla.shape, lambda i: (0, 0), **const),
            pl.BlockSpec(w_out.shape, lambda i: (0, 0), **const),
        ],
        out_specs=pl.BlockSpec((tm, d), lambda i: (i, 0)),
        out_shape=jax.ShapeDtypeStruct((n, d), F32),
        compiler_params=pltpu.CompilerParams(vmem_limit_bytes=48 * MIB),
        name="merge_out",
    )(x2d, mod3, y_hg, y_gla, r, r, w_up_hg, w_up_gla, w_out)


def kernel(x, c, w_ada, b_ada, norm_gains, ffn1_w_in, ffn1_w_down, w_in_mix, w_gk_up, b_gk_up, lb_logits,
           hg_norm, gla_norm, w_up_hg, w_up_gla, w_out, ffn2_w_in, ffn2_w_down, final_norm):
    batch, seq, d = x.shape
    depth = w_ada.shape[0]
    hg_qk = HG_HEADS * HG_DK
    hg_w = HG_HEADS * HG_DV
    gla_k = GLA_HEADS * GLA_DK
    gla_v = GLA_HEADS * GLA_DV
    n_rec = 2 * hg_qk + 2 * hg_w + 2 * gla_k + 2 * gla_v
    n_mix = n_rec + LANES

    x2d = x.reshape(batch * seq, d)
    c_pad = jnp.pad(c, ((0, SUBLANES - batch % SUBLANES if batch % SUBLANES else 0), (0, 0)))
    for l in range(depth):
        mod = _ada(c_pad, w_ada[l], b_ada[l])[:batch].reshape(batch, 3 * N_SUB, d)

        x2d = _ffn(x2d, mod, norm_gains[l, 0], ffn1_w_in[l].astype(BF16), ffn1_w_down[l].astype(BF16),
                   final_norm, sub=0, tokens_per_batch=seq, final=False)

        w_mix = w_in_mix[l]
        w_r = jnp.concatenate([
            w_mix[:, :n_rec],
            w_mix[:, n_rec:n_rec + GLA_RANK],
            jnp.zeros((d, LANES - GLA_RANK), w_mix.dtype),
            w_mix[:, n_rec + GLA_RANK:],
        ], axis=1).astype(BF16)
        mix, r = _inproj(x2d, mod, norm_gains[l, 1], w_r, n_mix, sub=1, tokens_per_batch=seq)

        wup_pad = jnp.pad(w_gk_up[l], ((0, LANES - GLA_RANK), (0, 0)))
        y_hg, y_gla = _recurrence(mix, lb_logits, hg_norm[l], gla_norm[l], wup_pad, b_gk_up[l],
                                  batch=batch, tokens_per_batch=seq, layer=l)

        x2d = _merge(x2d, mod, y_hg, y_gla, r, w_up_hg[l].astype(BF16), w_up_gla[l].astype(BF16),
                     w_out[l].astype(BF16), sub=1, tokens_per_batch=seq)

        last = l == depth - 1
        x2d = _ffn(x2d, mod, norm_gains[l, 2], ffn2_w_in[l].astype(BF16), ffn2_w_down[l].astype(BF16),
                   final_norm, sub=2, tokens_per_batch=seq, final=last)
    if depth == 0:
        raise ValueError("depth must be >= 1")
    return x2d.reshape(batch, seq, d)
```

```python
import functools

import jax
import jax.numpy as jnp
from jax import lax
from jax.experimental import pallas as pl
from jax.experimental.pallas import tpu as pltpu

F32 = jnp.float32
BF16 = jnp.bfloat16

EPS = 1e-6
N_SUB = 3
HG_HEADS = 4
HG_DK = 128
HG_DV = 128
GLA_HEADS = 4
GLA_DK = 64
GLA_DV = 128
GLA_RANK = 16
GLA_NORMALIZER = 16.0

LANES = 128
SUBLANES = 8
MXU_WIDTH = 256
MIB = 1024 * 1024

REC_CHUNK = 64
PERM_STRIDE = 4
PERM_GROUP = PERM_STRIDE * SUBLANES
N_GROUPS = REC_CHUNK // PERM_GROUP
SAFE_EXP = 70.0
SAFE_MAG = 1e7


def _dot(a, b):
    return jnp.dot(a, b, preferred_element_type=F32)


def _dot_nt(a, b):
    return lax.dot_general(a, b, (((1,), (1,)), ((), ())), preferred_element_type=F32)


def _dot_tn(a, b):
    return lax.dot_general(a, b, (((0,), (0,)), ((), ())), preferred_element_type=F32)


def _silu(v):
    return v * jax.nn.sigmoid(v)


def _norm_mod(x, gain, shift, scale):
    y = x * lax.rsqrt(jnp.mean(x * x, axis=-1, keepdims=True) + EPS) * gain
    return y * (1.0 + scale) + shift


def _ada_kernel(c_ref, w_ref, b_ref, o_ref):
    cond = _silu(c_ref[...]).astype(BF16)
    o_ref[...] = _dot(cond, w_ref[...].astype(BF16)) + b_ref[...]


def _ada(c_pad, w, b):
    rows, d = c_pad.shape
    n = w.shape[1]
    bn = n // 8
    return pl.pallas_call(
        _ada_kernel,
        grid=(n // bn,),
        in_specs=[
            pl.BlockSpec((rows, d), lambda j: (0, 0)),
            pl.BlockSpec((d, bn), lambda j: (0, j)),
            pl.BlockSpec((1, bn), lambda j: (0, j)),
        ],
        out_specs=pl.BlockSpec((rows, bn), lambda j: (0, j)),
        out_shape=jax.ShapeDtypeStruct((rows, n), F32),
        name="adaln",
    )(c_pad, w, b.reshape(1, n))


def _mxu_aligned_bounds(width, parts):
    if width % MXU_WIDTH:
        return (0, width)
    tiles = width // MXU_WIDTH
    return tuple(MXU_WIDTH * ((tiles * p + parts - 1) // parts) for p in range(parts)) + (width,)


def _ffn_kernel(x_ref, mod_ref, gain_ref, w_in_ref, w_down_ref, fgain_ref, o_ref, *, sub, d_ff, bounds, final):
    x = x_ref[...]
    shift = mod_ref[0, 3 * sub + 0:3 * sub + 1, :]
    scale = mod_ref[0, 3 * sub + 1:3 * sub + 2, :]
    gate = mod_ref[0, 3 * sub + 2:3 * sub + 3, :]
    hb = _norm_mod(x, gain_ref[...], shift, scale).astype(BF16)
    acc = None
    for lo, hi in zip(bounds[:-1], bounds[1:]):
        g = _dot(hb, w_in_ref[:, lo:hi])
        u = _dot(hb, w_in_ref[:, d_ff + lo:d_ff + hi])
        act = (_silu(g) * u).astype(BF16)
        part = _dot(act, w_down_ref[lo:hi, :])
        acc = part if acc is None else acc + part
    xn = x + (0.5 * gate) * acc
    if final:
        xn = xn * lax.rsqrt(jnp.mean(xn * xn, axis=-1, keepdims=True) + EPS) * fgain_ref[...]
    o_ref[...] = xn


def _ffn(x2d, mod3, gain, w_in, w_down, fgain, *, sub, tokens_per_batch, final, tm=256):
    n, d = x2d.shape
    d_ff = w_down.shape[0]
    tiles_per_batch = tokens_per_batch // tm
    bounds = _mxu_aligned_bounds(d_ff, 2)
    const = dict(pipeline_mode=pl.Buffered(1))
    return pl.pallas_call(
        functools.partial(_ffn_kernel, sub=sub, d_ff=d_ff, bounds=bounds, final=final),
        grid=(n // tm,),
        in_specs=[
            pl.BlockSpec((tm, d), lambda i: (i, 0)),
            pl.BlockSpec((1, 3 * N_SUB, d), lambda i: (i // tiles_per_batch, 0, 0)),
            pl.BlockSpec((1, d), lambda i: (0, 0)),
            pl.BlockSpec((d, 2 * d_ff), lambda i: (0, 0), **const),
            pl.BlockSpec((d_ff, d), lambda i: (0, 0), **const),
            pl.BlockSpec((1, d), lambda i: (0, 0)),
        ],
        out_specs=pl.BlockSpec((tm, d), lambda i: (i, 0)),
        out_shape=jax.ShapeDtypeStruct((n, d), F32),
        compiler_params=pltpu.CompilerParams(vmem_limit_bytes=48 * MIB),
        name="ffn_final" if final else "ffn",
    )(x2d, mod3, gain.reshape(1, d), w_in, w_down, fgain.reshape(1, d))


def _inproj_kernel(x_ref, mod_ref, gain_ref, w_ref, mix_ref, r_ref, *, sub, n_mix):
    shift = mod_ref[0, 3 * sub + 0:3 * sub + 1, :]
    scale = mod_ref[0, 3 * sub + 1:3 * sub + 2, :]
    hb = _norm_mod(x_ref[...], gain_ref[...], shift, scale).astype(BF16)
    mix_ref[...] = _dot(hb, w_ref[:, :n_mix])
    r_ref[...] = _dot(hb, w_ref[:, n_mix:])


def _inproj(x2d, mod3, gain, w, n_mix, *, sub, tokens_per_batch, tm=256):
    n, d = x2d.shape
    n_all = w.shape[1]
    tiles_per_batch = tokens_per_batch // tm
    return pl.pallas_call(
        functools.partial(_inproj_kernel, sub=sub, n_mix=n_mix),
        grid=(n // tm,),
        in_specs=[
            pl.BlockSpec((tm, d), lambda i: (i, 0)),
            pl.BlockSpec((1, 3 * N_SUB, d), lambda i: (i // tiles_per_batch, 0, 0)),
            pl.BlockSpec((1, d), lambda i: (0, 0)),
            pl.BlockSpec((d, n_all), lambda i: (0, 0), pipeline_mode=pl.Buffered(1)),
        ],
        out_specs=[
            pl.BlockSpec((tm, n_mix), lambda i: (i, 0)),
            pl.BlockSpec((tm, n_all - n_mix), lambda i: (i, 0)),
        ],
        out_shape=[
            jax.ShapeDtypeStruct((n, n_mix), F32),
            jax.ShapeDtypeStruct((n, n_all - n_mix), F32),
        ],
        compiler_params=pltpu.CompilerParams(vmem_limit_bytes=48 * MIB),
        name="inproj",
    )(x2d, mod3, gain.reshape(1, d), w)


def _load_perm(ref, base):
    return [ref[pl.ds(base + PERM_GROUP * c + a, SUBLANES, stride=PERM_STRIDE), :]
            for c in range(N_GROUPS) for a in range(PERM_STRIDE)]


def _store_perm(ref, base, val):
    for c in range(N_GROUPS):
        for a in range(PERM_STRIDE):
            u = PERM_STRIDE * c + a
            ref[pl.ds(base + PERM_GROUP * c + a, SUBLANES, stride=PERM_STRIDE), :] = (
                val[SUBLANES * u:SUBLANES * (u + 1), :])


def _shift_down(v, d, sub_iota):
    return jnp.where(sub_iota >= d, pltpu.roll(v, d, 0), 0.0)


def _cumsum_perm(g, sub_iota):
    out = []
    carry = None
    for c in range(N_GROUPS):
        pre = [g[PERM_STRIDE * c]]
        for a in range(1, PERM_STRIDE):
            pre.append(pre[-1] + g[PERM_STRIDE * c + a])
        inc = pre[-1]
        inc = inc + _shift_down(inc, 1, sub_iota)
        inc = inc + _shift_down(inc, 2, sub_iota)
        inc = inc + _shift_down(inc, 4, sub_iota)
        exc = _shift_down(inc, 1, sub_iota)
        if carry is not None:
            exc = exc + carry
        out.extend(p + exc for p in pre)
        tot = inc[SUBLANES - 1:SUBLANES, :]
        carry = tot if carry is None else carry + tot
    return jnp.concatenate(out, axis=0), carry


def _perm_token_ids(shape, dim):
    r = lax.broadcasted_iota(jnp.int32, shape, dim)
    u = r >> 3
    return PERM_GROUP * (u >> 2) + PERM_STRIDE * (r & 7) + (u & 3)


def _finish(o, gain, gate):
    y = o * lax.rsqrt(jnp.mean(o * o, axis=-1, keepdims=True) + EPS) * gain
    return y * _silu(gate)


def _rec_kernel(hq0, hq1, hf0, hf1, hi0, hi1, hg0, hg1, gq, gk, gv0, gv1, gg0, gg1, glow,
                lbl_ref, hgn_ref, glan_ref, wup_ref, bup_ref,
                yh_ref, yg_ref,
                q_s, k_s, g_s, st_s, fb_s, oi_s, *, tt, layer):
    nc = tt // REC_CHUNK
    nb = hq0.shape[0]

    @pl.when(pl.program_id(1) == 0)
    def _():
        st_s[...] = jnp.zeros_like(st_s)

    def lower_bound(e):
        lbl = lbl_ref[:, e * LANES:(e + 1) * LANES]
        ex = jnp.exp(lbl - jnp.max(lbl, axis=0, keepdims=True))
        return jnp.sum(ex[:layer + 1], axis=0, keepdims=True) / jnp.sum(ex, axis=0, keepdims=True)

    def stats(q, k, g):
        tot = jnp.sum(g.reshape(nc, REC_CHUNK, LANES), axis=1)
        return jnp.min(tot), jnp.max(jnp.maximum(jnp.abs(q), jnp.abs(k)))

    mins, maxs = [], []
    wup = wup_ref[...].astype(BF16)
    for bi in range(nb):
        for e, (hq, hf) in enumerate(((hq0, hf0), (hq1, hf1))):
            q = _silu(hq[bi])
            lb = lower_bound(e)
            forget = lb + (1.0 - lb) * jax.nn.sigmoid(hf[bi])
            g = jnp.log(forget)
            k = 1.0 - forget
            q_s[bi, e] = q
            k_s[bi, e] = k
            g_s[bi, e] = g
            mn, mx = stats(q, k, g)
            mins.append(mn)
            maxs.append(mx)
        logits = _dot(glow[bi].astype(BF16), wup) + bup_ref[...]
        g = (jnp.minimum(logits, 0.0) - jnp.log(1.0 + jnp.exp(-jnp.abs(logits)))) * (1.0 / GLA_NORMALIZER)
        q = gq[bi] * (GLA_DK ** -0.5)
        k = gk[bi]
        q_s[bi, 2] = q
        k_s[bi, 2] = k
        g_s[bi, 2] = g
        mn, mx = stats(q, k, g)
        mins.append(mn)
        maxs.append(mx)
    min_tot = functools.reduce(jnp.minimum, mins)
    max_mag = functools.reduce(jnp.maximum, maxs)
    safe = jnp.logical_and(min_tot >= -2.0 * SAFE_EXP, max_mag <= SAFE_MAG)

    sub_iota = lax.broadcasted_iota(jnp.int32, (SUBLANES, LANES), 0)
    lane = lax.broadcasted_iota(jnp.int32, (REC_CHUNK, LANES), 1)
    tok_row = _perm_token_ids((REC_CHUNK, REC_CHUNK), 0)
    tok_col = _perm_token_ids((REC_CHUNK, REC_CHUNK), 1)
    causal = tok_col <= tok_row
    tok_of_row = _perm_token_ids((REC_CHUNK, LANES), 0)
    hgn = hgn_ref[...]
    glan = glan_ref[...]

    bodies = []
    for bi in range(nb):
        bodies += [
            ((bi, 0), None, hi0.at[bi], hg0.at[bi], hgn, yh_ref.at[0, bi], st_s.at[bi, 0]),
            ((bi, 1), None, hi1.at[bi], hg1.at[bi], hgn, yh_ref.at[1, bi], st_s.at[bi, 1]),
            ((bi, 2), lane < GLA_DK, gv0.at[bi], gg0.at[bi], glan, yg_ref.at[0, bi], st_s.at[bi, 2]),
            ((bi, 2), lane >= GLA_DK, gv1.at[bi], gg1.at[bi], glan, yg_ref.at[1, bi], st_s.at[bi, 3]),
        ]

    def load_qkb(slot, base):
        b, b_last = _cumsum_perm(_load_perm(g_s.at[slot], base), sub_iota)
        q = jnp.concatenate(_load_perm(q_s.at[slot], base), axis=0)
        k = jnp.concatenate(_load_perm(k_s.at[slot], base), axis=0)
        return q, k, b, b_last

    def fast_chunk(ci, carry):
        base = pl.multiple_of(ci * REC_CHUNK, REC_CHUNK)
        shared = {}
        staged = []
        for slot, qmask, v_ref, gate_ref, gain, y_view, st_view in bodies:
            if slot not in shared:
                q, k, b, b_last = load_qkb(slot, base)
                half = 0.5 * b_last
                d = b - half
                em = jnp.exp(half)
                qt = q * jnp.exp(d)
                kt = k * jnp.exp(-d)
                shared[slot] = (qt.astype(BF16), (qt * em).astype(BF16), kt.astype(BF16),
                                (kt * em).astype(BF16), em * em)
            qt_b, qi_b, kt_b, kd_b, dec = shared[slot]
            if qmask is not None:
                qt_b = jnp.where(qmask, qt_b, jnp.zeros_like(qt_b))
                qi_b = jnp.where(qmask, qi_b, jnp.zeros_like(qi_b))
            vb = jnp.concatenate(_load_perm(v_ref, base), axis=0).astype(BF16)
            state = st_view[...]
            scores = _dot_nt(qt_b, kt_b)
            o_inter = _dot_nt(qi_b, state.astype(BF16))
            st_view[...] = state * dec + _dot_tn(vb, kd_b)
            staged.append((scores, o_inter, vb))
        for (scores, o_inter, vb), (_, _, _, gate_ref, gain, y_view, _) in zip(staged, bodies):
            s = jnp.where(causal, scores, 0.0).astype(BF16)
            o = o_inter + _dot(s, vb)
            gate = jnp.concatenate(_load_perm(gate_ref, base), axis=0)
            _store_perm(y_view, base, _finish(o, gain, gate))
        return carry

    def exact_chunk(ci, carry):
        base = pl.multiple_of(ci * REC_CHUNK, REC_CHUNK)
        for slot, qmask, v_ref, gate_ref, gain, y_view, st_view in bodies:
            q, k, b, b_last = load_qkb(slot, base)
            if qmask is not None:
                q = jnp.where(qmask, q, 0.0)
            v = jnp.concatenate(_load_perm(v_ref, base), axis=0)
            gate = jnp.concatenate(_load_perm(gate_ref, base), axis=0)
            fb_s[0] = b
            fb_s[1] = q
            fb_s[2] = k
            fb_s[3] = v

            def row(r, c2):
                u = r >> 3
                tr = PERM_GROUP * (u >> 2) + PERM_STRIDE * (r & 7) + (u & 3)
                bt = fb_s[0, pl.ds(r, 1), :]
                qt = fb_s[1, pl.ds(r, 1), :]
                w = jnp.exp(jnp.where(tok_of_row <= tr, bt - fb_s[0], -jnp.inf))
                sc = jnp.sum(qt * w * fb_s[2], axis=-1, keepdims=True)
                oi_s[pl.ds(r, 1), :] = jnp.sum(sc * fb_s[3], axis=0, keepdims=True)
                return c2

            lax.fori_loop(0, REC_CHUNK, row, 0)
            state = st_view[...]
            vb = v.astype(BF16)
            o = _dot_nt((q * jnp.exp(b)).astype(BF16), state.astype(BF16)) + oi_s[...]
            kd_b = (k * jnp.exp(b_last - b)).astype(BF16)
            st_view[...] = state * jnp.exp(b_last) + _dot_tn(vb, kd_b)
            _store_perm(y_view, base, _finish(o, gain, gate))
        return carry

    @pl.when(safe)
    def _():
        lax.fori_loop(0, nc, fast_chunk, 0)

    @pl.when(jnp.logical_not(safe))
    def _():
        lax.fori_loop(0, nc, exact_chunk, 0)


def _recurrence(mix, lb_logits, hg_norm, gla_norm, wup_pad, bup, *, batch, tokens_per_batch, layer, tt=256):
    n, n_cols = mix.shape
    nt = tokens_per_batch // tt
    pairs = HG_HEADS // 2
    mix3 = mix.reshape(batch, tokens_per_batch, n_cols)
    off = {"hq": 0, "hf": 4, "hi": 8, "hg": 12, "gq": 16, "gk": 18, "gv": 20, "gg": 24, "gl": 28}

    def head(name, e):
        return pl.BlockSpec((batch, tt, LANES), lambda p, i: (0, i, off[name] + 2 * p + e))

    def pair(name):
        return pl.BlockSpec((batch, tt, LANES), lambda p, i: (0, i, off[name] + p))

    in_specs = [
        head("hq", 0), head("hq", 1), head("hf", 0), head("hf", 1),
        head("hi", 0), head("hi", 1), head("hg", 0), head("hg", 1),
        pair("gq"), pair("gk"),
        head("gv", 0), head("gv", 1), head("gg", 0), head("gg", 1),
        pl.BlockSpec((batch, tt, LANES), lambda p, i: (0, i, off["gl"])),
    ]
    in_specs_params = [
        pl.BlockSpec((lb_logits.shape[0], 2 * LANES), lambda p, i: (0, p)),
        pl.BlockSpec((1, LANES), lambda p, i: (0, 0)),
        pl.BlockSpec((1, LANES), lambda p, i: (0, 0)),
        pl.BlockSpec((LANES, LANES), lambda p, i: (0, p)),
        pl.BlockSpec((1, LANES), lambda p, i: (0, p)),
    ]
    out_spec = pl.BlockSpec((2, batch, tt, LANES), lambda p, i: (p, 0, i, 0))
    y_hg, y_gla = pl.pallas_call(
        functools.partial(_rec_kernel, tt=tt, layer=layer),
        grid=(pairs, nt),
        in_specs=in_specs + in_specs_params,
        out_specs=[out_spec, out_spec],
        out_shape=[jax.ShapeDtypeStruct((HG_HEADS, batch, tokens_per_batch, HG_DV), F32),
                   jax.ShapeDtypeStruct((GLA_HEADS, batch, tokens_per_batch, GLA_DV), F32)],
        scratch_shapes=[
            pltpu.VMEM((batch, 3, tt, LANES), F32),
            pltpu.VMEM((batch, 3, tt, LANES), F32),
            pltpu.VMEM((batch, 3, tt, LANES), F32),
            pltpu.VMEM((batch, 4, LANES, LANES), F32),
            pltpu.VMEM((4, REC_CHUNK, LANES), F32),
            pltpu.VMEM((REC_CHUNK, LANES), F32),
        ],
        compiler_params=pltpu.CompilerParams(
            dimension_semantics=("arbitrary", "arbitrary"), vmem_limit_bytes=48 * MIB),
        name="recurrence",
    )(*([mix3] * 15), lb_logits, hg_norm.reshape(1, HG_DV), gla_norm.reshape(1, GLA_DV), wup_pad,
      bup.reshape(1, -1))
    return y_hg.reshape(HG_HEADS, n, HG_DV), y_gla.reshape(GLA_HEADS, n, GLA_DV)


def _merge_kernel(x_ref, mod_ref, yh_ref, yg_ref, rh_ref, rg_ref, wuh_ref, wug_ref, wo_ref, o_ref, *, sub):
    gate = mod_ref[0, 3 * sub + 2:3 * sub + 3, :]

    def up(y_ref, w_ref):
        y = jnp.concatenate([y_ref[h].astype(BF16) for h in range(y_ref.shape[0])], axis=-1)
        return _dot(y, w_ref[...])

    merged = (jax.nn.sigmoid(rh_ref[...]) * up(yh_ref, wuh_ref)
              + jax.nn.sigmoid(rg_ref[...]) * up(yg_ref, wug_ref))
    o_ref[...] = x_ref[...] + gate * _dot(merged.astype(BF16), wo_ref[...])


def _merge(x2d, mod3, y_hg, y_gla, r, w_up_hg, w_up_gla, w_out, *, sub, tokens_per_batch, tm=512):
    n, d = x2d.shape
    tiles_per_batch = tokens_per_batch // tm
    const = dict(pipeline_mode=pl.Buffered(1))
    return pl.pallas_call(
        functools.partial(_merge_kernel, sub=sub),
        grid=(n // tm,),
        in_specs=[
            pl.BlockSpec((tm, d), lambda i: (i, 0)),
            pl.BlockSpec((1, 3 * N_SUB, d), lambda i: (i // tiles_per_batch, 0, 0)),
            pl.BlockSpec((HG_HEADS, tm, HG_DV), lambda i: (0, i, 0)),
            pl.BlockSpec((GLA_HEADS, tm, GLA_DV), lambda i: (0, i, 0)),
            pl.BlockSpec((tm, d), lambda i: (i, 0)),
            pl.BlockSpec((tm, d), lambda i: (i, 1)),
            pl.BlockSpec(w_up_hg.shape, lambda i: (0, 0), **const),
            pl.BlockSpec(w_up_gla.shape, lambda i: (0, 0), **const),
            pl.BlockSpec(w_out.shape, lambda i: (0, 0), **const),
        ],
        out_specs=pl.BlockSpec((tm, d), lambda i: (i, 0)),
        out_shape=jax.ShapeDtypeStruct((n, d), F32),
        compiler_params=pltpu.CompilerParams(vmem_limit_bytes=48 * MIB),
        name="merge_out",
    )(x2d, mod3, y_hg, y_gla, r, r, w_up_hg, w_up_gla, w_out)


def kernel(x, c, w_ada, b_ada, norm_gains, ffn1_w_in, ffn1_w_down, w_in_mix, w_gk_up, b_gk_up, lb_logits,
           hg_norm, gla_norm, w_up_hg, w_up_gla, w_out, ffn2_w_in, ffn2_w_down, final_norm):
    batch, seq, d = x.shape
    depth = w_ada.shape[0]
    hg_qk = HG_HEADS * HG_DK
    hg_w = HG_HEADS * HG_DV
    gla_k = GLA_HEADS * GLA_DK
    gla_v = GLA_HEADS * GLA_DV
    n_rec = 2 * hg_qk + 2 * hg_w + 2 * gla_k + 2 * gla_v
    n_mix = n_rec + LANES

    x2d = x.reshape(batch * seq, d)
    c_pad = jnp.pad(c, ((0, SUBLANES - batch % SUBLANES if batch % SUBLANES else 0), (0, 0)))
    for l in range(depth):
        mod = _ada(c_pad, w_ada[l], b_ada[l])[:batch].reshape(batch, 3 * N_SUB, d)

        x2d = _ffn(x2d, mod, norm_gains[l, 0], ffn1_w_in[l].astype(BF16), ffn1_w_down[l].astype(BF16),
                   final_norm, sub=0, tokens_per_batch=seq, final=False)

        w_mix = w_in_mix[l]
        w_r = jnp.concatenate([
            w_mix[:, :n_rec],
            w_mix[:, n_rec:n_rec + GLA_RANK],
            jnp.zeros((d, LANES - GLA_RANK), w_mix.dtype),
            w_mix[:, n_rec + GLA_RANK:],
        ], axis=1).astype(BF16)
        mix, r = _inproj(x2d, mod, norm_gains[l, 1], w_r, n_mix, sub=1, tokens_per_batch=seq)

        wup_pad = jnp.pad(w_gk_up[l], ((0, LANES - GLA_RANK), (0, 0)))
        y_hg, y_gla = _recurrence(mix, lb_logits, hg_norm[l], gla_norm[l], wup_pad, b_gk_up[l],
                                  batch=batch, tokens_per_batch=seq, layer=l)

        x2d = _merge(x2d, mod, y_hg, y_gla, r, w_up_hg[l].astype(BF16), w_up_gla[l].astype(BF16),
                     w_out[l].astype(BF16), sub=1, tokens_per_batch=seq)

        last = l == depth - 1
        x2d = _ffn(x2d, mod, norm_gains[l, 2], ffn2_w_in[l].astype(BF16), ffn2_w_down[l].astype(BF16),
                   final_norm, sub=2, tokens_per_batch=seq, final=last)
    if depth == 0:
        raise ValueError("depth must be >= 1")
    return x2d.reshape(batch, seq, d)
```

```python
import functools

import jax
import jax.numpy as jnp
from jax import lax
from jax.experimental import pallas as pl
from jax.experimental.pallas import tpu as pltpu

F32 = jnp.float32
BF16 = jnp.bfloat16

EPS = 1e-6
N_SUB = 3
HG_HEADS = 4
HG_DK = 128
HG_DV = 128
GLA_HEADS = 4
GLA_DK = 64
GLA_DV = 128
GLA_RANK = 16
GLA_NORMALIZER = 16.0

LANES = 128
SUBLANES = 8
MXU_WIDTH = 256
MIB = 1024 * 1024

REC_CHUNK = 64
PERM_STRIDE = 4
PERM_GROUP = PERM_STRIDE * SUBLANES
N_GROUPS = REC_CHUNK // PERM_GROUP
SAFE_EXP = 70.0
SAFE_MAG = 1e7


def _dot(a, b):
    return jnp.dot(a, b, preferred_element_type=F32)


def _dot_nt(a, b):
    return lax.dot_general(a, b, (((1,), (1,)), ((), ())), preferred_element_type=F32)


def _dot_tn(a, b):
    return lax.dot_general(a, b, (((0,), (0,)), ((), ())), preferred_element_type=F32)


def _silu(v):
    return v * jax.nn.sigmoid(v)


def _norm_mod(x, gain, shift, scale):
    y = x * lax.rsqrt(jnp.mean(x * x, axis=-1, keepdims=True) + EPS) * gain
    return y * (1.0 + scale) + shift


def _ada_kernel(c_ref, w_ref, b_ref, o_ref):
    cond = _silu(c_ref[...]).astype(BF16)
    o_ref[...] = _dot(cond, w_ref[...].astype(BF16)) + b_ref[...]


def _ada(c_pad, w, b):
    rows, d = c_pad.shape
    n = w.shape[1]
    bn = n // 8
    return pl.pallas_call(
        _ada_kernel,
        grid=(n // bn,),
        in_specs=[
            pl.BlockSpec((rows, d), lambda j: (0, 0)),
            pl.BlockSpec((d, bn), lambda j: (0, j)),
            pl.BlockSpec((1, bn), lambda j: (0, j)),
        ],
        out_specs=pl.BlockSpec((rows, bn), lambda j: (0, j)),
        out_shape=jax.ShapeDtypeStruct((rows, n), F32),
        name="adaln",
    )(c_pad, w, b.reshape(1, n))


def _mxu_aligned_bounds(width, parts):
    if width % MXU_WIDTH:
        return (0, width)
    tiles = width // MXU_WIDTH
    return tuple(MXU_WIDTH * ((tiles * p + parts - 1) // parts) for p in range(parts)) + (width,)


def _ffn_kernel(x_ref, mod_ref, gain_ref, w_in_ref, w_down_ref, fgain_ref, o_ref, *, sub, d_ff, bounds, final):
    o_ref[...] = _ffn_body(x_ref[...], mod_ref, gain_ref, w_in_ref, w_down_ref, fgain_ref,
                           sub=sub, d_ff=d_ff, bounds=bounds, final=final)


def _ffn_body(x, mod_ref, gain_ref, w_in_ref, w_down_ref, fgain_ref, *, sub, d_ff, bounds, final):
    shift = mod_ref[0, 3 * sub + 0:3 * sub + 1, :]
    scale = mod_ref[0, 3 * sub + 1:3 * sub + 2, :]
    gate = mod_ref[0, 3 * sub + 2:3 * sub + 3, :]
    hb = _norm_mod(x, gain_ref[...], shift, scale).astype(BF16)
    acc = None
    for lo, hi in zip(bounds[:-1], bounds[1:]):
        g = _dot(hb, w_in_ref[:, lo:hi])
        u = _dot(hb, w_in_ref[:, d_ff + lo:d_ff + hi])
        act = (_silu(g) * u).astype(BF16)
        part = _dot(act, w_down_ref[lo:hi, :])
        acc = part if acc is None else acc + part
    xn = x + (0.5 * gate) * acc
    if final:
        xn = xn * lax.rsqrt(jnp.mean(xn * xn, axis=-1, keepdims=True) + EPS) * fgain_ref[...]
    return xn


def _ffn(x2d, mod3, gain, w_in, w_down, fgain, *, sub, tokens_per_batch, final, tm=256):
    n, d = x2d.shape
    d_ff = w_down.shape[0]
    tiles_per_batch = tokens_per_batch // tm
    bounds = _mxu_aligned_bounds(d_ff, 2)
    const = dict(pipeline_mode=pl.Buffered(1))
    return pl.pallas_call(
        functools.partial(_ffn_kernel, sub=sub, d_ff=d_ff, bounds=bounds, final=final),
        grid=(n // tm,),
        in_specs=[
            pl.BlockSpec((tm, d), lambda i: (i, 0)),
            pl.BlockSpec((1, 3 * N_SUB, d), lambda i: (i // tiles_per_batch, 0, 0)),
            pl.BlockSpec((1, d), lambda i: (0, 0)),
            pl.BlockSpec((d, 2 * d_ff), lambda i: (0, 0), **const),
            pl.BlockSpec((d_ff, d), lambda i: (0, 0), **const),
            pl.BlockSpec((1, d), lambda i: (0, 0)),
        ],
        out_specs=pl.BlockSpec((tm, d), lambda i: (i, 0)),
        out_shape=jax.ShapeDtypeStruct((n, d), F32),
        compiler_params=pltpu.CompilerParams(vmem_limit_bytes=48 * MIB),
        name="ffn_final" if final else "ffn",
    )(x2d, mod3, gain.reshape(1, d), w_in, w_down, fgain.reshape(1, d))


def _inproj_kernel(x_ref, mod_ref, gain_ref, w_ref, mix_ref, r_ref, *, sub, n_mix):
    shift = mod_ref[0, 3 * sub + 0:3 * sub + 1, :]
    scale = mod_ref[0, 3 * sub + 1:3 * sub + 2, :]
    hb = _norm_mod(x_ref[...], gain_ref[...], shift, scale).astype(BF16)
    mix_ref[...] = _dot(hb, w_ref[:, :n_mix])
    r_ref[...] = _dot(hb, w_ref[:, n_mix:])


def _inproj(x2d, mod3, gain, w, n_mix, *, sub, tokens_per_batch, tm=256):
    n, d = x2d.shape
    n_all = w.shape[1]
    tiles_per_batch = tokens_per_batch // tm
    return pl.pallas_call(
        functools.partial(_inproj_kernel, sub=sub, n_mix=n_mix),
        grid=(n // tm,),
        in_specs=[
            pl.BlockSpec((tm, d), lambda i: (i, 0)),
            pl.BlockSpec((1, 3 * N_SUB, d), lambda i: (i // tiles_per_batch, 0, 0)),
            pl.BlockSpec((1, d), lambda i: (0, 0)),
            pl.BlockSpec((d, n_all), lambda i: (0, 0), pipeline_mode=pl.Buffered(1)),
        ],
        out_specs=[
            pl.BlockSpec((tm, n_mix), lambda i: (i, 0)),
            pl.BlockSpec((tm, n_all - n_mix), lambda i: (i, 0)),
        ],
        out_shape=[
            jax.ShapeDtypeStruct((n, n_mix), F32),
            jax.ShapeDtypeStruct((n, n_all - n_mix), F32),
        ],
        compiler_params=pltpu.CompilerParams(vmem_limit_bytes=48 * MIB),
        name="inproj",
    )(x2d, mod3, gain.reshape(1, d), w)


def _load_perm(ref, base):
    return [ref[pl.ds(base + PERM_GROUP * c + a, SUBLANES, stride=PERM_STRIDE), :]
            for c in range(N_GROUPS) for a in range(PERM_STRIDE)]


def _store_perm(ref, base, val):
    for c in range(N_GROUPS):
        for a in range(PERM_STRIDE):
            u = PERM_STRIDE * c + a
            ref[pl.ds(base + PERM_GROUP * c + a, SUBLANES, stride=PERM_STRIDE), :] = (
                val[SUBLANES * u:SUBLANES * (u + 1), :])


def _shift_down(v, d, sub_iota):
    return jnp.where(sub_iota >= d, pltpu.roll(v, d, 0), 0.0)


def _cumsum_perm(g, sub_iota):
    out = []
    carry = None
    for c in range(N_GROUPS):
        pre = [g[PERM_STRIDE * c]]
        for a in range(1, PERM_STRIDE):
            pre.append(pre[-1] + g[PERM_STRIDE * c + a])
        inc = pre[-1]
        inc = inc + _shift_down(inc, 1, sub_iota)
        inc = inc + _shift_down(inc, 2, sub_iota)
        inc = inc + _shift_down(inc, 4, sub_iota)
        exc = _shift_down(inc, 1, sub_iota)
        if carry is not None:
            exc = exc + carry
        out.extend(p + exc for p in pre)
        tot = inc[SUBLANES - 1:SUBLANES, :]
        carry = tot if carry is None else carry + tot
    return jnp.concatenate(out, axis=0), carry


def _perm_token_ids(shape, dim):
    r = lax.broadcasted_iota(jnp.int32, shape, dim)
    u = r >> 3
    return PERM_GROUP * (u >> 2) + PERM_STRIDE * (r & 7) + (u & 3)


def _finish(o, gain, gate):
    y = o * lax.rsqrt(jnp.mean(o * o, axis=-1, keepdims=True) + EPS) * gain
    return y * _silu(gate)


def _rec_kernel(hq0, hq1, hf0, hf1, hi0, hi1, hg0, hg1, gq, gk, gv0, gv1, gg0, gg1, glow,
                lbl_ref, hgn_ref, glan_ref, wup_ref, bup_ref,
                yh_ref, yg_ref,
                q_s, k_s, g_s, st_s, fb_s, oi_s, *, tt, layer):
    nc = tt // REC_CHUNK
    nb = hq0.shape[0]

    @pl.when(pl.program_id(1) == 0)
    def _():
        st_s[...] = jnp.zeros_like(st_s)

    def lower_bound(e):
        lbl = lbl_ref[:, e * LANES:(e + 1) * LANES]
        ex = jnp.exp(lbl - jnp.max(lbl, axis=0, keepdims=True))
        return jnp.sum(ex[:layer + 1], axis=0, keepdims=True) / jnp.sum(ex, axis=0, keepdims=True)

    def stats(q, k, g):
        tot = jnp.sum(g.reshape(nc, REC_CHUNK, LANES), axis=1)
        return jnp.min(tot), jnp.max(jnp.maximum(jnp.abs(q), jnp.abs(k)))

    mins, maxs = [], []
    wup = wup_ref[...].astype(BF16)
    for bi in range(nb):
        for e, (hq, hf) in enumerate(((hq0, hf0), (hq1, hf1))):
            q = _silu(hq[bi])
            lb = lower_bound(e)
            forget = lb + (1.0 - lb) * jax.nn.sigmoid(hf[bi])
            g = jnp.log(forget)
            k = 1.0 - forget
            q_s[bi, e] = q
            k_s[bi, e] = k
            g_s[bi, e] = g
            mn, mx = stats(q, k, g)
            mins.append(mn)
            maxs.append(mx)
        logits = _dot(glow[bi].astype(BF16), wup) + bup_ref[...]
        g = (jnp.minimum(logits, 0.0) - jnp.log(1.0 + jnp.exp(-jnp.abs(logits)))) * (1.0 / GLA_NORMALIZER)
        q = gq[bi] * (GLA_DK ** -0.5)
        k = gk[bi]
        q_s[bi, 2] = q
        k_s[bi, 2] = k
        g_s[bi, 2] = g
        mn, mx = stats(q, k, g)
        mins.append(mn)
        maxs.append(mx)
    min_tot = functools.reduce(jnp.minimum, mins)
    max_mag = functools.reduce(jnp.maximum, maxs)
    safe = jnp.logical_and(min_tot >= -2.0 * SAFE_EXP, max_mag <= SAFE_MAG)

    sub_iota = lax.broadcasted_iota(jnp.int32, (SUBLANES, LANES), 0)
    lane = lax.broadcasted_iota(jnp.int32, (REC_CHUNK, LANES), 1)
    tok_row = _perm_token_ids((REC_CHUNK, REC_CHUNK), 0)
    tok_col = _perm_token_ids((REC_CHUNK, REC_CHUNK), 1)
    causal = tok_col <= tok_row
    tok_of_row = _perm_token_ids((REC_CHUNK, LANES), 0)
    hgn = hgn_ref[...]
    glan = glan_ref[...]

    bodies = []
    for bi in range(nb):
        bodies += [
            ((bi, 0), None, hi0.at[bi], hg0.at[bi], hgn, yh_ref.at[0, bi], st_s.at[bi, 0]),
            ((bi, 1), None, hi1.at[bi], hg1.at[bi], hgn, yh_ref.at[1, bi], st_s.at[bi, 1]),
            ((bi, 2), lane < GLA_DK, gv0.at[bi], gg0.at[bi], glan, yg_ref.at[0, bi], st_s.at[bi, 2]),
            ((bi, 2), lane >= GLA_DK, gv1.at[bi], gg1.at[bi], glan, yg_ref.at[1, bi], st_s.at[bi, 3]),
        ]

    def load_qkb(slot, base):
        b, b_last = _cumsum_perm(_load_perm(g_s.at[slot], base), sub_iota)
        q = jnp.concatenate(_load_perm(q_s.at[slot], base), axis=0)
        k = jnp.concatenate(_load_perm(k_s.at[slot], base), axis=0)
        return q, k, b, b_last

    def fast_chunk(ci, carry):
        base = pl.multiple_of(ci * REC_CHUNK, REC_CHUNK)
        shared = {}
        staged = []
        for slot, qmask, v_ref, gate_ref, gain, y_view, st_view in bodies:
            if slot not in shared:
                q, k, b, b_last = load_qkb(slot, base)
                half = 0.5 * b_last
                d = b - half
                em = jnp.exp(half)
                qt = q * jnp.exp(d)
                kt = k * jnp.exp(-d)
                shared[slot] = (qt.astype(BF16), (qt * em).astype(BF16), kt.astype(BF16),
                                (kt * em).astype(BF16), em * em)
            qt_b, qi_b, kt_b, kd_b, dec = shared[slot]
            if qmask is not None:
                qt_b = jnp.where(qmask, qt_b, jnp.zeros_like(qt_b))
                qi_b = jnp.where(qmask, qi_b, jnp.zeros_like(qi_b))
            vb = jnp.concatenate(_load_perm(v_ref, base), axis=0).astype(BF16)
            state = st_view[...]
            scores = _dot_nt(qt_b, kt_b)
            o_inter = _dot_nt(qi_b, state.astype(BF16))
            st_view[...] = state * dec + _dot_tn(vb, kd_b)
            staged.append((scores, o_inter, vb))
        for (scores, o_inter, vb), (_, _, _, gate_ref, gain, y_view, _) in zip(staged, bodies):
            s = jnp.where(causal, scores, 0.0).astype(BF16)
            o = o_inter + _dot(s, vb)
            gate = jnp.concatenate(_load_perm(gate_ref, base), axis=0)
            _store_perm(y_view, base, _finish(o, gain, gate))
        return carry

    def exact_chunk(ci, carry):
        base = pl.multiple_of(ci * REC_CHUNK, REC_CHUNK)
        for slot, qmask, v_ref, gate_ref, gain, y_view, st_view in bodies:
            q, k, b, b_last = load_qkb(slot, base)
            if qmask is not None:
                q = jnp.where(qmask, q, 0.0)
            v = jnp.concatenate(_load_perm(v_ref, base), axis=0)
            gate = jnp.concatenate(_load_perm(gate_ref, base), axis=0)
            fb_s[0] = b
            fb_s[1] = q
            fb_s[2] = k
            fb_s[3] = v

            def row(r, c2):
                u = r >> 3
                tr = PERM_GROUP * (u >> 2) + PERM_STRIDE * (r & 7) + (u & 3)
                bt = fb_s[0, pl.ds(r, 1), :]
                qt = fb_s[1, pl.ds(r, 1), :]
                w = jnp.exp(jnp.where(tok_of_row <= tr, bt - fb_s[0], -jnp.inf))
                sc = jnp.sum(qt * w * fb_s[2], axis=-1, keepdims=True)
                oi_s[pl.ds(r, 1), :] = jnp.sum(sc * fb_s[3], axis=0, keepdims=True)
                return c2

            lax.fori_loop(0, REC_CHUNK, row, 0)
            state = st_view[...]
            vb = v.astype(BF16)
            o = _dot_nt((q * jnp.exp(b)).astype(BF16), state.astype(BF16)) + oi_s[...]
            kd_b = (k * jnp.exp(b_last - b)).astype(BF16)
            st_view[...] = state * jnp.exp(b_last) + _dot_tn(vb, kd_b)
            _store_perm(y_view, base, _finish(o, gain, gate))
        return carry

    @pl.when(safe)
    def _():
        lax.fori_loop(0, nc, fast_chunk, 0)

    @pl.when(jnp.logical_not(safe))
    def _():
        lax.fori_loop(0, nc, exact_chunk, 0)


def _recurrence(mix, lb_logits, hg_norm, gla_norm, wup_pad, bup, *, batch, tokens_per_batch, layer, tt=256):
    n, n_cols = mix.shape
    nt = tokens_per_batch // tt
    pairs = HG_HEADS // 2
    mix3 = mix.reshape(batch, tokens_per_batch, n_cols)
    off = {"hq": 0, "hf": 4, "hi": 8, "hg": 12, "gq": 16, "gk": 18, "gv": 20, "gg": 24, "gl": 28}

    def head(name, e):
        return pl.BlockSpec((batch, tt, LANES), lambda p, i: (0, i, off[name] + 2 * p + e))

    def pair(name):
        return pl.BlockSpec((batch, tt, LANES), lambda p, i: (0, i, off[name] + p))

    in_specs = [
        head("hq", 0), head("hq", 1), head("hf", 0), head("hf", 1),
        head("hi", 0), head("hi", 1), head("hg", 0), head("hg", 1),
        pair("gq"), pair("gk"),
        head("gv", 0), head("gv", 1), head("gg", 0), head("gg", 1),
        pl.BlockSpec((batch, tt, LANES), lambda p, i: (0, i, off["gl"])),
    ]
    in_specs_params = [
        pl.BlockSpec((lb_logits.shape[0], 2 * LANES), lambda p, i: (0, p)),
        pl.BlockSpec((1, LANES), lambda p, i: (0, 0)),
        pl.BlockSpec((1, LANES), lambda p, i: (0, 0)),
        pl.BlockSpec((LANES, LANES), lambda p, i: (0, p)),
        pl.BlockSpec((1, LANES), lambda p, i: (0, p)),
    ]
    out_spec = pl.BlockSpec((2, batch, tt, LANES), lambda p, i: (p, 0, i, 0))
    y_hg, y_gla = pl.pallas_call(
        functools.partial(_rec_kernel, tt=tt, layer=layer),
        grid=(pairs, nt),
        in_specs=in_specs + in_specs_params,
        out_specs=[out_spec, out_spec],
        out_shape=[jax.ShapeDtypeStruct((HG_HEADS, batch, tokens_per_batch, HG_DV), F32),
                   jax.ShapeDtypeStruct((GLA_HEADS, batch, tokens_per_batch, GLA_DV), F32)],
        scratch_shapes=[
            pltpu.VMEM((batch, 3, tt, LANES), F32),
            pltpu.VMEM((batch, 3, tt, LANES), F32),
            pltpu.VMEM((batch, 3, tt, LANES), F32),
            pltpu.VMEM((batch, 4, LANES, LANES), F32),
            pltpu.VMEM((4, REC_CHUNK, LANES), F32),
            pltpu.VMEM((REC_CHUNK, LANES), F32),
        ],
        compiler_params=pltpu.CompilerParams(
            dimension_semantics=("arbitrary", "arbitrary"), vmem_limit_bytes=48 * MIB),
        name="recurrence",
    )(*([mix3] * 15), lb_logits, hg_norm.reshape(1, HG_DV), gla_norm.reshape(1, GLA_DV), wup_pad,
      bup.reshape(1, -1))
    return y_hg.reshape(HG_HEADS, n, HG_DV), y_gla.reshape(GLA_HEADS, n, GLA_DV)


def _merge_ffn_kernel(x_ref, mod_ref, yh_ref, yg_ref, rh_ref, rg_ref, wuh_ref, wug_ref, wo_ref,
                      gain_ref, w_in_ref, w_down_ref, fgain_ref, o_ref, *, sub_mix, sub_ffn, d_ff, bounds, final):
    gate = mod_ref[0, 3 * sub_mix + 2:3 * sub_mix + 3, :]

    def up(y_ref, w_ref):
        y = jnp.concatenate([y_ref[h].astype(BF16) for h in range(y_ref.shape[0])], axis=-1)
        return _dot(y, w_ref[...])

    merged = (jax.nn.sigmoid(rh_ref[...]) * up(yh_ref, wuh_ref)
              + jax.nn.sigmoid(rg_ref[...]) * up(yg_ref, wug_ref))
    x = x_ref[...] + gate * _dot(merged.astype(BF16), wo_ref[...])
    o_ref[...] = _ffn_body(x, mod_ref, gain_ref, w_in_ref, w_down_ref, fgain_ref,
                           sub=sub_ffn, d_ff=d_ff, bounds=bounds, final=final)


def _merge_ffn(x2d, mod3, y_hg, y_gla, r, w_up_hg, w_up_gla, w_out, gain, w_in, w_down, fgain, *,
               sub_mix, sub_ffn, tokens_per_batch, final, tm=256):
    n, d = x2d.shape
    d_ff = w_down.shape[0]
    tiles_per_batch = tokens_per_batch // tm
    const = dict(pipeline_mode=pl.Buffered(1))
    return pl.pallas_call(
        functools.partial(_merge_ffn_kernel, sub_mix=sub_mix, sub_ffn=sub_ffn, d_ff=d_ff,
                          bounds=_mxu_aligned_bounds(d_ff, 2), final=final),
        grid=(n // tm,),
        in_specs=[
            pl.BlockSpec((tm, d), lambda i: (i, 0)),
            pl.BlockSpec((1, 3 * N_SUB, d), lambda i: (i // tiles_per_batch, 0, 0)),
            pl.BlockSpec((HG_HEADS, tm, HG_DV), lambda i: (0, i, 0)),
            pl.BlockSpec((GLA_HEADS, tm, GLA_DV), lambda i: (0, i, 0)),
            pl.BlockSpec((tm, d), lambda i: (i, 0)),
            pl.BlockSpec((tm, d), lambda i: (i, 1)),
            pl.BlockSpec(w_up_hg.shape, lambda i: (0, 0), **const),
            pl.BlockSpec(w_up_gla.shape, lambda i: (0, 0), **const),
            pl.BlockSpec(w_out.shape, lambda i: (0, 0), **const),
            pl.BlockSpec((1, d), lambda i: (0, 0)),
            pl.BlockSpec((d, 2 * d_ff), lambda i: (0, 0), **const),
            pl.BlockSpec((d_ff, d), lambda i: (0, 0), **const),
            pl.BlockSpec((1, d), lambda i: (0, 0)),
        ],
        out_specs=pl.BlockSpec((tm, d), lambda i: (i, 0)),
        out_shape=jax.ShapeDtypeStruct((n, d), F32),
        compiler_params=pltpu.CompilerParams(vmem_limit_bytes=56 * MIB),
        name="merge_ffn",
    )(x2d, mod3, y_hg, y_gla, r, r, w_up_hg, w_up_gla, w_out, gain.reshape(1, d), w_in, w_down,
      fgain.reshape(1, d))


def kernel(x, c, w_ada, b_ada, norm_gains, ffn1_w_in, ffn1_w_down, w_in_mix, w_gk_up, b_gk_up, lb_logits,
           hg_norm, gla_norm, w_up_hg, w_up_gla, w_out, ffn2_w_in, ffn2_w_down, final_norm):
    batch, seq, d = x.shape
    depth = w_ada.shape[0]
    hg_qk = HG_HEADS * HG_DK
    hg_w = HG_HEADS * HG_DV
    gla_k = GLA_HEADS * GLA_DK
    gla_v = GLA_HEADS * GLA_DV
    n_rec = 2 * hg_qk + 2 * hg_w + 2 * gla_k + 2 * gla_v
    n_mix = n_rec + LANES

    x2d = x.reshape(batch * seq, d)
    c_pad = jnp.pad(c, ((0, SUBLANES - batch % SUBLANES if batch % SUBLANES else 0), (0, 0)))
    for l in range(depth):
        mod = _ada(c_pad, w_ada[l], b_ada[l])[:batch].reshape(batch, 3 * N_SUB, d)

        x2d = _ffn(x2d, mod, norm_gains[l, 0], ffn1_w_in[l].astype(BF16), ffn1_w_down[l].astype(BF16),
                   final_norm, sub=0, tokens_per_batch=seq, final=False)

        w_mix = w_in_mix[l]
        w_r = jnp.concatenate([
            w_mix[:, :n_rec],
            w_mix[:, n_rec:n_rec + GLA_RANK],
            jnp.zeros((d, LANES - GLA_RANK), w_mix.dtype),
            w_mix[:, n_rec + GLA_RANK:],
        ], axis=1).astype(BF16)
        mix, r = _inproj(x2d, mod, norm_gains[l, 1], w_r, n_mix, sub=1, tokens_per_batch=seq)

        wup_pad = jnp.pad(w_gk_up[l], ((0, LANES - GLA_RANK), (0, 0)))
        y_hg, y_gla = _recurrence(mix, lb_logits, hg_norm[l], gla_norm[l], wup_pad, b_gk_up[l],
                                  batch=batch, tokens_per_batch=seq, layer=l)

        x2d = _merge_ffn(x2d, mod, y_hg, y_gla, r, w_up_hg[l].astype(BF16), w_up_gla[l].astype(BF16),
                         w_out[l].astype(BF16), norm_gains[l, 2], ffn2_w_in[l].astype(BF16),
                         ffn2_w_down[l].astype(BF16), final_norm, sub_mix=1, sub_ffn=2, tokens_per_batch=seq,
                         final=l == depth - 1)
    if depth == 0:
        raise ValueError("depth must be >= 1")
    return x2d.reshape(batch, seq, d)
```

```python
import functools

import jax
import jax.numpy as jnp
from jax import lax
from jax.experimental import pallas as pl
from jax.experimental.pallas import tpu as pltpu

F32 = jnp.float32
BF16 = jnp.bfloat16

EPS = 1e-6
N_SUB = 3
HG_HEADS = 4
HG_DK = 128
HG_DV = 128
GLA_HEADS = 4
GLA_DK = 64
GLA_DV = 128
GLA_RANK = 16
GLA_NORMALIZER = 16.0

LANES = 128
SUBLANES = 8
MXU_WIDTH = 256
MIB = 1024 * 1024

REC_CHUNK = 64
PERM_STRIDE = 4
PERM_GROUP = PERM_STRIDE * SUBLANES
N_GROUPS = REC_CHUNK // PERM_GROUP
REC_UNROLL = 2
LOG2E = 1.4426950408889634
SAFE_EXP2 = 100.0
SAFE_MAG = 1e7


def _dot(a, b):
    return jnp.dot(a, b, preferred_element_type=F32)


def _dot_nt(a, b):
    return lax.dot_general(a, b, (((1,), (1,)), ((), ())), preferred_element_type=F32)


def _dot_tn(a, b):
    return lax.dot_general(a, b, (((0,), (0,)), ((), ())), preferred_element_type=F32)


def _silu(v):
    return v * jax.nn.sigmoid(v)


def _norm_mod(x, gain, shift, scale):
    y = x * lax.rsqrt(jnp.mean(x * x, axis=-1, keepdims=True) + EPS) * gain
    return y * (1.0 + scale) + shift


def _ada_kernel(c_ref, w_ref, b_ref, o_ref):
    cond = _silu(c_ref[...]).astype(BF16)
    o_ref[...] = _dot(cond, w_ref[...].astype(BF16)) + b_ref[...]


def _ada(c_pad, w, b):
    rows, d = c_pad.shape
    n = w.shape[1]
    bn = n // 8
    return pl.pallas_call(
        _ada_kernel,
        grid=(n // bn,),
        in_specs=[
            pl.BlockSpec((rows, d), lambda j: (0, 0)),
            pl.BlockSpec((d, bn), lambda j: (0, j)),
            pl.BlockSpec((1, bn), lambda j: (0, j)),
        ],
        out_specs=pl.BlockSpec((rows, bn), lambda j: (0, j)),
        out_shape=jax.ShapeDtypeStruct((rows, n), F32),
        name="adaln",
    )(c_pad, w, b.reshape(1, n))


def _mxu_aligned_bounds(width, parts):
    if width % MXU_WIDTH:
        return (0, width)
    tiles = width // MXU_WIDTH
    return tuple(MXU_WIDTH * ((tiles * p + parts - 1) // parts) for p in range(parts)) + (width,)


def _ffn_kernel(x_ref, mod_ref, gain_ref, w_in_ref, w_down_ref, fgain_ref, o_ref, *, sub, d_ff, bounds, final):
    o_ref[...] = _ffn_body(x_ref[...], mod_ref, gain_ref, w_in_ref, w_down_ref, fgain_ref,
                           sub=sub, d_ff=d_ff, bounds=bounds, final=final)


def _ffn_body(x, mod_ref, gain_ref, w_in_ref, w_down_ref, fgain_ref, *, sub, d_ff, bounds, final):
    shift = mod_ref[0, 3 * sub + 0:3 * sub + 1, :]
    scale = mod_ref[0, 3 * sub + 1:3 * sub + 2, :]
    gate = mod_ref[0, 3 * sub + 2:3 * sub + 3, :]
    hb = _norm_mod(x, gain_ref[...], shift, scale).astype(BF16)
    acc = None
    for lo, hi in zip(bounds[:-1], bounds[1:]):
        g = _dot(hb, w_in_ref[:, lo:hi])
        u = _dot(hb, w_in_ref[:, d_ff + lo:d_ff + hi])
        act = (_silu(g) * u).astype(BF16)
        part = _dot(act, w_down_ref[lo:hi, :])
        acc = part if acc is None else acc + part
    xn = x + (0.5 * gate) * acc
    if final:
        xn = xn * lax.rsqrt(jnp.mean(xn * xn, axis=-1, keepdims=True) + EPS) * fgain_ref[...]
    return xn


def _ffn(x2d, mod3, gain, w_in, w_down, fgain, *, sub, tokens_per_batch, final, tm=512):
    n, d = x2d.shape
    d_ff = w_down.shape[0]
    tiles_per_batch = tokens_per_batch // tm
    bounds = _mxu_aligned_bounds(d_ff, 2)
    const = dict(pipeline_mode=pl.Buffered(1))
    return pl.pallas_call(
        functools.partial(_ffn_kernel, sub=sub, d_ff=d_ff, bounds=bounds, final=final),
        grid=(n // tm,),
        in_specs=[
            pl.BlockSpec((tm, d), lambda i: (i, 0)),
            pl.BlockSpec((1, 3 * N_SUB, d), lambda i: (i // tiles_per_batch, 0, 0)),
            pl.BlockSpec((1, d), lambda i: (0, 0)),
            pl.BlockSpec((d, 2 * d_ff), lambda i: (0, 0), **const),
            pl.BlockSpec((d_ff, d), lambda i: (0, 0), **const),
            pl.BlockSpec((1, d), lambda i: (0, 0)),
        ],
        out_specs=pl.BlockSpec((tm, d), lambda i: (i, 0)),
        out_shape=jax.ShapeDtypeStruct((n, d), F32),
        compiler_params=pltpu.CompilerParams(vmem_limit_bytes=48 * MIB),
        name="ffn_final" if final else "ffn",
    )(x2d, mod3, gain.reshape(1, d), w_in, w_down, fgain.reshape(1, d))


def _inproj_kernel(x_ref, mod_ref, gain_ref, w_ref, mix_ref, r_ref, *, sub, n_mix):
    shift = mod_ref[0, 3 * sub + 0:3 * sub + 1, :]
    scale = mod_ref[0, 3 * sub + 1:3 * sub + 2, :]
    hb = _norm_mod(x_ref[...], gain_ref[...], shift, scale).astype(BF16)
    mix_ref[...] = _dot(hb, w_ref[:, :n_mix])
    r_ref[...] = _dot(hb, w_ref[:, n_mix:])


def _inproj(x2d, mod3, gain, w, n_mix, *, sub, tokens_per_batch, tm=512):
    n, d = x2d.shape
    n_all = w.shape[1]
    tiles_per_batch = tokens_per_batch // tm
    return pl.pallas_call(
        functools.partial(_inproj_kernel, sub=sub, n_mix=n_mix),
        grid=(n // tm,),
        in_specs=[
            pl.BlockSpec((tm, d), lambda i: (i, 0)),
            pl.BlockSpec((1, 3 * N_SUB, d), lambda i: (i // tiles_per_batch, 0, 0)),
            pl.BlockSpec((1, d), lambda i: (0, 0)),
            pl.BlockSpec((d, n_all), lambda i: (0, 0), pipeline_mode=pl.Buffered(1)),
        ],
        out_specs=[
            pl.BlockSpec((tm, n_mix), lambda i: (i, 0)),
            pl.BlockSpec((tm, n_all - n_mix), lambda i: (i, 0)),
        ],
        out_shape=[
            jax.ShapeDtypeStruct((n, n_mix), F32),
            jax.ShapeDtypeStruct((n, n_all - n_mix), F32),
        ],
        compiler_params=pltpu.CompilerParams(vmem_limit_bytes=48 * MIB),
        name="inproj",
    )(x2d, mod3, gain.reshape(1, d), w)


def _load_perm(ref, base):
    return [ref[pl.ds(base + PERM_GROUP * c + a, SUBLANES, stride=PERM_STRIDE), :]
            for c in range(N_GROUPS) for a in range(PERM_STRIDE)]


def _store_perm(ref, base, val):
    for c in range(N_GROUPS):
        for a in range(PERM_STRIDE):
            u = PERM_STRIDE * c + a
            ref[pl.ds(base + PERM_GROUP * c + a, SUBLANES, stride=PERM_STRIDE), :] = (
                val[SUBLANES * u:SUBLANES * (u + 1), :])


def _shift_down(v, d, sub_iota):
    return jnp.where(sub_iota >= d, pltpu.roll(v, d, 0), 0.0)


def _cumsum_perm(g, sub_iota):
    out = []
    carry = None
    for c in range(N_GROUPS):
        pre = [g[PERM_STRIDE * c]]
        for a in range(1, PERM_STRIDE):
            pre.append(pre[-1] + g[PERM_STRIDE * c + a])
        inc = pre[-1]
        inc = inc + _shift_down(inc, 1, sub_iota)
        inc = inc + _shift_down(inc, 2, sub_iota)
        inc = inc + _shift_down(inc, 4, sub_iota)
        exc = _shift_down(inc, 1, sub_iota)
        if carry is not None:
            exc = exc + carry
        out.extend(p + exc for p in pre)
        tot = inc[SUBLANES - 1:SUBLANES, :]
        carry = tot if carry is None else carry + tot
    return jnp.concatenate(out, axis=0), carry


def _perm_token_ids(shape, dim):
    r = lax.broadcasted_iota(jnp.int32, shape, dim)
    u = r >> 3
    return PERM_GROUP * (u >> 2) + PERM_STRIDE * (r & 7) + (u & 3)


def _silu_tanh(v):
    h = 0.5 * v
    return h * jnp.tanh(h) + h


def _finish(o, gain, gate):
    y = o * lax.rsqrt(jnp.mean(o * o, axis=-1, keepdims=True) + EPS) * gain
    return y * _silu_tanh(gate)


def _rec_kernel(hq0, hq1, hf0, hf1, hi0, hi1, hg0, hg1, gq, gk, gv0, gv1, gg0, gg1, glow,
                lbl_ref, hgn_ref, glan_ref, wup_ref, bup_ref,
                yh_ref, yg_ref,
                q_s, k_s, g_s, st_s, fb_s, oi_s, *, tt, layer, unroll):
    nc = tt // REC_CHUNK
    nb = hq0.shape[0]

    @pl.when(pl.program_id(1) == 0)
    def _():
        st_s[...] = jnp.zeros_like(st_s)

    def lower_bound(e):
        lbl = lbl_ref[:, e * LANES:(e + 1) * LANES]
        ex = jnp.exp(lbl - jnp.max(lbl, axis=0, keepdims=True))
        return jnp.sum(ex[:layer + 1], axis=0, keepdims=True) / jnp.sum(ex, axis=0, keepdims=True)

    tot_min = None
    mag_max = None

    def fold_stats(g, mags):
        nonlocal tot_min, mag_max
        tot = jnp.sum(g.reshape(nc, REC_CHUNK, LANES), axis=1)
        tot_min = tot if tot_min is None else jnp.minimum(tot_min, tot)
        for m in mags:
            m = jnp.max(jnp.abs(m).reshape(tt // SUBLANES, SUBLANES, LANES), axis=0)
            mag_max = m if mag_max is None else jnp.maximum(mag_max, m)

    wup = wup_ref[...].astype(BF16)
    lbs = [lower_bound(e) for e in range(2)]
    for bi in range(nb):
        for e, (hq, hf) in enumerate(((hq0, hf0), (hq1, hf1))):
            q = _silu_tanh(hq[bi])
            c1 = 0.5 * (1.0 - lbs[e])
            p = c1 * jnp.tanh(0.5 * hf[bi])
            g = jnp.log2((1.0 - c1) + p)
            q_s[bi, e] = q
            k_s[bi, e] = c1 - p
            g_s[bi, e] = g
            fold_stats(g, (q,))
        logits = _dot(glow[bi].astype(BF16), wup) + bup_ref[...]
        g = (jnp.minimum(logits, 0.0) - jnp.log(1.0 + jnp.exp(-jnp.abs(logits)))) * (LOG2E / GLA_NORMALIZER)
        q = gq[bi] * (GLA_DK ** -0.5)
        k = gk[bi]
        q_s[bi, 2] = q
        k_s[bi, 2] = k
        g_s[bi, 2] = g
        fold_stats(g, (q, k))
    safe = jnp.logical_and(jnp.min(tot_min) >= -2.0 * SAFE_EXP2, jnp.max(mag_max) <= SAFE_MAG)

    sub_iota = lax.broadcasted_iota(jnp.int32, (SUBLANES, LANES), 0)
    lane = lax.broadcasted_iota(jnp.int32, (REC_CHUNK, LANES), 1)
    tok_row = _perm_token_ids((REC_CHUNK, REC_CHUNK), 0)
    tok_col = _perm_token_ids((REC_CHUNK, REC_CHUNK), 1)
    causal = tok_col <= tok_row
    tok_of_row = _perm_token_ids((REC_CHUNK, LANES), 0)
    hgn = hgn_ref[...]
    glan = glan_ref[...]

    bodies = []
    for bi in range(nb):
        bodies += [
            ((bi, 0), None, hi0.at[bi], hg0.at[bi], hgn, yh_ref.at[0, bi], st_s.at[bi, 0]),
            ((bi, 1), None, hi1.at[bi], hg1.at[bi], hgn, yh_ref.at[1, bi], st_s.at[bi, 1]),
            ((bi, 2), lane < GLA_DK, gv0.at[bi], gg0.at[bi], glan, yg_ref.at[0, bi], st_s.at[bi, 2]),
            ((bi, 2), lane >= GLA_DK, gv1.at[bi], gg1.at[bi], glan, yg_ref.at[1, bi], st_s.at[bi, 3]),
        ]

    def load_qkb(slot, base):
        b, b_last = _cumsum_perm(_load_perm(g_s.at[slot], base), sub_iota)
        q = jnp.concatenate(_load_perm(q_s.at[slot], base), axis=0)
        k = jnp.concatenate(_load_perm(k_s.at[slot], base), axis=0)
        return q, k, b, b_last

    def fast_chunk(ci, carry):
        staged = []
        for sub in range(unroll):
            base = pl.multiple_of((ci * unroll + sub) * REC_CHUNK, REC_CHUNK)
            shared = {}
            for slot, qmask, v_ref, gate_ref, gain, y_view, st_view in bodies:
                if slot not in shared:
                    q, k, b, b_last = load_qkb(slot, base)
                    half = 0.5 * b_last
                    d = b - half
                    em = jnp.exp2(half)
                    qt = q * jnp.exp2(d)
                    kt = k * jnp.exp2(-d)
                    shared[slot] = (qt.astype(BF16), kt.astype(BF16), em)
                qt_b, kt_b, em = shared[slot]
                if qmask is not None:
                    qt_b = jnp.where(qmask, qt_b, jnp.zeros_like(qt_b))
                vb = jnp.concatenate(_load_perm(v_ref, base), axis=0).astype(BF16)
                state_e = st_view[...] * em
                scores = _dot_nt(qt_b, kt_b)
                o_inter = _dot_nt(qt_b, state_e.astype(BF16))
                st_view[...] = (state_e + _dot_tn(vb, kt_b)) * em
                staged.append((scores, o_inter, vb, base, gate_ref, gain, y_view))
        for scores, o_inter, vb, base, gate_ref, gain, y_view in staged:
            s = jnp.where(causal, scores, 0.0).astype(BF16)
            o = o_inter + _dot(s, vb)
            gate = jnp.concatenate(_load_perm(gate_ref, base), axis=0)
            _store_perm(y_view, base, _finish(o, gain, gate))
        return carry

    def exact_chunk(ci, carry):
        base = pl.multiple_of(ci * REC_CHUNK, REC_CHUNK)
        for slot, qmask, v_ref, gate_ref, gain, y_view, st_view in bodies:
            q, k, b, b_last = load_qkb(slot, base)
            if qmask is not None:
                q = jnp.where(qmask, q, 0.0)
            v = jnp.concatenate(_load_perm(v_ref, base), axis=0)
            gate = jnp.concatenate(_load_perm(gate_ref, base), axis=0)
            fb_s[0] = b
            fb_s[1] = q
            fb_s[2] = k
            fb_s[3] = v

            def row(r, c2):
                u = r >> 3
                tr = PERM_GROUP * (u >> 2) + PERM_STRIDE * (r & 7) + (u & 3)
                bt = fb_s[0, pl.ds(r, 1), :]
                qt = fb_s[1, pl.ds(r, 1), :]
                w = jnp.exp2(jnp.where(tok_of_row <= tr, bt - fb_s[0], -jnp.inf))
                sc = jnp.sum(qt * w * fb_s[2], axis=-1, keepdims=True)
                oi_s[pl.ds(r, 1), :] = jnp.sum(sc * fb_s[3], axis=0, keepdims=True)
                return c2

            lax.fori_loop(0, REC_CHUNK, row, 0)
            state = st_view[...]
            vb = v.astype(BF16)
            o = _dot_nt((q * jnp.exp2(b)).astype(BF16), state.astype(BF16)) + oi_s[...]
            kd_b = (k * jnp.exp2(b_last - b)).astype(BF16)
            st_view[...] = state * jnp.exp2(b_last) + _dot_tn(vb, kd_b)
            _store_perm(y_view, base, _finish(o, gain, gate))
        return carry

    @pl.when(safe)
    def _():
        lax.fori_loop(0, nc // unroll, fast_chunk, 0)

    @pl.when(jnp.logical_not(safe))
    def _():
        lax.fori_loop(0, nc, exact_chunk, 0)


def _recurrence(mix, lb_logits, hg_norm, gla_norm, wup_pad, bup, *, batch, tokens_per_batch, layer, tt=256):
    n, n_cols = mix.shape
    nt = tokens_per_batch // tt
    pairs = HG_HEADS // 2
    mix3 = mix.reshape(batch, tokens_per_batch, n_cols)
    off = {"hq": 0, "hf": 4, "hi": 8, "hg": 12, "gq": 16, "gk": 18, "gv": 20, "gg": 24, "gl": 28}

    def head(name, e):
        return pl.BlockSpec((batch, tt, LANES), lambda p, i: (0, i, off[name] + 2 * p + e))

    def pair(name):
        return pl.BlockSpec((batch, tt, LANES), lambda p, i: (0, i, off[name] + p))

    in_specs = [
        head("hq", 0), head("hq", 1), head("hf", 0), head("hf", 1),
        head("hi", 0), head("hi", 1), head("hg", 0), head("hg", 1),
        pair("gq"), pair("gk"),
        head("gv", 0), head("gv", 1), head("gg", 0), head("gg", 1),
        pl.BlockSpec((batch, tt, LANES), lambda p, i: (0, i, off["gl"])),
    ]
    in_specs_params = [
        pl.BlockSpec((lb_logits.shape[0], 2 * LANES), lambda p, i: (0, p)),
        pl.BlockSpec((1, LANES), lambda p, i: (0, 0)),
        pl.BlockSpec((1, LANES), lambda p, i: (0, 0)),
        pl.BlockSpec((LANES, LANES), lambda p, i: (0, p)),
        pl.BlockSpec((1, LANES), lambda p, i: (0, p)),
    ]
    out_spec = pl.BlockSpec((2, batch, tt, LANES), lambda p, i: (p, 0, i, 0))
    y_hg, y_gla = pl.pallas_call(
        functools.partial(_rec_kernel, tt=tt, layer=layer, unroll=REC_UNROLL),
        grid=(pairs, nt),
        in_specs=in_specs + in_specs_params,
        out_specs=[out_spec, out_spec],
        out_shape=[jax.ShapeDtypeStruct((HG_HEADS, batch, tokens_per_batch, HG_DV), F32),
                   jax.ShapeDtypeStruct((GLA_HEADS, batch, tokens_per_batch, GLA_DV), F32)],
        scratch_shapes=[
            pltpu.VMEM((batch, 3, tt, LANES), F32),
            pltpu.VMEM((batch, 3, tt, LANES), F32),
            pltpu.VMEM((batch, 3, tt, LANES), F32),
            pltpu.VMEM((batch, 4, LANES, LANES), F32),
            pltpu.VMEM((4, REC_CHUNK, LANES), F32),
            pltpu.VMEM((REC_CHUNK, LANES), F32),
        ],
        compiler_params=pltpu.CompilerParams(
            dimension_semantics=("arbitrary", "arbitrary"), vmem_limit_bytes=48 * MIB),
        name="recurrence",
    )(*([mix3] * 15), lb_logits, hg_norm.reshape(1, HG_DV), gla_norm.reshape(1, GLA_DV), wup_pad,
      bup.reshape(1, -1))
    return y_hg.reshape(HG_HEADS, n, HG_DV), y_gla.reshape(GLA_HEADS, n, GLA_DV)


def _merge_ffn_kernel(x_ref, mod_ref, yh_ref, yg_ref, rh_ref, rg_ref, wuh_ref, wug_ref, wo_ref,
                      gain_ref, w_in_ref, w_down_ref, fgain_ref, o_ref, *, sub_mix, sub_ffn, d_ff, bounds, final):
    gate = mod_ref[0, 3 * sub_mix + 2:3 * sub_mix + 3, :]

    def up(y_ref, w_ref):
        y = jnp.concatenate([y_ref[h].astype(BF16) for h in range(y_ref.shape[0])], axis=-1)
        return _dot(y, w_ref[...])

    merged = (jax.nn.sigmoid(rh_ref[...]) * up(yh_ref, wuh_ref)
              + jax.nn.sigmoid(rg_ref[...]) * up(yg_ref, wug_ref))
    x = x_ref[...] + gate * _dot(merged.astype(BF16), wo_ref[...])
    o_ref[...] = _ffn_body(x, mod_ref, gain_ref, w_in_ref, w_down_ref, fgain_ref,
                           sub=sub_ffn, d_ff=d_ff, bounds=bounds, final=final)


def _merge_ffn(x2d, mod3, y_hg, y_gla, r, w_up_hg, w_up_gla, w_out, gain, w_in, w_down, fgain, *,
               sub_mix, sub_ffn, tokens_per_batch, final, tm=512):
    n, d = x2d.shape
    d_ff = w_down.shape[0]
    tiles_per_batch = tokens_per_batch // tm
    const = dict(pipeline_mode=pl.Buffered(1))
    return pl.pallas_call(
        functools.partial(_merge_ffn_kernel, sub_mix=sub_mix, sub_ffn=sub_ffn, d_ff=d_ff,
                          bounds=_mxu_aligned_bounds(d_ff, 2), final=final),
        grid=(n // tm,),
        in_specs=[
            pl.BlockSpec((tm, d), lambda i: (i, 0)),
            pl.BlockSpec((1, 3 * N_SUB, d), lambda i: (i // tiles_per_batch, 0, 0)),
            pl.BlockSpec((HG_HEADS, tm, HG_DV), lambda i: (0, i, 0)),
            pl.BlockSpec((GLA_HEADS, tm, GLA_DV), lambda i: (0, i, 0)),
            pl.BlockSpec((tm, d), lambda i: (i, 0)),
            pl.BlockSpec((tm, d), lambda i: (i, 1)),
            pl.BlockSpec(w_up_hg.shape, lambda i: (0, 0), **const),
            pl.BlockSpec(w_up_gla.shape, lambda i: (0, 0), **const),
            pl.BlockSpec(w_out.shape, lambda i: (0, 0), **const),
            pl.BlockSpec((1, d), lambda i: (0, 0)),
            pl.BlockSpec((d, 2 * d_ff), lambda i: (0, 0), **const),
            pl.BlockSpec((d_ff, d), lambda i: (0, 0), **const),
            pl.BlockSpec((1, d), lambda i: (0, 0)),
        ],
        out_specs=pl.BlockSpec((tm, d), lambda i: (i, 0)),
        out_shape=jax.ShapeDtypeStruct((n, d), F32),
        compiler_params=pltpu.CompilerParams(vmem_limit_bytes=56 * MIB),
        name="merge_ffn",
    )(x2d, mod3, y_hg, y_gla, r, r, w_up_hg, w_up_gla, w_out, gain.reshape(1, d), w_in, w_down,
      fgain.reshape(1, d))


def kernel(x, c, w_ada, b_ada, norm_gains, ffn1_w_in, ffn1_w_down, w_in_mix, w_gk_up, b_gk_up, lb_logits,
           hg_norm, gla_norm, w_up_hg, w_up_gla, w_out, ffn2_w_in, ffn2_w_down, final_norm):
    batch, seq, d = x.shape
    depth = w_ada.shape[0]
    hg_qk = HG_HEADS * HG_DK
    hg_w = HG_HEADS * HG_DV
    gla_k = GLA_HEADS * GLA_DK
    gla_v = GLA_HEADS * GLA_DV
    n_rec = 2 * hg_qk + 2 * hg_w + 2 * gla_k + 2 * gla_v
    n_mix = n_rec + LANES

    x2d = x.reshape(batch * seq, d)
    c_pad = jnp.pad(c, ((0, SUBLANES - batch % SUBLANES if batch % SUBLANES else 0), (0, 0)))
    for l in range(depth):
        mod = _ada(c_pad, w_ada[l], b_ada[l])[:batch].reshape(batch, 3 * N_SUB, d)

        x2d = _ffn(x2d, mod, norm_gains[l, 0], ffn1_w_in[l].astype(BF16), ffn1_w_down[l].astype(BF16),
                   final_norm, sub=0, tokens_per_batch=seq, final=False)

        w_mix = w_in_mix[l]
        w_r = jnp.concatenate([
            w_mix[:, :n_rec],
            w_mix[:, n_rec:n_rec + GLA_RANK],
            jnp.zeros((d, LANES - GLA_RANK), w_mix.dtype),
            w_mix[:, n_rec + GLA_RANK:],
        ], axis=1).astype(BF16)
        mix, r = _inproj(x2d, mod, norm_gains[l, 1], w_r, n_mix, sub=1, tokens_per_batch=seq)

        wup_pad = jnp.pad(w_gk_up[l], ((0, LANES - GLA_RANK), (0, 0)))
        y_hg, y_gla = _recurrence(mix, lb_logits, hg_norm[l], gla_norm[l], wup_pad, b_gk_up[l],
                                  batch=batch, tokens_per_batch=seq, layer=l)

        x2d = _merge_ffn(x2d, mod, y_hg, y_gla, r, w_up_hg[l].astype(BF16), w_up_gla[l].astype(BF16),
                         w_out[l].astype(BF16), norm_gains[l, 2], ffn2_w_in[l].astype(BF16),
                         ffn2_w_down[l].astype(BF16), final_norm, sub_mix=1, sub_ffn=2, tokens_per_batch=seq,
                         final=l == depth - 1)
    if depth == 0:
        raise ValueError("depth must be >= 1")
    return x2d.reshape(batch, seq, d)
```

```python
import functools

import jax
import jax.numpy as jnp
from jax import lax
from jax.experimental import pallas as pl
from jax.experimental.pallas import tpu as pltpu

F32 = jnp.float32
BF16 = jnp.bfloat16

EPS = 1e-6
N_SUB = 3
HG_HEADS = 4
HG_DK = 128
HG_DV = 128
GLA_HEADS = 4
GLA_DK = 64
GLA_DV = 128
GLA_RANK = 16
GLA_NORMALIZER = 16.0

LANES = 128
SUBLANES = 8
MXU_WIDTH = 256
MIB = 1024 * 1024

REC_CHUNK = 64
PERM_STRIDE = 4
PERM_GROUP = PERM_STRIDE * SUBLANES
N_GROUPS = REC_CHUNK // PERM_GROUP
REC_UNROLL = 2
LOG2E = 1.4426950408889634
SAFE_EXP2 = 100.0
SAFE_MAG = 1e7


def _dot(a, b):
    return jnp.dot(a, b, preferred_element_type=F32)


def _dot_nt(a, b):
    return lax.dot_general(a, b, (((1,), (1,)), ((), ())), preferred_element_type=F32)


def _dot_tn(a, b):
    return lax.dot_general(a, b, (((0,), (0,)), ((), ())), preferred_element_type=F32)


def _silu(v):
    return v * jax.nn.sigmoid(v)


def _norm_mod(x, gain, shift, scale):
    y = x * lax.rsqrt(jnp.mean(x * x, axis=-1, keepdims=True) + EPS) * gain
    return y * (1.0 + scale) + shift


def _ada_kernel(c_ref, w_ref, b_ref, o_ref):
    cond = _silu(c_ref[...]).astype(BF16)
    o_ref[...] = _dot(cond, w_ref[...].astype(BF16)) + b_ref[...]


def _ada(c_pad, w, b):
    rows, d = c_pad.shape
    n = w.shape[1]
    bn = n // 8
    return pl.pallas_call(
        _ada_kernel,
        grid=(n // bn,),
        in_specs=[
            pl.BlockSpec((rows, d), lambda j: (0, 0)),
            pl.BlockSpec((d, bn), lambda j: (0, j)),
            pl.BlockSpec((1, bn), lambda j: (0, j)),
        ],
        out_specs=pl.BlockSpec((rows, bn), lambda j: (0, j)),
        out_shape=jax.ShapeDtypeStruct((rows, n), F32),
        name="adaln",
    )(c_pad, w, b.reshape(1, n))


def _mxu_aligned_bounds(width, parts):
    if width % MXU_WIDTH:
        return (0, width)
    tiles = width // MXU_WIDTH
    return tuple(MXU_WIDTH * ((tiles * p + parts - 1) // parts) for p in range(parts)) + (width,)


def _ffn_weight_specs(layer, d, d_ff):
    assert d_ff % MXU_WIDTH == 0
    steps = d_ff // MXU_WIDTH

    def chunk(i):
        return jnp.minimum(i, steps - 1)

    specs = [
        pl.BlockSpec((None, d, MXU_WIDTH), lambda i: (layer, 0, chunk(i))),
        pl.BlockSpec((None, d, MXU_WIDTH), lambda i: (layer, 0, steps + chunk(i))),
        pl.BlockSpec((None, MXU_WIDTH, d), lambda i: (layer, chunk(i), 0)),
    ]
    scratch = [pltpu.VMEM((d, 2 * d_ff), BF16), pltpu.VMEM((d_ff, d), BF16)]
    return steps, specs, scratch


def _stage_ffn_weights(step, wg_ref, wu_ref, wd_ref, w_in_s, w_down_s, d_ff):
    for j in range(d_ff // MXU_WIDTH):
        @pl.when(step == j)
        def _(lo=j * MXU_WIDTH):
            w_in_s[:, lo:lo + MXU_WIDTH] = wg_ref[...].astype(BF16)
            w_in_s[:, d_ff + lo:d_ff + lo + MXU_WIDTH] = wu_ref[...].astype(BF16)
            w_down_s[lo:lo + MXU_WIDTH, :] = wd_ref[...].astype(BF16)


def _ffn_kernel(x_ref, mod_ref, gain_ref, wg_ref, wu_ref, wd_ref, fgain_ref, o_ref, w_in_s, w_down_s, *,
                sub, d_ff, bounds, final, w_steps):
    step = pl.program_id(0)
    _stage_ffn_weights(step, wg_ref, wu_ref, wd_ref, w_in_s, w_down_s, d_ff)

    @pl.when(step >= w_steps)
    def _():
        o_ref[...] = _ffn_body(x_ref[...], mod_ref, gain_ref, w_in_s, w_down_s, fgain_ref,
                               sub=sub, d_ff=d_ff, bounds=bounds, final=final)


def _ffn_body(x, mod_ref, gain_ref, w_in_ref, w_down_ref, fgain_ref, *, sub, d_ff, bounds, final):
    shift = mod_ref[0, 3 * sub + 0:3 * sub + 1, :]
    scale = mod_ref[0, 3 * sub + 1:3 * sub + 2, :]
    gate = mod_ref[0, 3 * sub + 2:3 * sub + 3, :]
    hb = _norm_mod(x, gain_ref[...], shift, scale).astype(BF16)
    acc = None
    for lo, hi in zip(bounds[:-1], bounds[1:]):
        g = _dot(hb, w_in_ref[:, lo:hi])
        u = _dot(hb, w_in_ref[:, d_ff + lo:d_ff + hi])
        act = (_silu(g) * u).astype(BF16)
        part = _dot(act, w_down_ref[lo:hi, :])
        acc = part if acc is None else acc + part
    xn = x + (0.5 * gate) * acc
    if final:
        xn = xn * lax.rsqrt(jnp.mean(xn * xn, axis=-1, keepdims=True) + EPS) * fgain_ref[...]
    return xn


def _ffn(x2d, mod3, gain, w_in_all, w_down_all, fgain, *, layer, sub, tokens_per_batch, final, tm=512):
    n, d = x2d.shape
    d_ff = w_down_all.shape[1]
    tiles_per_batch = tokens_per_batch // tm
    w_steps, w_specs, w_scratch = _ffn_weight_specs(layer, d, d_ff)

    def tile(i):
        return jnp.maximum(i - w_steps, 0)

    return pl.pallas_call(
        functools.partial(_ffn_kernel, sub=sub, d_ff=d_ff, bounds=_mxu_aligned_bounds(d_ff, 2), final=final,
                          w_steps=w_steps),
        grid=(w_steps + n // tm,),
        in_specs=[
            pl.BlockSpec((tm, d), lambda i: (tile(i), 0)),
            pl.BlockSpec((1, 3 * N_SUB, d), lambda i: (tile(i) // tiles_per_batch, 0, 0)),
            pl.BlockSpec((1, d), lambda i: (0, 0)),
            *w_specs,
            pl.BlockSpec((1, d), lambda i: (0, 0)),
        ],
        out_specs=pl.BlockSpec((tm, d), lambda i: (tile(i), 0)),
        out_shape=jax.ShapeDtypeStruct((n, d), F32),
        scratch_shapes=w_scratch,
        compiler_params=pltpu.CompilerParams(dimension_semantics=("arbitrary",), vmem_limit_bytes=52 * MIB),
        name="ffn_final" if final else "ffn",
    )(x2d, mod3, gain.reshape(1, d), w_in_all, w_in_all, w_down_all, fgain.reshape(1, d))


MIX_W_CHUNK = 512


def _inproj_kernel(x_ref, mod_ref, gain_ref, wt_ref, mix_ref, r_ref, wt_s, *, sub, n_rec, n_cols, w_steps):
    step = pl.program_id(0)
    for j in range(w_steps):
        @pl.when(step == j)
        def _(lo=j * MIX_W_CHUNK):
            w = wt_ref[...]
            if lo + MIX_W_CHUNK > n_cols:
                row = lo + lax.broadcasted_iota(jnp.int32, w.shape, 0)
                w = jnp.where(row < n_cols, w, 0.0)
            wt_s[lo:lo + MIX_W_CHUNK, :] = w.astype(BF16)

    @pl.when(step >= w_steps)
    def _():
        shift = mod_ref[0, 3 * sub + 0:3 * sub + 1, :]
        scale = mod_ref[0, 3 * sub + 1:3 * sub + 2, :]
        hb = _norm_mod(x_ref[...], gain_ref[...], shift, scale).astype(BF16)
        n_r = r_ref.shape[1]
        mix_ref[:, :n_rec] = _dot_nt(hb, wt_s[:n_rec, :])
        mix_ref[:, n_rec:] = _dot_nt(hb, wt_s[n_rec:n_rec + LANES, :])
        r_ref[...] = _dot_nt(hb, wt_s[n_rec + GLA_RANK:n_rec + GLA_RANK + n_r, :])


def _inproj(x2d, mod3, gain, wt_all, n_rec, *, layer, sub, tokens_per_batch, tm=512):
    n, d = x2d.shape
    n_cols = wt_all.shape[1]
    n_r = n_cols - n_rec - GLA_RANK
    w_steps = pl.cdiv(n_cols, MIX_W_CHUNK)
    assert n_rec % MXU_WIDTH == 0 and n_rec + LANES <= w_steps * MIX_W_CHUNK
    tiles_per_batch = tokens_per_batch // tm

    def tile(i):
        return jnp.maximum(i - w_steps, 0)

    return pl.pallas_call(
        functools.partial(_inproj_kernel, sub=sub, n_rec=n_rec, n_cols=n_cols, w_steps=w_steps),
        grid=(w_steps + n // tm,),
        in_specs=[
            pl.BlockSpec((tm, d), lambda i: (tile(i), 0)),
            pl.BlockSpec((1, 3 * N_SUB, d), lambda i: (tile(i) // tiles_per_batch, 0, 0)),
            pl.BlockSpec((1, d), lambda i: (0, 0)),
            pl.BlockSpec((None, MIX_W_CHUNK, d), lambda i: (layer, jnp.minimum(i, w_steps - 1), 0)),
        ],
        out_specs=[
            pl.BlockSpec((tm, n_rec + LANES), lambda i: (tile(i), 0)),
            pl.BlockSpec((tm, n_r), lambda i: (tile(i), 0)),
        ],
        out_shape=[
            jax.ShapeDtypeStruct((n, n_rec + LANES), F32),
            jax.ShapeDtypeStruct((n, n_r), F32),
        ],
        scratch_shapes=[pltpu.VMEM((w_steps * MIX_W_CHUNK, d), BF16)],
        compiler_params=pltpu.CompilerParams(dimension_semantics=("arbitrary",), vmem_limit_bytes=52 * MIB),
        name="inproj",
    )(x2d, mod3, gain.reshape(1, d), wt_all)


def _load_perm(ref, base):
    return [ref[pl.ds(base + PERM_GROUP * c + a, SUBLANES, stride=PERM_STRIDE), :]
            for c in range(N_GROUPS) for a in range(PERM_STRIDE)]


def _store_perm(ref, base, val):
    for c in range(N_GROUPS):
        for a in range(PERM_STRIDE):
            u = PERM_STRIDE * c + a
            ref[pl.ds(base + PERM_GROUP * c + a, SUBLANES, stride=PERM_STRIDE), :] = (
                val[SUBLANES * u:SUBLANES * (u + 1), :])


def _shift_down(v, d, sub_iota):
    return jnp.where(sub_iota >= d, pltpu.roll(v, d, 0), 0.0)


def _cumsum_perm(g, sub_iota):
    out = []
    carry = None
    for c in range(N_GROUPS):
        pre = [g[PERM_STRIDE * c]]
        for a in range(1, PERM_STRIDE):
            pre.append(pre[-1] + g[PERM_STRIDE * c + a])
        inc = pre[-1]
        inc = inc + _shift_down(inc, 1, sub_iota)
        inc = inc + _shift_down(inc, 2, sub_iota)
        inc = inc + _shift_down(inc, 4, sub_iota)
        exc = _shift_down(inc, 1, sub_iota)
        if carry is not None:
            exc = exc + carry
        out.extend(p + exc for p in pre)
        tot = inc[SUBLANES - 1:SUBLANES, :]
        carry = tot if carry is None else carry + tot
    return jnp.concatenate(out, axis=0), carry


def _perm_token_ids(shape, dim):
    r = lax.broadcasted_iota(jnp.int32, shape, dim)
    u = r >> 3
    return PERM_GROUP * (u >> 2) + PERM_STRIDE * (r & 7) + (u & 3)


def _silu_tanh(v):
    h = 0.5 * v
    return h * jnp.tanh(h) + h


def _finish(o, gain, gate):
    y = o * lax.rsqrt(jnp.mean(o * o, axis=-1, keepdims=True) + EPS) * gain
    return y * _silu_tanh(gate)


def _rec_kernel(hq0, hq1, hf0, hf1, hi0, hi1, hg0, hg1, gq, gk, gv0, gv1, gg0, gg1, glow,
                lbl_ref, hgn_ref, glan_ref, wup_ref, bup_ref,
                yh_ref, yg_ref,
                q_s, k_s, g_s, st_s, fb_s, oi_s, *, tt, layer, unroll):
    nc = tt // REC_CHUNK
    nb = hq0.shape[0]

    @pl.when(pl.program_id(1) == 0)
    def _():
        st_s[...] = jnp.zeros_like(st_s)

    def lower_bound(e):
        lbl = lbl_ref[:, e * LANES:(e + 1) * LANES]
        ex = jnp.exp(lbl - jnp.max(lbl, axis=0, keepdims=True))
        return jnp.sum(ex[:layer + 1], axis=0, keepdims=True) / jnp.sum(ex, axis=0, keepdims=True)

    tot_min = None
    mag_max = None

    def fold_stats(g, mags):
        nonlocal tot_min, mag_max
        tot = jnp.sum(g.reshape(nc, REC_CHUNK, LANES), axis=1)
        tot_min = tot if tot_min is None else jnp.minimum(tot_min, tot)
        for m in mags:
            m = jnp.max(jnp.abs(m).reshape(tt // SUBLANES, SUBLANES, LANES), axis=0)
            mag_max = m if mag_max is None else jnp.maximum(mag_max, m)

    wup = wup_ref[...].astype(BF16)
    lbs = [lower_bound(e) for e in range(2)]
    for bi in range(nb):
        for e, (hq, hf) in enumerate(((hq0, hf0), (hq1, hf1))):
            q = _silu_tanh(hq[bi])
            c1 = 0.5 * (1.0 - lbs[e])
            p = c1 * jnp.tanh(0.5 * hf[bi])
            g = jnp.log2((1.0 - c1) + p)
            q_s[bi, e] = q
            k_s[bi, e] = c1 - p
            g_s[bi, e] = g
            fold_stats(g, (q,))
        logits = _dot(glow[bi].astype(BF16), wup) + bup_ref[...]
        g = (jnp.minimum(logits, 0.0) - jnp.log(1.0 + jnp.exp(-jnp.abs(logits)))) * (LOG2E / GLA_NORMALIZER)
        q = gq[bi] * (GLA_DK ** -0.5)
        k = gk[bi]
        q_s[bi, 2] = q
        k_s[bi, 2] = k
        g_s[bi, 2] = g
        fold_stats(g, (q, k))
    safe = jnp.logical_and(jnp.min(tot_min) >= -2.0 * SAFE_EXP2, jnp.max(mag_max) <= SAFE_MAG)

    sub_iota = lax.broadcasted_iota(jnp.int32, (SUBLANES, LANES), 0)
    lane = lax.broadcasted_iota(jnp.int32, (REC_CHUNK, LANES), 1)
    tok_row = _perm_token_ids((REC_CHUNK, REC_CHUNK), 0)
    tok_col = _perm_token_ids((REC_CHUNK, REC_CHUNK), 1)
    causal = tok_col <= tok_row
    tok_of_row = _perm_token_ids((REC_CHUNK, LANES), 0)
    hgn = hgn_ref[...]
    glan = glan_ref[...]

    bodies = []
    for bi in range(nb):
        bodies += [
            ((bi, 0), None, hi0.at[bi], hg0.at[bi], hgn, yh_ref.at[0, bi], st_s.at[bi, 0]),
            ((bi, 1), None, hi1.at[bi], hg1.at[bi], hgn, yh_ref.at[1, bi], st_s.at[bi, 1]),
            ((bi, 2), lane < GLA_DK, gv0.at[bi], gg0.at[bi], glan, yg_ref.at[0, bi], st_s.at[bi, 2]),
            ((bi, 2), lane >= GLA_DK, gv1.at[bi], gg1.at[bi], glan, yg_ref.at[1, bi], st_s.at[bi, 3]),
        ]

    def load_qkb(slot, base):
        b, b_last = _cumsum_perm(_load_perm(g_s.at[slot], base), sub_iota)
        q = jnp.concatenate(_load_perm(q_s.at[slot], base), axis=0)
        k = jnp.concatenate(_load_perm(k_s.at[slot], base), axis=0)
        return q, k, b, b_last

    def fast_chunk(ci, carry):
        staged = []
        for sub in range(unroll):
            base = pl.multiple_of((ci * unroll + sub) * REC_CHUNK, REC_CHUNK)
            shared = {}
            for slot, qmask, v_ref, gate_ref, gain, y_view, st_view in bodies:
                if slot not in shared:
                    q, k, b, b_last = load_qkb(slot, base)
                    half = 0.5 * b_last
                    d = b - half
                    em = jnp.exp2(half)
                    qt = q * jnp.exp2(d)
                    kt = k * jnp.exp2(-d)
                    shared[slot] = (qt.astype(BF16), kt.astype(BF16), em)
                qt_b, kt_b, em = shared[slot]
                if qmask is not None:
                    qt_b = jnp.where(qmask, qt_b, jnp.zeros_like(qt_b))
                vb = jnp.concatenate(_load_perm(v_ref, base), axis=0).astype(BF16)
                state_e = st_view[...] * em
                scores = _dot_nt(qt_b, kt_b)
                o_inter = _dot_nt(qt_b, state_e.astype(BF16))
                st_view[...] = (state_e + _dot_tn(vb, kt_b)) * em
                staged.append((scores, o_inter, vb, base, gate_ref, gain, y_view))
        for scores, o_inter, vb, base, gate_ref, gain, y_view in staged:
            s = jnp.where(causal, scores, 0.0).astype(BF16)
            o = o_inter + _dot(s, vb)
            gate = jnp.concatenate(_load_perm(gate_ref, base), axis=0)
            _store_perm(y_view, base, _finish(o, gain, gate))
        return carry

    def exact_chunk(ci, carry):
        base = pl.multiple_of(ci * REC_CHUNK, REC_CHUNK)
        for slot, qmask, v_ref, gate_ref, gain, y_view, st_view in bodies:
            q, k, b, b_last = load_qkb(slot, base)
            if qmask is not None:
                q = jnp.where(qmask, q, 0.0)
            v = jnp.concatenate(_load_perm(v_ref, base), axis=0)
            gate = jnp.concatenate(_load_perm(gate_ref, base), axis=0)
            fb_s[0] = b
            fb_s[1] = q
            fb_s[2] = k
            fb_s[3] = v

            def row(r, c2):
                u = r >> 3
                tr = PERM_GROUP * (u >> 2) + PERM_STRIDE * (r & 7) + (u & 3)
                bt = fb_s[0, pl.ds(r, 1), :]
                qt = fb_s[1, pl.ds(r, 1), :]
                w = jnp.exp2(jnp.where(tok_of_row <= tr, bt - fb_s[0], -jnp.inf))
                sc = jnp.sum(qt * w * fb_s[2], axis=-1, keepdims=True)
                oi_s[pl.ds(r, 1), :] = jnp.sum(sc * fb_s[3], axis=0, keepdims=True)
                return c2

            lax.fori_loop(0, REC_CHUNK, row, 0)
            state = st_view[...]
            vb = v.astype(BF16)
            o = _dot_nt((q * jnp.exp2(b)).astype(BF16), state.astype(BF16)) + oi_s[...]
            kd_b = (k * jnp.exp2(b_last - b)).astype(BF16)
            st_view[...] = state * jnp.exp2(b_last) + _dot_tn(vb, kd_b)
            _store_perm(y_view, base, _finish(o, gain, gate))
        return carry

    @pl.when(safe)
    def _():
        lax.fori_loop(0, nc // unroll, fast_chunk, 0)

    @pl.when(jnp.logical_not(safe))
    def _():
        lax.fori_loop(0, nc, exact_chunk, 0)


def _recurrence(mix, lb_logits, hg_norm, gla_norm, wup_pad, bup, *, batch, tokens_per_batch, layer, tt=256):
    n, n_cols = mix.shape
    nt = tokens_per_batch // tt
    pairs = HG_HEADS // 2
    mix3 = mix.reshape(batch, tokens_per_batch, n_cols)
    off = {"hq": 0, "hf": 4, "hi": 8, "hg": 12, "gq": 16, "gk": 18, "gv": 20, "gg": 24, "gl": 28}

    def head(name, e):
        return pl.BlockSpec((batch, tt, LANES), lambda p, i: (0, i, off[name] + 2 * p + e))

    def pair(name):
        return pl.BlockSpec((batch, tt, LANES), lambda p, i: (0, i, off[name] + p))

    in_specs = [
        head("hq", 0), head("hq", 1), head("hf", 0), head("hf", 1),
        head("hi", 0), head("hi", 1), head("hg", 0), head("hg", 1),
        pair("gq"), pair("gk"),
        head("gv", 0), head("gv", 1), head("gg", 0), head("gg", 1),
        pl.BlockSpec((batch, tt, LANES), lambda p, i: (0, i, off["gl"])),
    ]
    in_specs_params = [
        pl.BlockSpec((lb_logits.shape[0], 2 * LANES), lambda p, i: (0, p)),
        pl.BlockSpec((1, LANES), lambda p, i: (0, 0)),
        pl.BlockSpec((1, LANES), lambda p, i: (0, 0)),
        pl.BlockSpec((LANES, LANES), lambda p, i: (0, p)),
        pl.BlockSpec((1, LANES), lambda p, i: (0, p)),
    ]
    out_spec = pl.BlockSpec((2, batch, tt, LANES), lambda p, i: (p, 0, i, 0))
    y_hg, y_gla = pl.pallas_call(
        functools.partial(_rec_kernel, tt=tt, layer=layer, unroll=REC_UNROLL),
        grid=(pairs, nt),
        in_specs=in_specs + in_specs_params,
        out_specs=[out_spec, out_spec],
        out_shape=[jax.ShapeDtypeStruct((HG_HEADS, batch, tokens_per_batch, HG_DV), F32),
                   jax.ShapeDtypeStruct((GLA_HEADS, batch, tokens_per_batch, GLA_DV), F32)],
        scratch_shapes=[
            pltpu.VMEM((batch, 3, tt, LANES), F32),
            pltpu.VMEM((batch, 3, tt, LANES), F32),
            pltpu.VMEM((batch, 3, tt, LANES), F32),
            pltpu.VMEM((batch, 4, LANES, LANES), F32),
            pltpu.VMEM((4, REC_CHUNK, LANES), F32),
            pltpu.VMEM((REC_CHUNK, LANES), F32),
        ],
        compiler_params=pltpu.CompilerParams(
            dimension_semantics=("arbitrary", "arbitrary"), vmem_limit_bytes=48 * MIB),
        name="recurrence",
    )(*([mix3] * 15), lb_logits, hg_norm.reshape(1, HG_DV), gla_norm.reshape(1, GLA_DV), wup_pad,
      bup.reshape(1, -1))
    return y_hg.reshape(HG_HEADS, n, HG_DV), y_gla.reshape(GLA_HEADS, n, GLA_DV)


def _merge_ffn_kernel(x_ref, mod_ref, yh_ref, yg_ref, rh_ref, rg_ref, wuh_ref, wug_ref, wo_ref,
                      gain_ref, wg_ref, wu_ref, wd_ref, fgain_ref, o_ref, w_in_s, w_down_s, *,
                      sub_mix, sub_ffn, d_ff, bounds, final, w_steps):
    step = pl.program_id(0)
    _stage_ffn_weights(step, wg_ref, wu_ref, wd_ref, w_in_s, w_down_s, d_ff)

    @pl.when(step >= w_steps)
    def _():
        gate = mod_ref[0, 3 * sub_mix + 2:3 * sub_mix + 3, :]

        def up(y_ref, w_ref):
            y = jnp.concatenate([y_ref[h].astype(BF16) for h in range(y_ref.shape[0])], axis=-1)
            return _dot(y, w_ref[...])

        merged = (jax.nn.sigmoid(rh_ref[...]) * up(yh_ref, wuh_ref)
                  + jax.nn.sigmoid(rg_ref[...]) * up(yg_ref, wug_ref))
        x = x_ref[...] + gate * _dot(merged.astype(BF16), wo_ref[...])
        o_ref[...] = _ffn_body(x, mod_ref, gain_ref, w_in_s, w_down_s, fgain_ref,
                               sub=sub_ffn, d_ff=d_ff, bounds=bounds, final=final)


def _merge_ffn(x2d, mod3, y_hg, y_gla, r, w_up_hg, w_up_gla, w_out, gain, w_in_all, w_down_all, fgain, *,
               layer, sub_mix, sub_ffn, tokens_per_batch, final, tm=512):
    n, d = x2d.shape
    d_ff = w_down_all.shape[1]
    tiles_per_batch = tokens_per_batch // tm
    w_steps, w_specs, w_scratch = _ffn_weight_specs(layer, d, d_ff)
    const = dict(pipeline_mode=pl.Buffered(1))

    def tile(i):
        return jnp.maximum(i - w_steps, 0)

    return pl.pallas_call(
        functools.partial(_merge_ffn_kernel, sub_mix=sub_mix, sub_ffn=sub_ffn, d_ff=d_ff,
                          bounds=_mxu_aligned_bounds(d_ff, 2), final=final, w_steps=w_steps),
        grid=(w_steps + n // tm,),
        in_specs=[
            pl.BlockSpec((tm, d), lambda i: (tile(i), 0)),
            pl.BlockSpec((1, 3 * N_SUB, d), lambda i: (tile(i) // tiles_per_batch, 0, 0)),
            pl.BlockSpec((HG_HEADS, tm, HG_DV), lambda i: (0, tile(i), 0)),
            pl.BlockSpec((GLA_HEADS, tm, GLA_DV), lambda i: (0, tile(i), 0)),
            pl.BlockSpec((tm, d), lambda i: (tile(i), 0)),
            pl.BlockSpec((tm, d), lambda i: (tile(i), 1)),
            pl.BlockSpec(w_up_hg.shape, lambda i: (0, 0), **const),
            pl.BlockSpec(w_up_gla.shape, lambda i: (0, 0), **const),
            pl.BlockSpec(w_out.shape, lambda i: (0, 0), **const),
            pl.BlockSpec((1, d), lambda i: (0, 0)),
            *w_specs,
            pl.BlockSpec((1, d), lambda i: (0, 0)),
        ],
        out_specs=pl.BlockSpec((tm, d), lambda i: (tile(i), 0)),
        out_shape=jax.ShapeDtypeStruct((n, d), F32),
        scratch_shapes=w_scratch,
        compiler_params=pltpu.CompilerParams(dimension_semantics=("arbitrary",), vmem_limit_bytes=58 * MIB),
        name="merge_ffn",
    )(x2d, mod3, y_hg, y_gla, r, r, w_up_hg, w_up_gla, w_out, gain.reshape(1, d), w_in_all, w_in_all,
      w_down_all, fgain.reshape(1, d))


def kernel(x, c, w_ada, b_ada, norm_gains, ffn1_w_in, ffn1_w_down, w_in_mix, w_gk_up, b_gk_up, lb_logits,
           hg_norm, gla_norm, w_up_hg, w_up_gla, w_out, ffn2_w_in, ffn2_w_down, final_norm):
    batch, seq, d = x.shape
    depth = w_ada.shape[0]
    hg_qk = HG_HEADS * HG_DK
    hg_w = HG_HEADS * HG_DV
    gla_k = GLA_HEADS * GLA_DK
    gla_v = GLA_HEADS * GLA_DV
    n_rec = 2 * hg_qk + 2 * hg_w + 2 * gla_k + 2 * gla_v

    x2d = x.reshape(batch * seq, d)
    c_pad = jnp.pad(c, ((0, SUBLANES - batch % SUBLANES if batch % SUBLANES else 0), (0, 0)))
    for l in range(depth):
        mod = _ada(c_pad, w_ada[l], b_ada[l])[:batch].reshape(batch, 3 * N_SUB, d)

        x2d = _ffn(x2d, mod, norm_gains[l, 0], ffn1_w_in, ffn1_w_down, final_norm, layer=l, sub=0,
                   tokens_per_batch=seq, final=False)

        mix, r = _inproj(x2d, mod, norm_gains[l, 1], jnp.swapaxes(w_in_mix, 1, 2), n_rec, layer=l, sub=1,
                         tokens_per_batch=seq)

        wup_pad = jnp.pad(w_gk_up[l], ((0, LANES - GLA_RANK), (0, 0)))
        y_hg, y_gla = _recurrence(mix, lb_logits, hg_norm[l], gla_norm[l], wup_pad, b_gk_up[l],
                                  batch=batch, tokens_per_batch=seq, layer=l)

        x2d = _merge_ffn(x2d, mod, y_hg, y_gla, r, w_up_hg[l].astype(BF16), w_up_gla[l].astype(BF16),
                         w_out[l].astype(BF16), norm_gains[l, 2], ffn2_w_in, ffn2_w_down, final_norm, layer=l,
                         sub_mix=1, sub_ffn=2, tokens_per_batch=seq, final=l == depth - 1)
    if depth == 0:
        raise ValueError("depth must be >= 1")
    return x2d.reshape(batch, seq, d)
```

```python
import functools

import jax
import jax.numpy as jnp
from jax import lax
from jax.experimental import pallas as pl
from jax.experimental.pallas import tpu as pltpu

F32 = jnp.float32
BF16 = jnp.bfloat16

EPS = 1e-6
N_SUB = 3
HG_HEADS = 4
HG_DK = 128
HG_DV = 128
GLA_HEADS = 4
GLA_DK = 64
GLA_DV = 128
GLA_RANK = 16
GLA_NORMALIZER = 16.0

LANES = 128
SUBLANES = 8
MXU_WIDTH = 256
MIB = 1024 * 1024

REC_CHUNK = 64
PERM_STRIDE = 4
PERM_GROUP = PERM_STRIDE * SUBLANES
N_GROUPS = REC_CHUNK // PERM_GROUP
REC_UNROLL = 2
LOG2E = 1.4426950408889634
SAFE_EXP2 = 100.0
SAFE_MAG = 1e7


def _dot(a, b):
    return jnp.dot(a, b, preferred_element_type=F32)


def _dot_nt(a, b):
    return lax.dot_general(a, b, (((1,), (1,)), ((), ())), preferred_element_type=F32)


def _dot_tn(a, b):
    return lax.dot_general(a, b, (((0,), (0,)), ((), ())), preferred_element_type=F32)


def _silu(v):
    return v * jax.nn.sigmoid(v)


def _norm_mod(x, gain, shift, scale):
    y = x * lax.rsqrt(jnp.mean(x * x, axis=-1, keepdims=True) + EPS) * gain
    return y * (1.0 + scale) + shift


def _ada_kernel(c_ref, w_ref, b_ref, o_ref):
    cond = _silu(c_ref[...]).astype(BF16)
    o_ref[...] = _dot(cond, w_ref[...].astype(BF16)) + b_ref[...]


def _ada(c_pad, w, b):
    rows, d = c_pad.shape
    n = w.shape[1]
    bn = n // 8
    return pl.pallas_call(
        _ada_kernel,
        grid=(n // bn,),
        in_specs=[
            pl.BlockSpec((rows, d), lambda j: (0, 0)),
            pl.BlockSpec((d, bn), lambda j: (0, j)),
            pl.BlockSpec((1, bn), lambda j: (0, j)),
        ],
        out_specs=pl.BlockSpec((rows, bn), lambda j: (0, j)),
        out_shape=jax.ShapeDtypeStruct((rows, n), F32),
        name="adaln",
    )(c_pad, w, b.reshape(1, n))


def _mxu_aligned_bounds(width, parts):
    if width % MXU_WIDTH:
        return (0, width)
    tiles = width // MXU_WIDTH
    return tuple(MXU_WIDTH * ((tiles * p + parts - 1) // parts) for p in range(parts)) + (width,)


def _ffn_weight_specs(layer, d, d_ff):
    assert d_ff % MXU_WIDTH == 0
    steps = d_ff // MXU_WIDTH

    def chunk(i):
        return jnp.minimum(i, steps - 1)

    specs = [
        pl.BlockSpec((None, d, MXU_WIDTH), lambda i: (layer, 0, chunk(i))),
        pl.BlockSpec((None, d, MXU_WIDTH), lambda i: (layer, 0, steps + chunk(i))),
        pl.BlockSpec((None, MXU_WIDTH, d), lambda i: (layer, chunk(i), 0)),
    ]
    scratch = [pltpu.VMEM((d, 2 * d_ff), BF16), pltpu.VMEM((d_ff, d), BF16)]
    return steps, specs, scratch


def _stage_ffn_weights(step, wg_ref, wu_ref, wd_ref, w_in_s, w_down_s, d_ff):
    for j in range(d_ff // MXU_WIDTH):
        @pl.when(step == j)
        def _(lo=j * MXU_WIDTH):
            w_in_s[:, lo:lo + MXU_WIDTH] = wg_ref[...].astype(BF16)
            w_in_s[:, d_ff + lo:d_ff + lo + MXU_WIDTH] = wu_ref[...].astype(BF16)
            w_down_s[lo:lo + MXU_WIDTH, :] = wd_ref[...].astype(BF16)


def _ffn_kernel(x_ref, mod_ref, gain_ref, wg_ref, wu_ref, wd_ref, fgain_ref, o_ref, w_in_s, w_down_s, *,
                sub, d_ff, bounds, final, w_steps):
    step = pl.program_id(0)
    _stage_ffn_weights(step, wg_ref, wu_ref, wd_ref, w_in_s, w_down_s, d_ff)

    @pl.when(step >= w_steps)
    def _():
        o_ref[...] = _ffn_body(x_ref[...], mod_ref, gain_ref, w_in_s, w_down_s, fgain_ref,
                               sub=sub, d_ff=d_ff, bounds=bounds, final=final)


def _ffn_body(x, mod_ref, gain_ref, w_in_ref, w_down_ref, fgain_ref, *, sub, d_ff, bounds, final):
    shift = mod_ref[0, 3 * sub + 0:3 * sub + 1, :]
    scale = mod_ref[0, 3 * sub + 1:3 * sub + 2, :]
    gate = mod_ref[0, 3 * sub + 2:3 * sub + 3, :]
    hb = _norm_mod(x, gain_ref[...], shift, scale).astype(BF16)
    acc = None
    for lo, hi in zip(bounds[:-1], bounds[1:]):
        g = _dot(hb, w_in_ref[:, lo:hi])
        u = _dot(hb, w_in_ref[:, d_ff + lo:d_ff + hi])
        act = (_silu(g) * u).astype(BF16)
        part = _dot(act, w_down_ref[lo:hi, :])
        acc = part if acc is None else acc + part
    xn = x + (0.5 * gate) * acc
    if final:
        xn = xn * lax.rsqrt(jnp.mean(xn * xn, axis=-1, keepdims=True) + EPS) * fgain_ref[...]
    return xn


def _ffn(x2d, mod3, gain, w_in_all, w_down_all, fgain, *, layer, sub, tokens_per_batch, final, tm=512):
    n, d = x2d.shape
    d_ff = w_down_all.shape[1]
    tiles_per_batch = tokens_per_batch // tm
    w_steps, w_specs, w_scratch = _ffn_weight_specs(layer, d, d_ff)

    def tile(i):
        return jnp.maximum(i - w_steps, 0)

    return pl.pallas_call(
        functools.partial(_ffn_kernel, sub=sub, d_ff=d_ff, bounds=_mxu_aligned_bounds(d_ff, 2), final=final,
                          w_steps=w_steps),
        grid=(w_steps + n // tm,),
        in_specs=[
            pl.BlockSpec((tm, d), lambda i: (tile(i), 0)),
            pl.BlockSpec((1, 3 * N_SUB, d), lambda i: (tile(i) // tiles_per_batch, 0, 0)),
            pl.BlockSpec((1, d), lambda i: (0, 0)),
            *w_specs,
            pl.BlockSpec((1, d), lambda i: (0, 0)),
        ],
        out_specs=pl.BlockSpec((tm, d), lambda i: (tile(i), 0)),
        out_shape=jax.ShapeDtypeStruct((n, d), F32),
        scratch_shapes=w_scratch,
        compiler_params=pltpu.CompilerParams(dimension_semantics=("arbitrary",), vmem_limit_bytes=52 * MIB),
        name="ffn_final" if final else "ffn",
    )(x2d, mod3, gain.reshape(1, d), w_in_all, w_in_all, w_down_all, fgain.reshape(1, d))


MIX_W_CHUNK = 512


MIX_OFF = {"q": 0, "k": 4, "g": 8, "v": 12, "sg": 16, "gq": 20, "gk": 22, "gd": 24, "gv": 26, "gsg": 30}
MIX_TILES = 34


def _inproj_kernel(x_ref, mod_ref, gain_ref, wt_ref, lbl_ref, wup_ref, bup_ref, mix_ref, r_ref, stats_ref, wt_s, *,
                   sub, layer, n_cols, w_steps):
    step = pl.program_id(0)
    for j in range(w_steps):
        @pl.when(step == j)
        def _(lo=j * MIX_W_CHUNK):
            w = wt_ref[...]
            if lo + MIX_W_CHUNK > n_cols:
                row = lo + lax.broadcasted_iota(jnp.int32, w.shape, 0)
                w = jnp.where(row < n_cols, w, 0.0)
            wt_s[lo:lo + MIX_W_CHUNK, :] = w.astype(BF16)

    @pl.when(step >= w_steps)
    def _():
        shift = mod_ref[0, 3 * sub + 0:3 * sub + 1, :]
        scale = mod_ref[0, 3 * sub + 1:3 * sub + 2, :]
        hb = _norm_mod(x_ref[...], gain_ref[...], shift, scale).astype(BF16)
        tm = hb.shape[0]
        hg_qk, hg_w = HG_HEADS * HG_DK, HG_HEADS * HG_DV
        gla_k, gla_v = GLA_HEADS * GLA_DK, GLA_HEADS * GLA_DV
        n_hg = 2 * hg_qk + 2 * hg_w
        n_rec = n_hg + 2 * gla_k + 2 * gla_v

        def put(name, val):
            lo = MIX_OFF[name] * LANES
            mix_ref[:, lo:lo + val.shape[1]] = val

        def chunk_min(g):
            tot = jnp.sum(g.reshape(tm // REC_CHUNK, REC_CHUNK, g.shape[1]), axis=1)
            return functools.reduce(jnp.minimum, [tot[:, j:j + LANES] for j in range(0, g.shape[1], LANES)])

        def mag_max(v):
            m = jnp.max(jnp.abs(v).reshape(tm // SUBLANES, SUBLANES, v.shape[1]), axis=0)
            return functools.reduce(jnp.maximum, [m[:, j:j + LANES] for j in range(0, v.shape[1], LANES)])

        ph = _dot_nt(hb, wt_s[:n_hg, :])
        lbl = lbl_ref[...]
        ex = jnp.exp(lbl - jnp.max(lbl, axis=0, keepdims=True))
        lb = jnp.sum(ex[:layer + 1], axis=0, keepdims=True) / jnp.sum(ex, axis=0, keepdims=True)
        c1 = 0.5 * (1.0 - lb)
        q = _silu_tanh(ph[:, :hg_qk])
        p = c1 * jnp.tanh(0.5 * ph[:, hg_qk:2 * hg_qk])
        g = jnp.log2((1.0 - c1) + p)
        put("q", q)
        put("k", c1 - p)
        put("g", g)
        put("v", ph[:, 2 * hg_qk:2 * hg_qk + hg_w])
        put("sg", _silu_tanh(ph[:, 2 * hg_qk + hg_w:]))

        pg = _dot_nt(hb, wt_s[n_hg:n_rec, :])
        code = _dot_nt(hb, wt_s[n_rec:n_rec + LANES, :])
        logits = _dot(code.astype(BF16), wup_ref[...].astype(BF16)) + bup_ref[...]
        gd = (jnp.minimum(logits, 0.0) - jnp.log(1.0 + jnp.exp(-jnp.abs(logits)))) * (LOG2E / GLA_NORMALIZER)
        gq = pg[:, :gla_k] * (GLA_DK ** -0.5)
        gk = pg[:, gla_k:2 * gla_k]
        put("gq", gq)
        put("gk", gk)
        put("gd", gd)
        put("gv", pg[:, 2 * gla_k:2 * gla_k + gla_v])
        put("gsg", _silu_tanh(pg[:, 2 * gla_k + gla_v:]))

        r_ref[...] = _dot_nt(hb, wt_s[n_rec + GLA_RANK:n_rec + GLA_RANK + r_ref.shape[1], :])
        stats_ref[0, 0] = jnp.minimum(chunk_min(g), chunk_min(gd))
        stats_ref[0, 1] = functools.reduce(jnp.maximum, [mag_max(q), mag_max(gq), mag_max(gk)])


def _inproj(x2d, mod3, gain, wt_all, lb_logits, wup_pad, bup, *, layer, sub, tokens_per_batch, tm=512):
    n, d = x2d.shape
    n_cols = wt_all.shape[1]
    n_rec = 2 * HG_HEADS * HG_DK + 2 * HG_HEADS * HG_DV + 2 * GLA_HEADS * GLA_DK + 2 * GLA_HEADS * GLA_DV
    n_r = n_cols - n_rec - GLA_RANK
    w_steps = pl.cdiv(n_cols, MIX_W_CHUNK)
    assert n_rec + LANES <= w_steps * MIX_W_CHUNK and tm // REC_CHUNK == SUBLANES
    tiles_per_batch = tokens_per_batch // tm

    def tile(i):
        return jnp.maximum(i - w_steps, 0)

    return pl.pallas_call(
        functools.partial(_inproj_kernel, sub=sub, layer=layer, n_cols=n_cols, w_steps=w_steps),
        grid=(w_steps + n // tm,),
        in_specs=[
            pl.BlockSpec((tm, d), lambda i: (tile(i), 0)),
            pl.BlockSpec((1, 3 * N_SUB, d), lambda i: (tile(i) // tiles_per_batch, 0, 0)),
            pl.BlockSpec((1, d), lambda i: (0, 0)),
            pl.BlockSpec((None, MIX_W_CHUNK, d), lambda i: (layer, jnp.minimum(i, w_steps - 1), 0)),
            pl.BlockSpec(lb_logits.shape, lambda i: (0, 0)),
            pl.BlockSpec(wup_pad.shape, lambda i: (0, 0)),
            pl.BlockSpec((1, bup.shape[0]), lambda i: (0, 0)),
        ],
        out_specs=[
            pl.BlockSpec((tm, MIX_TILES * LANES), lambda i: (tile(i), 0)),
            pl.BlockSpec((tm, n_r), lambda i: (tile(i), 0)),
            pl.BlockSpec((1, 2, SUBLANES, LANES), lambda i: (tile(i), 0, 0, 0)),
        ],
        out_shape=[
            jax.ShapeDtypeStruct((n, MIX_TILES * LANES), F32),
            jax.ShapeDtypeStruct((n, n_r), F32),
            jax.ShapeDtypeStruct((n // tm, 2, SUBLANES, LANES), F32),
        ],
        scratch_shapes=[pltpu.VMEM((w_steps * MIX_W_CHUNK, d), BF16)],
        compiler_params=pltpu.CompilerParams(dimension_semantics=("arbitrary",), vmem_limit_bytes=56 * MIB),
        name="inproj",
    )(x2d, mod3, gain.reshape(1, d), wt_all, lb_logits, wup_pad, bup.reshape(1, -1))


def _load_perm(ref, base):
    return [ref[pl.ds(base + PERM_GROUP * c + a, SUBLANES, stride=PERM_STRIDE), :]
            for c in range(N_GROUPS) for a in range(PERM_STRIDE)]


def _store_perm(ref, base, val):
    for c in range(N_GROUPS):
        for a in range(PERM_STRIDE):
            u = PERM_STRIDE * c + a
            ref[pl.ds(base + PERM_GROUP * c + a, SUBLANES, stride=PERM_STRIDE), :] = (
                val[SUBLANES * u:SUBLANES * (u + 1), :])


def _shift_down(v, d, sub_iota):
    return jnp.where(sub_iota >= d, pltpu.roll(v, d, 0), 0.0)


def _cumsum_perm(g, sub_iota):
    out = []
    carry = None
    for c in range(N_GROUPS):
        pre = [g[PERM_STRIDE * c]]
        for a in range(1, PERM_STRIDE):
            pre.append(pre[-1] + g[PERM_STRIDE * c + a])
        inc = pre[-1]
        inc = inc + _shift_down(inc, 1, sub_iota)
        inc = inc + _shift_down(inc, 2, sub_iota)
        inc = inc + _shift_down(inc, 4, sub_iota)
        exc = _shift_down(inc, 1, sub_iota)
        if carry is not None:
            exc = exc + carry
        out.extend(p + exc for p in pre)
        tot = inc[SUBLANES - 1:SUBLANES, :]
        carry = tot if carry is None else carry + tot
    return jnp.concatenate(out, axis=0), carry


def _perm_token_ids(shape, dim):
    r = lax.broadcasted_iota(jnp.int32, shape, dim)
    u = r >> 3
    return PERM_GROUP * (u >> 2) + PERM_STRIDE * (r & 7) + (u & 3)


def _silu_tanh(v):
    h = 0.5 * v
    return h * jnp.tanh(h) + h


def _finish(o, gain, act_gate):
    return o * lax.rsqrt(jnp.mean(o * o, axis=-1, keepdims=True) + EPS) * gain * act_gate


def _rec_kernel(safe_ref, q0, q1, k0, k1, g0, g1, v0, v1, sg0, sg1, gq, gk, gd, gv0, gv1, gsg0, gsg1,
                hgn_ref, glan_ref, yh_ref, yg_ref, st_s, fb_s, oi_s, *, tt, unroll):
    nc = tt // REC_CHUNK
    nb = q0.shape[0]

    @pl.when(pl.program_id(1) == 0)
    def _():
        st_s[...] = jnp.zeros_like(st_s)

    safe = safe_ref[pl.program_id(1)] != 0

    sub_iota = lax.broadcasted_iota(jnp.int32, (SUBLANES, LANES), 0)
    lane = lax.broadcasted_iota(jnp.int32, (REC_CHUNK, LANES), 1)
    tok_row = _perm_token_ids((REC_CHUNK, REC_CHUNK), 0)
    tok_col = _perm_token_ids((REC_CHUNK, REC_CHUNK), 1)
    causal = tok_col <= tok_row
    tok_of_row = _perm_token_ids((REC_CHUNK, LANES), 0)
    hgn = hgn_ref[...]
    glan = glan_ref[...]

    qkg = ((q0, k0, g0), (q1, k1, g1), (gq, gk, gd))
    bodies = []
    for bi in range(nb):
        bodies += [
            ((bi, 0), None, v0.at[bi], sg0.at[bi], hgn, yh_ref.at[0, bi], st_s.at[bi, 0]),
            ((bi, 1), None, v1.at[bi], sg1.at[bi], hgn, yh_ref.at[1, bi], st_s.at[bi, 1]),
            ((bi, 2), lane < GLA_DK, gv0.at[bi], gsg0.at[bi], glan, yg_ref.at[0, bi], st_s.at[bi, 2]),
            ((bi, 2), lane >= GLA_DK, gv1.at[bi], gsg1.at[bi], glan, yg_ref.at[1, bi], st_s.at[bi, 3]),
        ]

    def load_qkb(slot, base):
        bi, grp = slot
        q_ref, k_ref, g_ref = qkg[grp]
        b, b_last = _cumsum_perm(_load_perm(g_ref.at[bi], base), sub_iota)
        q = jnp.concatenate(_load_perm(q_ref.at[bi], base), axis=0)
        k = jnp.concatenate(_load_perm(k_ref.at[bi], base), axis=0)
        return q, k, b, b_last

    def fast_chunk(ci, carry):
        staged = []
        for sub in range(unroll):
            base = pl.multiple_of((ci * unroll + sub) * REC_CHUNK, REC_CHUNK)
            shared = {}
            for slot, qmask, v_ref, gate_ref, gain, y_view, st_view in bodies:
                if slot not in shared:
                    q, k, b, b_last = load_qkb(slot, base)
                    half = 0.5 * b_last
                    d = b - half
                    em = jnp.exp2(half)
                    qt = q * jnp.exp2(d)
                    kt = k * jnp.exp2(-d)
                    shared[slot] = (qt.astype(BF16), kt.astype(BF16), em)
                qt_b, kt_b, em = shared[slot]
                if qmask is not None:
                    qt_b = jnp.where(qmask, qt_b, jnp.zeros_like(qt_b))
                vb = jnp.concatenate(_load_perm(v_ref, base), axis=0).astype(BF16)
                state_e = st_view[...] * em
                scores = _dot_nt(qt_b, kt_b)
                o_inter = _dot_nt(qt_b, state_e.astype(BF16))
                st_view[...] = (state_e + _dot_tn(vb, kt_b)) * em
                staged.append((scores, o_inter, vb, base, gate_ref, gain, y_view))
        for scores, o_inter, vb, base, gate_ref, gain, y_view in staged:
            s = jnp.where(causal, scores, 0.0).astype(BF16)
            o = o_inter + _dot(s, vb)
            gate = jnp.concatenate(_load_perm(gate_ref, base), axis=0)
            _store_perm(y_view, base, _finish(o, gain, gate))
        return carry

    def exact_chunk(ci, carry):
        base = pl.multiple_of(ci * REC_CHUNK, REC_CHUNK)
        for slot, qmask, v_ref, gate_ref, gain, y_view, st_view in bodies:
            q, k, b, b_last = load_qkb(slot, base)
            if qmask is not None:
                q = jnp.where(qmask, q, 0.0)
            v = jnp.concatenate(_load_perm(v_ref, base), axis=0)
            gate = jnp.concatenate(_load_perm(gate_ref, base), axis=0)
            fb_s[0] = b
            fb_s[1] = q
            fb_s[2] = k
            fb_s[3] = v

            def row(r, c2):
                u = r >> 3
                tr = PERM_GROUP * (u >> 2) + PERM_STRIDE * (r & 7) + (u & 3)
                bt = fb_s[0, pl.ds(r, 1), :]
                qt = fb_s[1, pl.ds(r, 1), :]
                w = jnp.exp2(jnp.where(tok_of_row <= tr, bt - fb_s[0], -jnp.inf))
                sc = jnp.sum(qt * w * fb_s[2], axis=-1, keepdims=True)
                oi_s[pl.ds(r, 1), :] = jnp.sum(sc * fb_s[3], axis=0, keepdims=True)
                return c2

            lax.fori_loop(0, REC_CHUNK, row, 0)
            state = st_view[...]
            vb = v.astype(BF16)
            o = _dot_nt((q * jnp.exp2(b)).astype(BF16), state.astype(BF16)) + oi_s[...]
            kd_b = (k * jnp.exp2(b_last - b)).astype(BF16)
            st_view[...] = state * jnp.exp2(b_last) + _dot_tn(vb, kd_b)
            _store_perm(y_view, base, _finish(o, gain, gate))
        return carry

    @pl.when(safe)
    def _():
        lax.fori_loop(0, nc // unroll, fast_chunk, 0)

    @pl.when(jnp.logical_not(safe))
    def _():
        lax.fori_loop(0, nc, exact_chunk, 0)


def _safe_flags(stats, *, batch, tokens_per_batch, tt):
    tiles_per_batch = stats.shape[0] // batch
    chunk_tot = jnp.min(stats[:, 0], axis=-1).reshape(batch, tokens_per_batch // tt, -1)
    tile_mag = jnp.max(stats[:, 1], axis=(-2, -1)).reshape(batch, tiles_per_batch)
    step_tot = jnp.min(chunk_tot, axis=(0, 2))
    step_mag = jnp.repeat(jnp.max(tile_mag, axis=0), (tokens_per_batch // tt) // tiles_per_batch)
    return jnp.logical_and(step_tot >= -2.0 * SAFE_EXP2, step_mag <= SAFE_MAG).astype(jnp.int32)


def _recurrence(mix, stats, hg_norm, gla_norm, *, batch, tokens_per_batch, tt=256):
    n, n_cols = mix.shape
    nt = tokens_per_batch // tt
    pairs = HG_HEADS // 2
    mix3 = mix.reshape(batch, tokens_per_batch, n_cols)
    safe = _safe_flags(stats, batch=batch, tokens_per_batch=tokens_per_batch, tt=tt)

    def head(name, e):
        return pl.BlockSpec((batch, tt, LANES), lambda p, i, flags: (0, i, MIX_OFF[name] + 2 * p + e))

    def pair(name):
        return pl.BlockSpec((batch, tt, LANES), lambda p, i, flags: (0, i, MIX_OFF[name] + p))

    in_specs = [
        head("q", 0), head("q", 1), head("k", 0), head("k", 1), head("g", 0), head("g", 1),
        head("v", 0), head("v", 1), head("sg", 0), head("sg", 1),
        pair("gq"), pair("gk"), pair("gd"),
        head("gv", 0), head("gv", 1), head("gsg", 0), head("gsg", 1),
        pl.BlockSpec((1, LANES), lambda p, i, flags: (0, 0)),
        pl.BlockSpec((1, LANES), lambda p, i, flags: (0, 0)),
    ]
    out_spec = pl.BlockSpec((2, batch, tt, LANES), lambda p, i, flags: (p, 0, i, 0))
    y_hg, y_gla = pl.pallas_call(
        functools.partial(_rec_kernel, tt=tt, unroll=REC_UNROLL),
        grid_spec=pltpu.PrefetchScalarGridSpec(
            num_scalar_prefetch=1,
            grid=(pairs, nt),
            in_specs=in_specs,
            out_specs=[out_spec, out_spec],
            scratch_shapes=[
                pltpu.VMEM((batch, 4, LANES, LANES), F32),
                pltpu.VMEM((4, REC_CHUNK, LANES), F32),
                pltpu.VMEM((REC_CHUNK, LANES), F32),
            ],
        ),
        out_shape=[jax.ShapeDtypeStruct((HG_HEADS, batch, tokens_per_batch, HG_DV), F32),
                   jax.ShapeDtypeStruct((GLA_HEADS, batch, tokens_per_batch, GLA_DV), F32)],
        compiler_params=pltpu.CompilerParams(
            dimension_semantics=("arbitrary", "arbitrary"), vmem_limit_bytes=48 * MIB),
        name="recurrence",
    )(safe, *([mix3] * 17), hg_norm.reshape(1, HG_DV), gla_norm.reshape(1, GLA_DV))
    return y_hg.reshape(HG_HEADS, n, HG_DV), y_gla.reshape(GLA_HEADS, n, GLA_DV)


def _merge_ffn_kernel(x_ref, mod_ref, yh_ref, yg_ref, rh_ref, rg_ref, wuh_ref, wug_ref, wo_ref,
                      gain_ref, wg_ref, wu_ref, wd_ref, fgain_ref, o_ref, w_in_s, w_down_s, *,
                      sub_mix, sub_ffn, d_ff, bounds, final, w_steps):
    step = pl.program_id(0)
    _stage_ffn_weights(step, wg_ref, wu_ref, wd_ref, w_in_s, w_down_s, d_ff)

    @pl.when(step >= w_steps)
    def _():
        gate = mod_ref[0, 3 * sub_mix + 2:3 * sub_mix + 3, :]

        def up(y_ref, w_ref):
            y = jnp.concatenate([y_ref[h].astype(BF16) for h in range(y_ref.shape[0])], axis=-1)
            return _dot(y, w_ref[...])

        merged = (jax.nn.sigmoid(rh_ref[...]) * up(yh_ref, wuh_ref)
                  + jax.nn.sigmoid(rg_ref[...]) * up(yg_ref, wug_ref))
        x = x_ref[...] + gate * _dot(merged.astype(BF16), wo_ref[...])
        o_ref[...] = _ffn_body(x, mod_ref, gain_ref, w_in_s, w_down_s, fgain_ref,
                               sub=sub_ffn, d_ff=d_ff, bounds=bounds, final=final)


def _merge_ffn(x2d, mod3, y_hg, y_gla, r, w_up_hg, w_up_gla, w_out, gain, w_in_all, w_down_all, fgain, *,
               layer, sub_mix, sub_ffn, tokens_per_batch, final, tm=512):
    n, d = x2d.shape
    d_ff = w_down_all.shape[1]
    tiles_per_batch = tokens_per_batch // tm
    w_steps, w_specs, w_scratch = _ffn_weight_specs(layer, d, d_ff)
    const = dict(pipeline_mode=pl.Buffered(1))

    def tile(i):
        return jnp.maximum(i - w_steps, 0)

    return pl.pallas_call(
        functools.partial(_merge_ffn_kernel, sub_mix=sub_mix, sub_ffn=sub_ffn, d_ff=d_ff,
                          bounds=_mxu_aligned_bounds(d_ff, 2), final=final, w_steps=w_steps),
        grid=(w_steps + n // tm,),
        in_specs=[
            pl.BlockSpec((tm, d), lambda i: (tile(i), 0)),
            pl.BlockSpec((1, 3 * N_SUB, d), lambda i: (tile(i) // tiles_per_batch, 0, 0)),
            pl.BlockSpec((HG_HEADS, tm, HG_DV), lambda i: (0, tile(i), 0)),
            pl.BlockSpec((GLA_HEADS, tm, GLA_DV), lambda i: (0, tile(i), 0)),
            pl.BlockSpec((tm, d), lambda i: (tile(i), 0)),
            pl.BlockSpec((tm, d), lambda i: (tile(i), 1)),
            pl.BlockSpec(w_up_hg.shape, lambda i: (0, 0), **const),
            pl.BlockSpec(w_up_gla.shape, lambda i: (0, 0), **const),
            pl.BlockSpec(w_out.shape, lambda i: (0, 0), **const),
            pl.BlockSpec((1, d), lambda i: (0, 0)),
            *w_specs,
            pl.BlockSpec((1, d), lambda i: (0, 0)),
        ],
        out_specs=pl.BlockSpec((tm, d), lambda i: (tile(i), 0)),
        out_shape=jax.ShapeDtypeStruct((n, d), F32),
        scratch_shapes=w_scratch,
        compiler_params=pltpu.CompilerParams(dimension_semantics=("arbitrary",), vmem_limit_bytes=58 * MIB),
        name="merge_ffn",
    )(x2d, mod3, y_hg, y_gla, r, r, w_up_hg, w_up_gla, w_out, gain.reshape(1, d), w_in_all, w_in_all,
      w_down_all, fgain.reshape(1, d))


def kernel(x, c, w_ada, b_ada, norm_gains, ffn1_w_in, ffn1_w_down, w_in_mix, w_gk_up, b_gk_up, lb_logits,
           hg_norm, gla_norm, w_up_hg, w_up_gla, w_out, ffn2_w_in, ffn2_w_down, final_norm):
    batch, seq, d = x.shape
    depth = w_ada.shape[0]

    x2d = x.reshape(batch * seq, d)
    c_pad = jnp.pad(c, ((0, SUBLANES - batch % SUBLANES if batch % SUBLANES else 0), (0, 0)))
    for l in range(depth):
        mod = _ada(c_pad, w_ada[l], b_ada[l])[:batch].reshape(batch, 3 * N_SUB, d)

        x2d = _ffn(x2d, mod, norm_gains[l, 0], ffn1_w_in, ffn1_w_down, final_norm, layer=l, sub=0,
                   tokens_per_batch=seq, final=False)

        wup_pad = jnp.pad(w_gk_up[l], ((0, LANES - GLA_RANK), (0, 0)))
        mix, r, stats = _inproj(x2d, mod, norm_gains[l, 1], jnp.swapaxes(w_in_mix, 1, 2), lb_logits, wup_pad,
                                b_gk_up[l], layer=l, sub=1, tokens_per_batch=seq)

        y_hg, y_gla = _recurrence(mix, stats, hg_norm[l], gla_norm[l], batch=batch, tokens_per_batch=seq)

        x2d = _merge_ffn(x2d, mod, y_hg, y_gla, r, w_up_hg[l].astype(BF16), w_up_gla[l].astype(BF16),
                         w_out[l].astype(BF16), norm_gains[l, 2], ffn2_w_in, ffn2_w_down, final_norm, layer=l,
                         sub_mix=1, sub_ffn=2, tokens_per_batch=seq, final=l == depth - 1)
    if depth == 0:
        raise ValueError("depth must be >= 1")
    return x2d.reshape(batch, seq, d)
```

```python
import functools

import jax
import jax.numpy as jnp
from jax import lax
from jax.experimental import pallas as pl
from jax.experimental.pallas import tpu as pltpu

F32 = jnp.float32
BF16 = jnp.bfloat16

EPS = 1e-6
N_SUB = 3
HG_HEADS = 4
HG_DK = 128
HG_DV = 128
GLA_HEADS = 4
GLA_DK = 64
GLA_DV = 128
GLA_RANK = 16
GLA_NORMALIZER = 16.0

LANES = 128
SUBLANES = 8
MXU_WIDTH = 256
MIB = 1024 * 1024

REC_CHUNK = 64
PERM_STRIDE = 4
PERM_GROUP = PERM_STRIDE * SUBLANES
N_GROUPS = REC_CHUNK // PERM_GROUP
REC_UNROLL = 2
LOG2E = 1.4426950408889634
SAFE_EXP2 = 100.0
SAFE_MAG = 1e7


def _dot(a, b):
    return jnp.dot(a, b, preferred_element_type=F32)


def _dot_nt(a, b):
    return lax.dot_general(a, b, (((1,), (1,)), ((), ())), preferred_element_type=F32)


def _dot_tn(a, b):
    return lax.dot_general(a, b, (((0,), (0,)), ((), ())), preferred_element_type=F32)


def _silu(v):
    return v * jax.nn.sigmoid(v)


def _norm_mod(x, gain, shift, scale):
    y = x * lax.rsqrt(jnp.mean(x * x, axis=-1, keepdims=True) + EPS) * gain
    return y * (1.0 + scale) + shift


def _ada_kernel(c_ref, w_ref, b_ref, o_ref):
    cond = _silu(c_ref[...]).astype(BF16)
    o_ref[...] = _dot(cond, w_ref[...].astype(BF16)) + b_ref[...]


def _ada(c_pad, w, b):
    rows, d = c_pad.shape
    n = w.shape[1]
    bn = n // 8
    return pl.pallas_call(
        _ada_kernel,
        grid=(n // bn,),
        in_specs=[
            pl.BlockSpec((rows, d), lambda j: (0, 0)),
            pl.BlockSpec((d, bn), lambda j: (0, j)),
            pl.BlockSpec((1, bn), lambda j: (0, j)),
        ],
        out_specs=pl.BlockSpec((rows, bn), lambda j: (0, j)),
        out_shape=jax.ShapeDtypeStruct((rows, n), F32),
        name="adaln",
    )(c_pad, w, b.reshape(1, n))


def _mxu_aligned_bounds(width, parts):
    if width % MXU_WIDTH:
        return (0, width)
    tiles = width // MXU_WIDTH
    return tuple(MXU_WIDTH * ((tiles * p + parts - 1) // parts) for p in range(parts)) + (width,)


def _ffn_weight_specs(layer, d, d_ff):
    assert d_ff % MXU_WIDTH == 0
    steps = d_ff // MXU_WIDTH

    def chunk(i):
        return jnp.minimum(i, steps - 1)

    specs = [
        pl.BlockSpec((None, d, MXU_WIDTH), lambda i: (layer, 0, chunk(i))),
        pl.BlockSpec((None, d, MXU_WIDTH), lambda i: (layer, 0, steps + chunk(i))),
        pl.BlockSpec((None, MXU_WIDTH, d), lambda i: (layer, chunk(i), 0)),
    ]
    scratch = [pltpu.VMEM((d, 2 * d_ff), BF16), pltpu.VMEM((d_ff, d), BF16)]
    return steps, specs, scratch


def _stage_ffn_weights(step, wg_ref, wu_ref, wd_ref, w_in_s, w_down_s, d_ff):
    for j in range(d_ff // MXU_WIDTH):
        @pl.when(step == j)
        def _(lo=j * MXU_WIDTH):
            w_in_s[:, lo:lo + MXU_WIDTH] = wg_ref[...].astype(BF16)
            w_in_s[:, d_ff + lo:d_ff + lo + MXU_WIDTH] = wu_ref[...].astype(BF16)
            w_down_s[lo:lo + MXU_WIDTH, :] = wd_ref[...].astype(BF16)


def _ffn_kernel(x_ref, mod_ref, gain_ref, wg_ref, wu_ref, wd_ref, fgain_ref, o_ref, w_in_s, w_down_s, *,
                sub, d_ff, bounds, final, w_steps):
    step = pl.program_id(0)
    _stage_ffn_weights(step, wg_ref, wu_ref, wd_ref, w_in_s, w_down_s, d_ff)

    @pl.when(step >= w_steps)
    def _():
        o_ref[...] = _ffn_body(x_ref[...], mod_ref, gain_ref, w_in_s, w_down_s, fgain_ref,
                               sub=sub, d_ff=d_ff, bounds=bounds, final=final)


def _ffn_body(x, mod_ref, gain_ref, w_in_ref, w_down_ref, fgain_ref, *, sub, d_ff, bounds, final):
    shift = mod_ref[0, 3 * sub + 0:3 * sub + 1, :]
    scale = mod_ref[0, 3 * sub + 1:3 * sub + 2, :]
    gate = mod_ref[0, 3 * sub + 2:3 * sub + 3, :]
    hb = _norm_mod(x, gain_ref[...], shift, scale).astype(BF16)
    acc = None
    for lo, hi in zip(bounds[:-1], bounds[1:]):
        g = _dot(hb, w_in_ref[:, lo:hi])
        u = _dot(hb, w_in_ref[:, d_ff + lo:d_ff + hi])
        act = (_silu(g) * u).astype(BF16)
        part = _dot(act, w_down_ref[lo:hi, :])
        acc = part if acc is None else acc + part
    xn = x + (0.5 * gate) * acc
    if final:
        xn = xn * lax.rsqrt(jnp.mean(xn * xn, axis=-1, keepdims=True) + EPS) * fgain_ref[...]
    return xn


def _ffn(x2d, mod3, gain, w_in_all, w_down_all, fgain, *, layer, sub, tokens_per_batch, final, tm=512):
    n, d = x2d.shape
    d_ff = w_down_all.shape[1]
    tiles_per_batch = tokens_per_batch // tm
    w_steps, w_specs, w_scratch = _ffn_weight_specs(layer, d, d_ff)

    def tile(i):
        return jnp.maximum(i - w_steps, 0)

    return pl.pallas_call(
        functools.partial(_ffn_kernel, sub=sub, d_ff=d_ff, bounds=_mxu_aligned_bounds(d_ff, 2), final=final,
                          w_steps=w_steps),
        grid=(w_steps + n // tm,),
        in_specs=[
            pl.BlockSpec((tm, d), lambda i: (tile(i), 0)),
            pl.BlockSpec((1, 3 * N_SUB, d), lambda i: (tile(i) // tiles_per_batch, 0, 0)),
            pl.BlockSpec((1, d), lambda i: (0, 0)),
            *w_specs,
            pl.BlockSpec((1, d), lambda i: (0, 0)),
        ],
        out_specs=pl.BlockSpec((tm, d), lambda i: (tile(i), 0)),
        out_shape=jax.ShapeDtypeStruct((n, d), F32),
        scratch_shapes=w_scratch,
        compiler_params=pltpu.CompilerParams(dimension_semantics=("arbitrary",), vmem_limit_bytes=52 * MIB),
        name="ffn_final" if final else "ffn",
    )(x2d, mod3, gain.reshape(1, d), w_in_all, w_in_all, w_down_all, fgain.reshape(1, d))


MIX_W_CHUNK = 512


MIX_OFF = {"q": 0, "k": 4, "g": 8, "v": 12, "sg": 16, "gq": 20, "gk": 22, "gd": 24, "gv": 26, "gsg": 30}
MIX_TILES = 34


def _inproj_kernel(x_ref, mod_ref, gain_ref, wt_ref, lbl_ref, wup_ref, bup_ref, mix_ref, r_ref, stats_ref, wt_s, *,
                   sub, layer, n_cols, w_steps):
    step = pl.program_id(0)
    for j in range(w_steps):
        @pl.when(step == j)
        def _(lo=j * MIX_W_CHUNK):
            w = wt_ref[...]
            if lo + MIX_W_CHUNK > n_cols:
                row = lo + lax.broadcasted_iota(jnp.int32, w.shape, 0)
                w = jnp.where(row < n_cols, w, 0.0)
            wt_s[lo:lo + MIX_W_CHUNK, :] = w.astype(BF16)

    @pl.when(step >= w_steps)
    def _():
        shift = mod_ref[0, 3 * sub + 0:3 * sub + 1, :]
        scale = mod_ref[0, 3 * sub + 1:3 * sub + 2, :]
        hb = _norm_mod(x_ref[...], gain_ref[...], shift, scale).astype(BF16)
        tm = hb.shape[0]
        hg_qk, hg_w = HG_HEADS * HG_DK, HG_HEADS * HG_DV
        gla_k, gla_v = GLA_HEADS * GLA_DK, GLA_HEADS * GLA_DV
        n_hg = 2 * hg_qk + 2 * hg_w
        n_rec = n_hg + 2 * gla_k + 2 * gla_v

        def put(name, val):
            lo = MIX_OFF[name] * LANES
            mix_ref[:, lo:lo + val.shape[1]] = val

        def chunk_min(g):
            tot = jnp.sum(g.reshape(tm // REC_CHUNK, REC_CHUNK, g.shape[1]), axis=1)
            return functools.reduce(jnp.minimum, [tot[:, j:j + LANES] for j in range(0, g.shape[1], LANES)])

        def mag_max(v):
            m = jnp.max(jnp.abs(v).reshape(tm // SUBLANES, SUBLANES, v.shape[1]), axis=0)
            return functools.reduce(jnp.maximum, [m[:, j:j + LANES] for j in range(0, v.shape[1], LANES)])

        ph = _dot_nt(hb, wt_s[:n_hg, :])
        lbl = lbl_ref[...]
        ex = jnp.exp(lbl - jnp.max(lbl, axis=0, keepdims=True))
        lb = jnp.sum(ex[:layer + 1], axis=0, keepdims=True) / jnp.sum(ex, axis=0, keepdims=True)
        c1 = 0.5 * (1.0 - lb)
        q = _silu_tanh(ph[:, :hg_qk])
        p = c1 * jnp.tanh(0.5 * ph[:, hg_qk:2 * hg_qk])
        g = jnp.log2((1.0 - c1) + p)
        put("q", q)
        put("k", c1 - p)
        put("g", g)
        put("v", ph[:, 2 * hg_qk:2 * hg_qk + hg_w])
        put("sg", _silu_tanh(ph[:, 2 * hg_qk + hg_w:]))

        pg = _dot_nt(hb, wt_s[n_hg:n_rec, :])
        code = _dot_nt(hb, wt_s[n_rec:n_rec + LANES, :])
        logits = _dot(code.astype(BF16), wup_ref[...].astype(BF16)) + bup_ref[...]
        gd = (jnp.minimum(logits, 0.0) - jnp.log(1.0 + jnp.exp(-jnp.abs(logits)))) * (LOG2E / GLA_NORMALIZER)
        gq = pg[:, :gla_k] * (GLA_DK ** -0.5)
        gk = pg[:, gla_k:2 * gla_k]
        put("gq", gq)
        put("gk", gk)
        put("gd", gd)
        put("gv", pg[:, 2 * gla_k:2 * gla_k + gla_v])
        put("gsg", _silu_tanh(pg[:, 2 * gla_k + gla_v:]))

        r_ref[...] = _dot_nt(hb, wt_s[n_rec + GLA_RANK:n_rec + GLA_RANK + r_ref.shape[1], :]).astype(r_ref.dtype)
        stats_ref[0, 0] = jnp.minimum(chunk_min(g), chunk_min(gd))
        stats_ref[0, 1] = functools.reduce(jnp.maximum, [mag_max(q), mag_max(gq), mag_max(gk)])


def _inproj(x2d, mod3, gain, wt_all, lb_logits, wup_pad, bup, *, layer, sub, tokens_per_batch, tm=512):
    n, d = x2d.shape
    n_cols = wt_all.shape[1]
    n_rec = 2 * HG_HEADS * HG_DK + 2 * HG_HEADS * HG_DV + 2 * GLA_HEADS * GLA_DK + 2 * GLA_HEADS * GLA_DV
    n_r = n_cols - n_rec - GLA_RANK
    w_steps = pl.cdiv(n_cols, MIX_W_CHUNK)
    assert n_rec + LANES <= w_steps * MIX_W_CHUNK and tm // REC_CHUNK == SUBLANES
    tiles_per_batch = tokens_per_batch // tm

    def tile(i):
        return jnp.maximum(i - w_steps, 0)

    return pl.pallas_call(
        functools.partial(_inproj_kernel, sub=sub, layer=layer, n_cols=n_cols, w_steps=w_steps),
        grid=(w_steps + n // tm,),
        in_specs=[
            pl.BlockSpec((tm, d), lambda i: (tile(i), 0)),
            pl.BlockSpec((1, 3 * N_SUB, d), lambda i: (tile(i) // tiles_per_batch, 0, 0)),
            pl.BlockSpec((1, d), lambda i: (0, 0)),
            pl.BlockSpec((None, MIX_W_CHUNK, d), lambda i: (layer, jnp.minimum(i, w_steps - 1), 0)),
            pl.BlockSpec(lb_logits.shape, lambda i: (0, 0)),
            pl.BlockSpec(wup_pad.shape, lambda i: (0, 0)),
            pl.BlockSpec((1, bup.shape[0]), lambda i: (0, 0)),
        ],
        out_specs=[
            pl.BlockSpec((tm, MIX_TILES * LANES), lambda i: (tile(i), 0)),
            pl.BlockSpec((tm, n_r), lambda i: (tile(i), 0)),
            pl.BlockSpec((1, 2, SUBLANES, LANES), lambda i: (tile(i), 0, 0, 0)),
        ],
        out_shape=[
            jax.ShapeDtypeStruct((n, MIX_TILES * LANES), F32),
            jax.ShapeDtypeStruct((n, n_r), BF16),
            jax.ShapeDtypeStruct((n // tm, 2, SUBLANES, LANES), F32),
        ],
        scratch_shapes=[pltpu.VMEM((w_steps * MIX_W_CHUNK, d), BF16)],
        compiler_params=pltpu.CompilerParams(dimension_semantics=("arbitrary",), vmem_limit_bytes=56 * MIB),
        name="inproj",
    )(x2d, mod3, gain.reshape(1, d), wt_all, lb_logits, wup_pad, bup.reshape(1, -1))


def _load_perm(ref, base):
    return [ref[pl.ds(base + PERM_GROUP * c + a, SUBLANES, stride=PERM_STRIDE), :]
            for c in range(N_GROUPS) for a in range(PERM_STRIDE)]


def _store_perm(ref, base, val):
    for c in range(N_GROUPS):
        for a in range(PERM_STRIDE):
            u = PERM_STRIDE * c + a
            ref[pl.ds(base + PERM_GROUP * c + a, SUBLANES, stride=PERM_STRIDE), :] = (
                val[SUBLANES * u:SUBLANES * (u + 1), :])


def _shift_down(v, d, sub_iota):
    return jnp.where(sub_iota >= d, pltpu.roll(v, d, 0), 0.0)


def _cumsum_perm(g, sub_iota):
    out = []
    carry = None
    for c in range(N_GROUPS):
        pre = [g[PERM_STRIDE * c]]
        for a in range(1, PERM_STRIDE):
            pre.append(pre[-1] + g[PERM_STRIDE * c + a])
        inc = pre[-1]
        inc = inc + _shift_down(inc, 1, sub_iota)
        inc = inc + _shift_down(inc, 2, sub_iota)
        inc = inc + _shift_down(inc, 4, sub_iota)
        exc = _shift_down(inc, 1, sub_iota)
        if carry is not None:
            exc = exc + carry
        out.extend(p + exc for p in pre)
        tot = inc[SUBLANES - 1:SUBLANES, :]
        carry = tot if carry is None else carry + tot
    return jnp.concatenate(out, axis=0), carry


def _perm_token_ids(shape, dim):
    r = lax.broadcasted_iota(jnp.int32, shape, dim)
    u = r >> 3
    return PERM_GROUP * (u >> 2) + PERM_STRIDE * (r & 7) + (u & 3)


def _silu_tanh(v):
    h = 0.5 * v
    return h * jnp.tanh(h) + h


def _finish(o, gain, act_gate):
    return o * lax.rsqrt(jnp.mean(o * o, axis=-1, keepdims=True) + EPS) * gain * act_gate


def _rec_kernel(safe_ref, q0, q1, k0, k1, g0, g1, v0, v1, sg0, sg1, gq, gk, gd, gv0, gv1, gsg0, gsg1,
                hgn_ref, glan_ref, yh_ref, yg_ref, st_s, fb_s, oi_s, *, tt, unroll):
    nc = tt // REC_CHUNK
    nb = q0.shape[0]

    @pl.when(pl.program_id(1) == 0)
    def _():
        st_s[...] = jnp.zeros_like(st_s)

    safe = safe_ref[pl.program_id(1)] != 0

    sub_iota = lax.broadcasted_iota(jnp.int32, (SUBLANES, LANES), 0)
    lane = lax.broadcasted_iota(jnp.int32, (REC_CHUNK, LANES), 1)
    tok_row = _perm_token_ids((REC_CHUNK, REC_CHUNK), 0)
    tok_col = _perm_token_ids((REC_CHUNK, REC_CHUNK), 1)
    causal = tok_col <= tok_row
    tok_of_row = _perm_token_ids((REC_CHUNK, LANES), 0)
    hgn = hgn_ref[...]
    glan = glan_ref[...]

    head_masks = (lane < GLA_DK, lane >= GLA_DK)
    groups = []
    for bi in range(nb):
        groups += [
            ((q0.at[bi], k0.at[bi], g0.at[bi]), st_s.at[bi, 0],
             [(None, v0.at[bi], sg0.at[bi], hgn, yh_ref.at[0, bi])]),
            ((q1.at[bi], k1.at[bi], g1.at[bi]), st_s.at[bi, 1],
             [(None, v1.at[bi], sg1.at[bi], hgn, yh_ref.at[1, bi])]),
            ((gq.at[bi], gk.at[bi], gd.at[bi]), st_s.at[bi, 2],
             [(head_masks[0], gv0.at[bi], gsg0.at[bi], glan, yg_ref.at[0, bi]),
              (head_masks[1], gv1.at[bi], gsg1.at[bi], glan, yg_ref.at[1, bi])]),
        ]

    def load_qkb(refs, base):
        q_ref, k_ref, g_ref = refs
        b, b_last = _cumsum_perm(_load_perm(g_ref, base), sub_iota)
        q = jnp.concatenate(_load_perm(q_ref, base), axis=0)
        k = jnp.concatenate(_load_perm(k_ref, base), axis=0)
        return q, k, b, b_last

    def masked(x, mask):
        return x if mask is None else jnp.where(mask, x, jnp.zeros_like(x))

    def fast_chunk(ci, carry):
        staged = []
        for sub in range(unroll):
            base = pl.multiple_of((ci * unroll + sub) * REC_CHUNK, REC_CHUNK)
            for refs, st_view, heads in groups:
                q, k, b, b_last = load_qkb(refs, base)
                half = 0.5 * b_last
                d = b - half
                em = jnp.exp2(half)
                qt_b = (q * jnp.exp2(d)).astype(BF16)
                kt_b = (k * jnp.exp2(-d)).astype(BF16)
                vbs = [jnp.concatenate(_load_perm(v_ref, base), axis=0).astype(BF16) for _, v_ref, _, _, _ in heads]
                q_all = jnp.concatenate([masked(qt_b, m) for m, *_ in heads], axis=0)
                k_all = jnp.concatenate([masked(kt_b, m) for m, *_ in heads], axis=0)
                v_all = jnp.concatenate(vbs, axis=0)
                state_e = st_view[...] * em
                scores = _dot_nt(q_all, kt_b)
                o_inter = _dot_nt(q_all, state_e.astype(BF16))
                st_view[...] = (state_e + _dot_tn(v_all, k_all)) * em
                for h, (_, _, gate_ref, gain, y_view) in enumerate(heads):
                    rows = slice(h * REC_CHUNK, (h + 1) * REC_CHUNK)
                    staged.append((scores[rows], o_inter[rows], vbs[h], base, gate_ref, gain, y_view))
        for scores, o_inter, vb, base, gate_ref, gain, y_view in staged:
            s = jnp.where(causal, scores, 0.0).astype(BF16)
            o = o_inter + _dot(s, vb)
            gate = jnp.concatenate(_load_perm(gate_ref, base), axis=0)
            _store_perm(y_view, base, _finish(o, gain, gate))
        return carry

    def exact_chunk(ci, carry):
        base = pl.multiple_of(ci * REC_CHUNK, REC_CHUNK)
        for refs, st_view, heads in groups:
            q_all, k, b, b_last = load_qkb(refs, base)
            state = st_view[...]
            new_state = state * jnp.exp2(b_last)
            kd = k * jnp.exp2(b_last - b)
            for qmask, v_ref, gate_ref, gain, y_view in heads:
                q = masked(q_all, qmask)
                v = jnp.concatenate(_load_perm(v_ref, base), axis=0)
                gate = jnp.concatenate(_load_perm(gate_ref, base), axis=0)
                fb_s[0] = b
                fb_s[1] = q
                fb_s[2] = k
                fb_s[3] = v

                def row(r, c2):
                    u = r >> 3
                    tr = PERM_GROUP * (u >> 2) + PERM_STRIDE * (r & 7) + (u & 3)
                    bt = fb_s[0, pl.ds(r, 1), :]
                    qt = fb_s[1, pl.ds(r, 1), :]
                    w = jnp.exp2(jnp.where(tok_of_row <= tr, bt - fb_s[0], -jnp.inf))
                    sc = jnp.sum(qt * w * fb_s[2], axis=-1, keepdims=True)
                    oi_s[pl.ds(r, 1), :] = jnp.sum(sc * fb_s[3], axis=0, keepdims=True)
                    return c2

                lax.fori_loop(0, REC_CHUNK, row, 0)
                o = _dot_nt((q * jnp.exp2(b)).astype(BF16), state.astype(BF16)) + oi_s[...]
                new_state = new_state + _dot_tn(v.astype(BF16), masked(kd, qmask).astype(BF16))
                _store_perm(y_view, base, _finish(o, gain, gate))
            st_view[...] = new_state
        return carry

    @pl.when(safe)
    def _():
        lax.fori_loop(0, nc // unroll, fast_chunk, 0)

    @pl.when(jnp.logical_not(safe))
    def _():
        lax.fori_loop(0, nc, exact_chunk, 0)


def _safe_flags(stats, *, batch, tokens_per_batch, tt):
    tiles_per_batch = stats.shape[0] // batch
    chunk_tot = jnp.min(stats[:, 0], axis=-1).reshape(batch, tokens_per_batch // tt, -1)
    tile_mag = jnp.max(stats[:, 1], axis=(-2, -1)).reshape(batch, tiles_per_batch)
    step_tot = jnp.min(chunk_tot, axis=(0, 2))
    step_mag = jnp.repeat(jnp.max(tile_mag, axis=0), (tokens_per_batch // tt) // tiles_per_batch)
    return jnp.logical_and(step_tot >= -2.0 * SAFE_EXP2, step_mag <= SAFE_MAG).astype(jnp.int32)


def _recurrence(mix, stats, hg_norm, gla_norm, *, batch, tokens_per_batch, tt=512):
    n, n_cols = mix.shape
    nt = tokens_per_batch // tt
    pairs = HG_HEADS // 2
    mix3 = mix.reshape(batch, tokens_per_batch, n_cols)
    safe = _safe_flags(stats, batch=batch, tokens_per_batch=tokens_per_batch, tt=tt)

    def head(name, e):
        return pl.BlockSpec((batch, tt, LANES), lambda p, i, flags: (0, i, MIX_OFF[name] + 2 * p + e))

    def pair(name):
        return pl.BlockSpec((batch, tt, LANES), lambda p, i, flags: (0, i, MIX_OFF[name] + p))

    in_specs = [
        head("q", 0), head("q", 1), head("k", 0), head("k", 1), head("g", 0), head("g", 1),
        head("v", 0), head("v", 1), head("sg", 0), head("sg", 1),
        pair("gq"), pair("gk"), pair("gd"),
        head("gv", 0), head("gv", 1), head("gsg", 0), head("gsg", 1),
        pl.BlockSpec((1, LANES), lambda p, i, flags: (0, 0)),
        pl.BlockSpec((1, LANES), lambda p, i, flags: (0, 0)),
    ]
    out_spec = pl.BlockSpec((2, batch, tt, LANES), lambda p, i, flags: (p, 0, i, 0))
    y_hg, y_gla = pl.pallas_call(
        functools.partial(_rec_kernel, tt=tt, unroll=REC_UNROLL),
        grid_spec=pltpu.PrefetchScalarGridSpec(
            num_scalar_prefetch=1,
            grid=(pairs, nt),
            in_specs=in_specs,
            out_specs=[out_spec, out_spec],
            scratch_shapes=[
                pltpu.VMEM((batch, 3, LANES, LANES), F32),
                pltpu.VMEM((4, REC_CHUNK, LANES), F32),
                pltpu.VMEM((REC_CHUNK, LANES), F32),
            ],
        ),
        out_shape=[jax.ShapeDtypeStruct((HG_HEADS, batch, tokens_per_batch, HG_DV), F32),
                   jax.ShapeDtypeStruct((GLA_HEADS, batch, tokens_per_batch, GLA_DV), F32)],
        compiler_params=pltpu.CompilerParams(
            dimension_semantics=("arbitrary", "arbitrary"), vmem_limit_bytes=48 * MIB),
        name="recurrence",
    )(safe, *([mix3] * 17), hg_norm.reshape(1, HG_DV), gla_norm.reshape(1, GLA_DV))
    return y_hg.reshape(HG_HEADS, n, HG_DV), y_gla.reshape(GLA_HEADS, n, GLA_DV)


def _merge_ffn_kernel(x_ref, mod_ref, yh_ref, yg_ref, rh_ref, rg_ref, wuh_ref, wug_ref, wo_ref,
                      gain_ref, wg_ref, wu_ref, wd_ref, fgain_ref, o_ref, w_in_s, w_down_s, *,
                      sub_mix, sub_ffn, d_ff, bounds, final, w_steps):
    step = pl.program_id(0)
    _stage_ffn_weights(step, wg_ref, wu_ref, wd_ref, w_in_s, w_down_s, d_ff)

    @pl.when(step >= w_steps)
    def _():
        gate = mod_ref[0, 3 * sub_mix + 2:3 * sub_mix + 3, :]

        def up(y_ref, w_ref):
            y = jnp.concatenate([y_ref[h].astype(BF16) for h in range(y_ref.shape[0])], axis=-1)
            return _dot(y, w_ref[...])

        merged = (jax.nn.sigmoid(rh_ref[...].astype(F32)) * up(yh_ref, wuh_ref)
                  + jax.nn.sigmoid(rg_ref[...].astype(F32)) * up(yg_ref, wug_ref))
        x = x_ref[...] + gate * _dot(merged.astype(BF16), wo_ref[...])
        o_ref[...] = _ffn_body(x, mod_ref, gain_ref, w_in_s, w_down_s, fgain_ref,
                               sub=sub_ffn, d_ff=d_ff, bounds=bounds, final=final)


def _merge_ffn(x2d, mod3, y_hg, y_gla, r, w_up_hg, w_up_gla, w_out, gain, w_in_all, w_down_all, fgain, *,
               layer, sub_mix, sub_ffn, tokens_per_batch, final, tm=512):
    n, d = x2d.shape
    d_ff = w_down_all.shape[1]
    tiles_per_batch = tokens_per_batch // tm
    w_steps, w_specs, w_scratch = _ffn_weight_specs(layer, d, d_ff)
    const = dict(pipeline_mode=pl.Buffered(1))

    def tile(i):
        return jnp.maximum(i - w_steps, 0)

    return pl.pallas_call(
        functools.partial(_merge_ffn_kernel, sub_mix=sub_mix, sub_ffn=sub_ffn, d_ff=d_ff,
                          bounds=_mxu_aligned_bounds(d_ff, 2), final=final, w_steps=w_steps),
        grid=(w_steps + n // tm,),
        in_specs=[
            pl.BlockSpec((tm, d), lambda i: (tile(i), 0)),
            pl.BlockSpec((1, 3 * N_SUB, d), lambda i: (tile(i) // tiles_per_batch, 0, 0)),
            pl.BlockSpec((HG_HEADS, tm, HG_DV), lambda i: (0, tile(i), 0)),
            pl.BlockSpec((GLA_HEADS, tm, GLA_DV), lambda i: (0, tile(i), 0)),
            pl.BlockSpec((tm, d), lambda i: (tile(i), 0)),
            pl.BlockSpec((tm, d), lambda i: (tile(i), 1)),
            pl.BlockSpec(w_up_hg.shape, lambda i: (0, 0), **const),
            pl.BlockSpec(w_up_gla.shape, lambda i: (0, 0), **const),
            pl.BlockSpec(w_out.shape, lambda i: (0, 0), **const),
            pl.BlockSpec((1, d), lambda i: (0, 0)),
            *w_specs,
            pl.BlockSpec((1, d), lambda i: (0, 0)),
        ],
        out_specs=pl.BlockSpec((tm, d), lambda i: (tile(i), 0)),
        out_shape=jax.ShapeDtypeStruct((n, d), F32),
        scratch_shapes=w_scratch,
        compiler_params=pltpu.CompilerParams(dimension_semantics=("arbitrary",), vmem_limit_bytes=58 * MIB),
        name="merge_ffn",
    )(x2d, mod3, y_hg, y_gla, r, r, w_up_hg, w_up_gla, w_out, gain.reshape(1, d), w_in_all, w_in_all,
      w_down_all, fgain.reshape(1, d))


def kernel(x, c, w_ada, b_ada, norm_gains, ffn1_w_in, ffn1_w_down, w_in_mix, w_gk_up, b_gk_up, lb_logits,
           hg_norm, gla_norm, w_up_hg, w_up_gla, w_out, ffn2_w_in, ffn2_w_down, final_norm):
    batch, seq, d = x.shape
    depth = w_ada.shape[0]

    x2d = x.reshape(batch * seq, d)
    c_pad = jnp.pad(c, ((0, SUBLANES - batch % SUBLANES if batch % SUBLANES else 0), (0, 0)))
    for l in range(depth):
        mod = _ada(c_pad, w_ada[l], b_ada[l])[:batch].reshape(batch, 3 * N_SUB, d)

        x2d = _ffn(x2d, mod, norm_gains[l, 0], ffn1_w_in, ffn1_w_down, final_norm, layer=l, sub=0,
                   tokens_per_batch=seq, final=False)

        wup_pad = jnp.pad(w_gk_up[l], ((0, LANES - GLA_RANK), (0, 0)))
        mix, r, stats = _inproj(x2d, mod, norm_gains[l, 1], jnp.swapaxes(w_in_mix, 1, 2), lb_logits, wup_pad,
                                b_gk_up[l], layer=l, sub=1, tokens_per_batch=seq)

        y_hg, y_gla = _recurrence(mix, stats, hg_norm[l], gla_norm[l], batch=batch, tokens_per_batch=seq)

        x2d = _merge_ffn(x2d, mod, y_hg, y_gla, r, w_up_hg[l].astype(BF16), w_up_gla[l].astype(BF16),
                         w_out[l].astype(BF16), norm_gains[l, 2], ffn2_w_in, ffn2_w_down, final_norm, layer=l,
                         sub_mix=1, sub_ffn=2, tokens_per_batch=seq, final=l == depth - 1)
    if depth == 0:
        raise ValueError("depth must be >= 1")
    return x2d.reshape(batch, seq, d)
```

```python
import functools

import jax
import jax.numpy as jnp
from jax import lax
from jax.experimental import pallas as pl
from jax.experimental.pallas import tpu as pltpu

F32 = jnp.float32
BF16 = jnp.bfloat16

EPS = 1e-6
N_SUB = 3
HG_HEADS = 4
HG_DK = 128
HG_DV = 128
GLA_HEADS = 4
GLA_DK = 64
GLA_DV = 128
GLA_RANK = 16
GLA_NORMALIZER = 16.0

LANES = 128
SUBLANES = 8
MXU_WIDTH = 256
MIB = 1024 * 1024

REC_CHUNK = 64
PERM_STRIDE = 4
PERM_GROUP = PERM_STRIDE * SUBLANES
N_GROUPS = REC_CHUNK // PERM_GROUP
REC_UNROLL = 2
LOG2E = 1.4426950408889634
SAFE_EXP2 = 100.0
SAFE_MAG = 1e7


def _dot(a, b):
    return jnp.dot(a, b, preferred_element_type=F32)


def _dot_nt(a, b):
    return lax.dot_general(a, b, (((1,), (1,)), ((), ())), preferred_element_type=F32)


def _dot_tn(a, b):
    return lax.dot_general(a, b, (((0,), (0,)), ((), ())), preferred_element_type=F32)


def _silu(v):
    return v * jax.nn.sigmoid(v)


def _norm_mod(x, gain, shift, scale):
    y = x * lax.rsqrt(jnp.mean(x * x, axis=-1, keepdims=True) + EPS) * gain
    return y * (1.0 + scale) + shift


def _ada_kernel(c_ref, w_ref, b_ref, o_ref):
    cond = _silu(c_ref[...]).astype(BF16)
    o_ref[...] = _dot(cond, w_ref[...].astype(BF16)) + b_ref[...]


def _ada(c_pad, w, b):
    rows, d = c_pad.shape
    n = w.shape[1]
    bn = n // 8
    return pl.pallas_call(
        _ada_kernel,
        grid=(n // bn,),
        in_specs=[
            pl.BlockSpec((rows, d), lambda j: (0, 0)),
            pl.BlockSpec((d, bn), lambda j: (0, j)),
            pl.BlockSpec((1, bn), lambda j: (0, j)),
        ],
        out_specs=pl.BlockSpec((rows, bn), lambda j: (0, j)),
        out_shape=jax.ShapeDtypeStruct((rows, n), F32),
        name="adaln",
    )(c_pad, w, b.reshape(1, n))


def _mxu_aligned_bounds(width, parts):
    if width % MXU_WIDTH:
        return (0, width)
    tiles = width // MXU_WIDTH
    return tuple(MXU_WIDTH * ((tiles * p + parts - 1) // parts) for p in range(parts)) + (width,)


def _ffn_weight_specs(layer, d, d_ff):
    assert d_ff % MXU_WIDTH == 0
    steps = d_ff // MXU_WIDTH

    def chunk(i):
        return jnp.minimum(i, steps - 1)

    specs = [
        pl.BlockSpec((None, d, MXU_WIDTH), lambda i: (layer, 0, chunk(i))),
        pl.BlockSpec((None, d, MXU_WIDTH), lambda i: (layer, 0, steps + chunk(i))),
        pl.BlockSpec((None, MXU_WIDTH, d), lambda i: (layer, chunk(i), 0)),
    ]
    scratch = [pltpu.VMEM((d, 2 * d_ff), BF16), pltpu.VMEM((d_ff, d), BF16)]
    return steps, specs, scratch


def _stage_ffn_weights(step, wg_ref, wu_ref, wd_ref, w_in_s, w_down_s, d_ff):
    for j in range(d_ff // MXU_WIDTH):
        @pl.when(step == j)
        def _(lo=j * MXU_WIDTH):
            w_in_s[:, lo:lo + MXU_WIDTH] = wg_ref[...].astype(BF16)
            w_in_s[:, d_ff + lo:d_ff + lo + MXU_WIDTH] = wu_ref[...].astype(BF16)
            w_down_s[lo:lo + MXU_WIDTH, :] = wd_ref[...].astype(BF16)


def _ffn_kernel(x_ref, mod_ref, gain_ref, wg_ref, wu_ref, wd_ref, fgain_ref, o_ref, w_in_s, w_down_s, *,
                sub, d_ff, bounds, final, w_steps):
    step = pl.program_id(0)
    _stage_ffn_weights(step, wg_ref, wu_ref, wd_ref, w_in_s, w_down_s, d_ff)

    @pl.when(step >= w_steps)
    def _():
        o_ref[...] = _ffn_body(x_ref[...], mod_ref, gain_ref, w_in_s, w_down_s, fgain_ref,
                               sub=sub, d_ff=d_ff, bounds=bounds, final=final)


def _ffn_body(x, mod_ref, gain_ref, w_in_ref, w_down_ref, fgain_ref, *, sub, d_ff, bounds, final):
    shift = mod_ref[0, 3 * sub + 0:3 * sub + 1, :]
    scale = mod_ref[0, 3 * sub + 1:3 * sub + 2, :]
    gate = mod_ref[0, 3 * sub + 2:3 * sub + 3, :]
    hb = _norm_mod(x, gain_ref[...], shift, scale).astype(BF16)
    acc = None
    for lo, hi in zip(bounds[:-1], bounds[1:]):
        g = _dot(hb, w_in_ref[:, lo:hi])
        u = _dot(hb, w_in_ref[:, d_ff + lo:d_ff + hi])
        act = (_silu(g) * u).astype(BF16)
        part = _dot(act, w_down_ref[lo:hi, :])
        acc = part if acc is None else acc + part
    xn = x + (0.5 * gate) * acc
    if final:
        xn = xn * lax.rsqrt(jnp.mean(xn * xn, axis=-1, keepdims=True) + EPS) * fgain_ref[...]
    return xn


def _ffn(x2d, mod3, gain, w_in_all, w_down_all, fgain, *, layer, sub, tokens_per_batch, final, tm=512):
    n, d = x2d.shape
    d_ff = w_down_all.shape[1]
    tiles_per_batch = tokens_per_batch // tm
    w_steps, w_specs, w_scratch = _ffn_weight_specs(layer, d, d_ff)

    def tile(i):
        return jnp.maximum(i - w_steps, 0)

    return pl.pallas_call(
        functools.partial(_ffn_kernel, sub=sub, d_ff=d_ff, bounds=_mxu_aligned_bounds(d_ff, 2), final=final,
                          w_steps=w_steps),
        grid=(w_steps + n // tm,),
        in_specs=[
            pl.BlockSpec((tm, d), lambda i: (tile(i), 0)),
            pl.BlockSpec((1, 3 * N_SUB, d), lambda i: (tile(i) // tiles_per_batch, 0, 0)),
            pl.BlockSpec((1, d), lambda i: (0, 0)),
            *w_specs,
            pl.BlockSpec((1, d), lambda i: (0, 0)),
        ],
        out_specs=pl.BlockSpec((tm, d), lambda i: (tile(i), 0)),
        out_shape=jax.ShapeDtypeStruct((n, d), F32),
        scratch_shapes=w_scratch,
        compiler_params=pltpu.CompilerParams(dimension_semantics=("arbitrary",), vmem_limit_bytes=52 * MIB),
        name="ffn_final" if final else "ffn",
    )(x2d, mod3, gain.reshape(1, d), w_in_all, w_in_all, w_down_all, fgain.reshape(1, d))


MIX_W_CHUNK = 512


DEC_OFF = {"g": 0, "gd": 4}
DEC_TILES = 6
ACT_OFF = {"q": 0, "k": 4, "v": 8, "sg": 12, "gq": 16, "gk": 18, "gv": 20, "gsg": 24}
ACT_TILES = 28


def _inproj_kernel(x_ref, mod_ref, gain_ref, wt_ref, lbl_ref, wup_ref, bup_ref, dec_ref, act_ref, r_ref, stats_ref,
                   wt_s, *, sub, layer, n_cols, w_steps):
    step = pl.program_id(0)
    for j in range(w_steps):
        @pl.when(step == j)
        def _(lo=j * MIX_W_CHUNK):
            w = wt_ref[...]
            if lo + MIX_W_CHUNK > n_cols:
                row = lo + lax.broadcasted_iota(jnp.int32, w.shape, 0)
                w = jnp.where(row < n_cols, w, 0.0)
            wt_s[lo:lo + MIX_W_CHUNK, :] = w.astype(BF16)

    @pl.when(step >= w_steps)
    def _():
        shift = mod_ref[0, 3 * sub + 0:3 * sub + 1, :]
        scale = mod_ref[0, 3 * sub + 1:3 * sub + 2, :]
        hb = _norm_mod(x_ref[...], gain_ref[...], shift, scale).astype(BF16)
        tm = hb.shape[0]
        hg_qk, hg_w = HG_HEADS * HG_DK, HG_HEADS * HG_DV
        gla_k, gla_v = GLA_HEADS * GLA_DK, GLA_HEADS * GLA_DV
        n_hg = 2 * hg_qk + 2 * hg_w
        n_rec = n_hg + 2 * gla_k + 2 * gla_v

        def put(name, val):
            if name in DEC_OFF:
                lo = DEC_OFF[name] * LANES
                dec_ref[:, lo:lo + val.shape[1]] = val
            else:
                lo = ACT_OFF[name] * LANES
                act_ref[:, lo:lo + val.shape[1]] = val.astype(act_ref.dtype)

        def chunk_min(g):
            tot = jnp.sum(g.reshape(tm // REC_CHUNK, REC_CHUNK, g.shape[1]), axis=1)
            return functools.reduce(jnp.minimum, [tot[:, j:j + LANES] for j in range(0, g.shape[1], LANES)])

        def mag_max(v):
            m = jnp.max(jnp.abs(v).reshape(tm // SUBLANES, SUBLANES, v.shape[1]), axis=0)
            return functools.reduce(jnp.maximum, [m[:, j:j + LANES] for j in range(0, v.shape[1], LANES)])

        ph = _dot_nt(hb, wt_s[:n_hg, :])
        lbl = lbl_ref[...]
        ex = jnp.exp(lbl - jnp.max(lbl, axis=0, keepdims=True))
        lb = jnp.sum(ex[:layer + 1], axis=0, keepdims=True) / jnp.sum(ex, axis=0, keepdims=True)
        c1 = 0.5 * (1.0 - lb)
        q = _silu_tanh(ph[:, :hg_qk])
        p = c1 * jnp.tanh(0.5 * ph[:, hg_qk:2 * hg_qk])
        g = jnp.log2((1.0 - c1) + p)
        put("q", q)
        put("k", c1 - p)
        put("g", g)
        put("v", ph[:, 2 * hg_qk:2 * hg_qk + hg_w])
        put("sg", _silu_tanh(ph[:, 2 * hg_qk + hg_w:]))

        pg = _dot_nt(hb, wt_s[n_hg:n_rec, :])
        code = _dot_nt(hb, wt_s[n_rec:n_rec + LANES, :])
        logits = _dot(code.astype(BF16), wup_ref[...].astype(BF16)) + bup_ref[...]
        gd = (jnp.minimum(logits, 0.0) - jnp.log(1.0 + jnp.exp(-jnp.abs(logits)))) * (LOG2E / GLA_NORMALIZER)
        gq = pg[:, :gla_k] * (GLA_DK ** -0.5)
        gk = pg[:, gla_k:2 * gla_k]
        put("gq", gq)
        put("gk", gk)
        put("gd", gd)
        put("gv", pg[:, 2 * gla_k:2 * gla_k + gla_v])
        put("gsg", _silu_tanh(pg[:, 2 * gla_k + gla_v:]))

        r_ref[...] = _dot_nt(hb, wt_s[n_rec + GLA_RANK:n_rec + GLA_RANK + r_ref.shape[1], :]).astype(r_ref.dtype)
        stats_ref[0, 0] = jnp.minimum(chunk_min(g), chunk_min(gd))
        stats_ref[0, 1] = functools.reduce(jnp.maximum, [mag_max(q), mag_max(gq), mag_max(gk)])


def _inproj(x2d, mod3, gain, wt_all, lb_logits, wup_pad, bup, *, layer, sub, tokens_per_batch, tm=512):
    n, d = x2d.shape
    n_cols = wt_all.shape[1]
    n_rec = 2 * HG_HEADS * HG_DK + 2 * HG_HEADS * HG_DV + 2 * GLA_HEADS * GLA_DK + 2 * GLA_HEADS * GLA_DV
    n_r = n_cols - n_rec - GLA_RANK
    w_steps = pl.cdiv(n_cols, MIX_W_CHUNK)
    assert n_rec + LANES <= w_steps * MIX_W_CHUNK and tm // REC_CHUNK == SUBLANES
    tiles_per_batch = tokens_per_batch // tm

    def tile(i):
        return jnp.maximum(i - w_steps, 0)

    return pl.pallas_call(
        functools.partial(_inproj_kernel, sub=sub, layer=layer, n_cols=n_cols, w_steps=w_steps),
        grid=(w_steps + n // tm,),
        in_specs=[
            pl.BlockSpec((tm, d), lambda i: (tile(i), 0)),
            pl.BlockSpec((1, 3 * N_SUB, d), lambda i: (tile(i) // tiles_per_batch, 0, 0)),
            pl.BlockSpec((1, d), lambda i: (0, 0)),
            pl.BlockSpec((None, MIX_W_CHUNK, d), lambda i: (layer, jnp.minimum(i, w_steps - 1), 0)),
            pl.BlockSpec(lb_logits.shape, lambda i: (0, 0)),
            pl.BlockSpec(wup_pad.shape, lambda i: (0, 0)),
            pl.BlockSpec((1, bup.shape[0]), lambda i: (0, 0)),
        ],
        out_specs=[
            pl.BlockSpec((tm, DEC_TILES * LANES), lambda i: (tile(i), 0)),
            pl.BlockSpec((tm, ACT_TILES * LANES), lambda i: (tile(i), 0)),
            pl.BlockSpec((tm, n_r), lambda i: (tile(i), 0)),
            pl.BlockSpec((1, 2, SUBLANES, LANES), lambda i: (tile(i), 0, 0, 0)),
        ],
        out_shape=[
            jax.ShapeDtypeStruct((n, DEC_TILES * LANES), F32),
            jax.ShapeDtypeStruct((n, ACT_TILES * LANES), BF16),
            jax.ShapeDtypeStruct((n, n_r), BF16),
            jax.ShapeDtypeStruct((n // tm, 2, SUBLANES, LANES), F32),
        ],
        scratch_shapes=[pltpu.VMEM((w_steps * MIX_W_CHUNK, d), BF16)],
        compiler_params=pltpu.CompilerParams(dimension_semantics=("arbitrary",), vmem_limit_bytes=56 * MIB),
        name="inproj",
    )(x2d, mod3, gain.reshape(1, d), wt_all, lb_logits, wup_pad, bup.reshape(1, -1))


def _load_perm(ref, base):
    return [ref[pl.ds(base + PERM_GROUP * c + a, SUBLANES, stride=PERM_STRIDE), :]
            for c in range(N_GROUPS) for a in range(PERM_STRIDE)]


def _store_perm(ref, base, val):
    for c in range(N_GROUPS):
        for a in range(PERM_STRIDE):
            u = PERM_STRIDE * c + a
            ref[pl.ds(base + PERM_GROUP * c + a, SUBLANES, stride=PERM_STRIDE), :] = (
                val[SUBLANES * u:SUBLANES * (u + 1), :])


def _shift_down(v, d, sub_iota):
    return jnp.where(sub_iota >= d, pltpu.roll(v, d, 0), 0.0)


def _cumsum_perm(g, sub_iota):
    out = []
    carry = None
    for c in range(N_GROUPS):
        pre = [g[PERM_STRIDE * c]]
        for a in range(1, PERM_STRIDE):
            pre.append(pre[-1] + g[PERM_STRIDE * c + a])
        inc = pre[-1]
        inc = inc + _shift_down(inc, 1, sub_iota)
        inc = inc + _shift_down(inc, 2, sub_iota)
        inc = inc + _shift_down(inc, 4, sub_iota)
        exc = _shift_down(inc, 1, sub_iota)
        if carry is not None:
            exc = exc + carry
        out.extend(p + exc for p in pre)
        tot = inc[SUBLANES - 1:SUBLANES, :]
        carry = tot if carry is None else carry + tot
    return jnp.concatenate(out, axis=0), carry


def _perm_token_ids(shape, dim):
    r = lax.broadcasted_iota(jnp.int32, shape, dim)
    u = r >> 3
    return PERM_GROUP * (u >> 2) + PERM_STRIDE * (r & 7) + (u & 3)


def _silu_tanh(v):
    h = 0.5 * v
    return h * jnp.tanh(h) + h


def _finish(o, gain, act_gate):
    return o * lax.rsqrt(jnp.mean(o * o, axis=-1, keepdims=True) + EPS) * gain * act_gate


def _rec_kernel(safe_ref, q0, q1, k0, k1, g0, g1, v0, v1, sg0, sg1, gq, gk, gd, gv0, gv1, gsg0, gsg1,
                hgn_ref, glan_ref, yh_ref, yg_ref, st_s, e_s, fb_s, oi_s, *, tt, unroll):
    nc = tt // REC_CHUNK
    nb = q0.shape[0]

    @pl.when(pl.program_id(1) == 0)
    def _():
        st_s[...] = jnp.zeros_like(st_s)

    safe = safe_ref[pl.program_id(1)] != 0

    sub_iota = lax.broadcasted_iota(jnp.int32, (SUBLANES, LANES), 0)
    lane = lax.broadcasted_iota(jnp.int32, (REC_CHUNK, LANES), 1)
    causal = (lax.broadcasted_iota(jnp.int32, (REC_CHUNK, REC_CHUNK), 1)
              <= lax.broadcasted_iota(jnp.int32, (REC_CHUNK, REC_CHUNK), 0))
    tok_of_row = lax.broadcasted_iota(jnp.int32, (REC_CHUNK, LANES), 0)
    hgn = hgn_ref[...]
    glan = glan_ref[...]

    head_masks = (lane < GLA_DK, lane >= GLA_DK)
    groups = []
    for bi in range(nb):
        groups += [
            ((q0.at[bi], k0.at[bi], g0.at[bi]), st_s.at[bi, 0],
             [(None, v0.at[bi], sg0.at[bi], hgn, yh_ref.at[0, bi])]),
            ((q1.at[bi], k1.at[bi], g1.at[bi]), st_s.at[bi, 1],
             [(None, v1.at[bi], sg1.at[bi], hgn, yh_ref.at[1, bi])]),
            ((gq.at[bi], gk.at[bi], gd.at[bi]), st_s.at[bi, 2],
             [(head_masks[0], gv0.at[bi], gsg0.at[bi], glan, yg_ref.at[0, bi]),
              (head_masks[1], gv1.at[bi], gsg1.at[bi], glan, yg_ref.at[1, bi])]),
        ]

    def natural(ref, base):
        return ref[pl.ds(base, REC_CHUNK), :]

    def decay_logs(g_ref, base):
        return _cumsum_perm(_load_perm(g_ref, base), sub_iota)

    def to_natural(slot, val):
        _store_perm(e_s.at[slot], 0, val)
        return e_s[slot]

    def masked(x, mask):
        return x if mask is None else jnp.where(mask, x, jnp.zeros_like(x))

    def fast_chunk(ci, carry):
        staged = []
        for sub in range(unroll):
            base = pl.multiple_of((ci * unroll + sub) * REC_CHUNK, REC_CHUNK)
            for gi, ((q_ref, k_ref, g_ref), st_view, heads) in enumerate(groups):
                b, b_last = decay_logs(g_ref, base)
                half = 0.5 * b_last
                d = b - half
                em = jnp.exp2(half)
                slot = 2 * (sub * len(groups) + gi)
                qt_b = (natural(q_ref, base).astype(F32) * to_natural(slot, jnp.exp2(d))).astype(BF16)
                kt_b = (natural(k_ref, base).astype(F32) * to_natural(slot + 1, jnp.exp2(-d))).astype(BF16)
                vbs = [natural(v_ref, base) for _, v_ref, _, _, _ in heads]
                q_all = jnp.concatenate([masked(qt_b, m) for m, *_ in heads], axis=0)
                k_all = jnp.concatenate([masked(kt_b, m) for m, *_ in heads], axis=0)
                v_all = jnp.concatenate(vbs, axis=0)
                state_e = st_view[...] * em
                scores = _dot_nt(q_all, kt_b)
                o_inter = _dot_nt(q_all, state_e.astype(BF16))
                st_view[...] = (state_e + _dot_tn(v_all, k_all)) * em
                for h, (_, _, gate_ref, gain, y_view) in enumerate(heads):
                    rows = slice(h * REC_CHUNK, (h + 1) * REC_CHUNK)
                    staged.append((scores[rows], o_inter[rows], vbs[h], base, gate_ref, gain, y_view))
        for scores, o_inter, vb, base, gate_ref, gain, y_view in staged:
            s = jnp.where(causal, scores, 0.0).astype(BF16)
            o = o_inter + _dot(s, vb)
            gate = natural(gate_ref, base).astype(F32)
            y_view[pl.ds(base, REC_CHUNK), :] = _finish(o, gain, gate).astype(y_view.dtype)
        return carry

    def exact_chunk(ci, carry):
        base = pl.multiple_of(ci * REC_CHUNK, REC_CHUNK)
        for (q_ref, k_ref, g_ref), st_view, heads in groups:
            b_perm, b_last = decay_logs(g_ref, base)
            b = to_natural(0, b_perm)
            q_all = natural(q_ref, base).astype(F32)
            k = natural(k_ref, base).astype(F32)
            state = st_view[...]
            new_state = state * jnp.exp2(b_last)
            kd = k * jnp.exp2(b_last - b)
            for qmask, v_ref, gate_ref, gain, y_view in heads:
                q = masked(q_all, qmask)
                vb = natural(v_ref, base)
                gate = natural(gate_ref, base).astype(F32)
                fb_s[0] = b
                fb_s[1] = q
                fb_s[2] = k
                fb_s[3] = vb.astype(F32)

                def row(r, c2):
                    bt = fb_s[0, pl.ds(r, 1), :]
                    qt = fb_s[1, pl.ds(r, 1), :]
                    w = jnp.exp2(jnp.where(tok_of_row <= r, bt - fb_s[0], -jnp.inf))
                    sc = jnp.sum(qt * w * fb_s[2], axis=-1, keepdims=True)
                    oi_s[pl.ds(r, 1), :] = jnp.sum(sc * fb_s[3], axis=0, keepdims=True)
                    return c2

                lax.fori_loop(0, REC_CHUNK, row, 0)
                o = _dot_nt((q * jnp.exp2(b)).astype(BF16), state.astype(BF16)) + oi_s[...]
                new_state = new_state + _dot_tn(vb, masked(kd, qmask).astype(BF16))
                y_view[pl.ds(base, REC_CHUNK), :] = _finish(o, gain, gate).astype(y_view.dtype)
            st_view[...] = new_state
        return carry

    @pl.when(safe)
    def _():
        lax.fori_loop(0, nc // unroll, fast_chunk, 0)

    @pl.when(jnp.logical_not(safe))
    def _():
        lax.fori_loop(0, nc, exact_chunk, 0)


def _safe_flags(stats, *, batch, tokens_per_batch, tt):
    tiles_per_batch = stats.shape[0] // batch
    chunk_tot = jnp.min(stats[:, 0], axis=-1).reshape(batch, tokens_per_batch // tt, -1)
    tile_mag = jnp.max(stats[:, 1], axis=(-2, -1)).reshape(batch, tiles_per_batch)
    step_tot = jnp.min(chunk_tot, axis=(0, 2))
    step_mag = jnp.repeat(jnp.max(tile_mag, axis=0), (tokens_per_batch // tt) // tiles_per_batch)
    return jnp.logical_and(step_tot >= -2.0 * SAFE_EXP2, step_mag <= SAFE_MAG).astype(jnp.int32)


def _recurrence(dec, act, stats, hg_norm, gla_norm, *, batch, tokens_per_batch, tt=512):
    n = dec.shape[0]
    nt = tokens_per_batch // tt
    pairs = HG_HEADS // 2
    dec3 = dec.reshape(batch, tokens_per_batch, dec.shape[1])
    act3 = act.reshape(batch, tokens_per_batch, act.shape[1])
    safe = _safe_flags(stats, batch=batch, tokens_per_batch=tokens_per_batch, tt=tt)

    def tile_of(name, p, e):
        off = DEC_OFF[name] if name in DEC_OFF else ACT_OFF[name]
        return off + (p if e is None else 2 * p + e)

    def spec(name, e=None):
        return pl.BlockSpec((batch, tt, LANES), lambda p, i, flags: (0, i, tile_of(name, p, e)))

    names = [("q", 0), ("q", 1), ("k", 0), ("k", 1), ("g", 0), ("g", 1), ("v", 0), ("v", 1), ("sg", 0), ("sg", 1),
             ("gq", None), ("gk", None), ("gd", None), ("gv", 0), ("gv", 1), ("gsg", 0), ("gsg", 1)]
    in_specs = [spec(name, e) for name, e in names] + [
        pl.BlockSpec((1, LANES), lambda p, i, flags: (0, 0)),
        pl.BlockSpec((1, LANES), lambda p, i, flags: (0, 0)),
    ]
    operands = [dec3 if name in DEC_OFF else act3 for name, _ in names]
    n_groups = 3 * batch
    out_spec = pl.BlockSpec((2, batch, tt, LANES), lambda p, i, flags: (p, 0, i, 0))
    y_hg, y_gla = pl.pallas_call(
        functools.partial(_rec_kernel, tt=tt, unroll=REC_UNROLL),
        grid_spec=pltpu.PrefetchScalarGridSpec(
            num_scalar_prefetch=1,
            grid=(pairs, nt),
            in_specs=in_specs,
            out_specs=[out_spec, out_spec],
            scratch_shapes=[
                pltpu.VMEM((batch, 3, LANES, LANES), F32),
                pltpu.VMEM((2 * REC_UNROLL * n_groups, REC_CHUNK, LANES), F32),
                pltpu.VMEM((4, REC_CHUNK, LANES), F32),
                pltpu.VMEM((REC_CHUNK, LANES), F32),
            ],
        ),
        out_shape=[jax.ShapeDtypeStruct((HG_HEADS, batch, tokens_per_batch, HG_DV), BF16),
                   jax.ShapeDtypeStruct((GLA_HEADS, batch, tokens_per_batch, GLA_DV), BF16)],
        compiler_params=pltpu.CompilerParams(
            dimension_semantics=("arbitrary", "arbitrary"), vmem_limit_bytes=48 * MIB),
        name="recurrence",
    )(safe, *operands, hg_norm.reshape(1, HG_DV), gla_norm.reshape(1, GLA_DV))
    return y_hg.reshape(HG_HEADS, n, HG_DV), y_gla.reshape(GLA_HEADS, n, GLA_DV)


def _merge_ffn_kernel(x_ref, mod_ref, yh_ref, yg_ref, rh_ref, rg_ref, wuh_ref, wug_ref, wo_ref,
                      gain_ref, wg_ref, wu_ref, wd_ref, fgain_ref, o_ref, w_in_s, w_down_s, *,
                      sub_mix, sub_ffn, d_ff, bounds, final, w_steps):
    step = pl.program_id(0)
    _stage_ffn_weights(step, wg_ref, wu_ref, wd_ref, w_in_s, w_down_s, d_ff)

    @pl.when(step >= w_steps)
    def _():
        gate = mod_ref[0, 3 * sub_mix + 2:3 * sub_mix + 3, :]

        def up(y_ref, w_ref):
            y = jnp.concatenate([y_ref[h].astype(BF16) for h in range(y_ref.shape[0])], axis=-1)
            return _dot(y, w_ref[...])

        merged = (jax.nn.sigmoid(rh_ref[...].astype(F32)) * up(yh_ref, wuh_ref)
                  + jax.nn.sigmoid(rg_ref[...].astype(F32)) * up(yg_ref, wug_ref))
        x = x_ref[...] + gate * _dot(merged.astype(BF16), wo_ref[...])
        o_ref[...] = _ffn_body(x, mod_ref, gain_ref, w_in_s, w_down_s, fgain_ref,
                               sub=sub_ffn, d_ff=d_ff, bounds=bounds, final=final)


def _merge_ffn(x2d, mod3, y_hg, y_gla, r, w_up_hg, w_up_gla, w_out, gain, w_in_all, w_down_all, fgain, *,
               layer, sub_mix, sub_ffn, tokens_per_batch, final, tm=512):
    n, d = x2d.shape
    d_ff = w_down_all.shape[1]
    tiles_per_batch = tokens_per_batch // tm
    w_steps, w_specs, w_scratch = _ffn_weight_specs(layer, d, d_ff)
    const = dict(pipeline_mode=pl.Buffered(1))

    def tile(i):
        return jnp.maximum(i - w_steps, 0)

    return pl.pallas_call(
        functools.partial(_merge_ffn_kernel, sub_mix=sub_mix, sub_ffn=sub_ffn, d_ff=d_ff,
                          bounds=_mxu_aligned_bounds(d_ff, 2), final=final, w_steps=w_steps),
        grid=(w_steps + n // tm,),
        in_specs=[
            pl.BlockSpec((tm, d), lambda i: (tile(i), 0)),
            pl.BlockSpec((1, 3 * N_SUB, d), lambda i: (tile(i) // tiles_per_batch, 0, 0)),
            pl.BlockSpec((HG_HEADS, tm, HG_DV), lambda i: (0, tile(i), 0)),
            pl.BlockSpec((GLA_HEADS, tm, GLA_DV), lambda i: (0, tile(i), 0)),
            pl.BlockSpec((tm, d), lambda i: (tile(i), 0)),
            pl.BlockSpec((tm, d), lambda i: (tile(i), 1)),
            pl.BlockSpec(w_up_hg.shape, lambda i: (0, 0), **const),
            pl.BlockSpec(w_up_gla.shape, lambda i: (0, 0), **const),
            pl.BlockSpec(w_out.shape, lambda i: (0, 0), **const),
            pl.BlockSpec((1, d), lambda i: (0, 0)),
            *w_specs,
            pl.BlockSpec((1, d), lambda i: (0, 0)),
        ],
        out_specs=pl.BlockSpec((tm, d), lambda i: (tile(i), 0)),
        out_shape=jax.ShapeDtypeStruct((n, d), F32),
        scratch_shapes=w_scratch,
        compiler_params=pltpu.CompilerParams(dimension_semantics=("arbitrary",), vmem_limit_bytes=58 * MIB),
        name="merge_ffn",
    )(x2d, mod3, y_hg, y_gla, r, r, w_up_hg, w_up_gla, w_out, gain.reshape(1, d), w_in_all, w_in_all,
      w_down_all, fgain.reshape(1, d))


def kernel(x, c, w_ada, b_ada, norm_gains, ffn1_w_in, ffn1_w_down, w_in_mix, w_gk_up, b_gk_up, lb_logits,
           hg_norm, gla_norm, w_up_hg, w_up_gla, w_out, ffn2_w_in, ffn2_w_down, final_norm):
    batch, seq, d = x.shape
    depth = w_ada.shape[0]

    x2d = x.reshape(batch * seq, d)
    c_pad = jnp.pad(c, ((0, SUBLANES - batch % SUBLANES if batch % SUBLANES else 0), (0, 0)))
    for l in range(depth):
        mod = _ada(c_pad, w_ada[l], b_ada[l])[:batch].reshape(batch, 3 * N_SUB, d)

        x2d = _ffn(x2d, mod, norm_gains[l, 0], ffn1_w_in, ffn1_w_down, final_norm, layer=l, sub=0,
                   tokens_per_batch=seq, final=False)

        wup_pad = jnp.pad(w_gk_up[l], ((0, LANES - GLA_RANK), (0, 0)))
        dec, act, r, stats = _inproj(x2d, mod, norm_gains[l, 1], jnp.swapaxes(w_in_mix, 1, 2), lb_logits, wup_pad,
                                b_gk_up[l], layer=l, sub=1, tokens_per_batch=seq)

        y_hg, y_gla = _recurrence(dec, act, stats, hg_norm[l], gla_norm[l], batch=batch, tokens_per_batch=seq)

        x2d = _merge_ffn(x2d, mod, y_hg, y_gla, r, w_up_hg[l].astype(BF16), w_up_gla[l].astype(BF16),
                         w_out[l].astype(BF16), norm_gains[l, 2], ffn2_w_in, ffn2_w_down, final_norm, layer=l,
                         sub_mix=1, sub_ffn=2, tokens_per_batch=seq, final=l == depth - 1)
    if depth == 0:
        raise ValueError("depth must be >= 1")
    return x2d.reshape(batch, seq, d)
```

```python
import functools

import jax
import jax.numpy as jnp
from jax import lax
from jax.experimental import pallas as pl
from jax.experimental.pallas import tpu as pltpu

F32 = jnp.float32
BF16 = jnp.bfloat16

EPS = 1e-6
N_SUB = 3
HG_HEADS = 4
HG_DK = 128
HG_DV = 128
GLA_HEADS = 4
GLA_DK = 64
GLA_DV = 128
GLA_RANK = 16
GLA_NORMALIZER = 16.0

LANES = 128
SUBLANES = 8
MXU_WIDTH = 256
FFN_SECTIONS = 4
MIB = 1024 * 1024

REC_CHUNK = 64
PERM_STRIDE = 4
PERM_GROUP = PERM_STRIDE * SUBLANES
N_GROUPS = REC_CHUNK // PERM_GROUP
REC_UNROLL = 2
LOG2E = 1.4426950408889634
SAFE_EXP2 = 100.0
SAFE_MAG = 1e7


def _dot(a, b):
    return jnp.dot(a, b, preferred_element_type=F32)


def _dot_nt(a, b):
    return lax.dot_general(a, b, (((1,), (1,)), ((), ())), preferred_element_type=F32)


def _dot_tn(a, b):
    return lax.dot_general(a, b, (((0,), (0,)), ((), ())), preferred_element_type=F32)


def _silu(v):
    return v * jax.nn.sigmoid(v)


def _norm_mod(x, gain, shift, scale):
    y = x * lax.rsqrt(jnp.mean(x * x, axis=-1, keepdims=True) + EPS) * gain
    return y * (1.0 + scale) + shift


def _ada_kernel(c_ref, w_ref, b_ref, o_ref):
    cond = _silu(c_ref[...]).astype(BF16)
    o_ref[...] = _dot(cond, w_ref[...].astype(BF16)) + b_ref[...]


def _ada(c_pad, w, b):
    rows, d = c_pad.shape
    n = w.shape[1]
    bn = n // 8
    return pl.pallas_call(
        _ada_kernel,
        grid=(n // bn,),
        in_specs=[
            pl.BlockSpec((rows, d), lambda j: (0, 0)),
            pl.BlockSpec((d, bn), lambda j: (0, j)),
            pl.BlockSpec((1, bn), lambda j: (0, j)),
        ],
        out_specs=pl.BlockSpec((rows, bn), lambda j: (0, j)),
        out_shape=jax.ShapeDtypeStruct((rows, n), F32),
        name="adaln",
    )(c_pad, w, b.reshape(1, n))


def _mxu_aligned_bounds(width, parts):
    if width % MXU_WIDTH:
        return (0, width)
    tiles = width // MXU_WIDTH
    return tuple(MXU_WIDTH * ((tiles * p + parts - 1) // parts) for p in range(parts)) + (width,)


def _ffn_weight_specs(layer, d, d_ff):
    assert d_ff % MXU_WIDTH == 0
    steps = d_ff // MXU_WIDTH

    def chunk(i):
        return jnp.minimum(i, steps - 1)

    specs = [
        pl.BlockSpec((None, d, MXU_WIDTH), lambda i: (layer, 0, chunk(i))),
        pl.BlockSpec((None, d, MXU_WIDTH), lambda i: (layer, 0, steps + chunk(i))),
        pl.BlockSpec((None, MXU_WIDTH, d), lambda i: (layer, chunk(i), 0)),
    ]
    scratch = [pltpu.VMEM((d, 2 * d_ff), BF16), pltpu.VMEM((d_ff, d), BF16)]
    return steps, specs, scratch


def _stage_ffn_weights(step, wg_ref, wu_ref, wd_ref, w_in_s, w_down_s, d_ff):
    for j in range(d_ff // MXU_WIDTH):
        @pl.when(step == j)
        def _(lo=j * MXU_WIDTH):
            w_in_s[:, lo:lo + MXU_WIDTH] = wg_ref[...].astype(BF16)
            w_in_s[:, d_ff + lo:d_ff + lo + MXU_WIDTH] = wu_ref[...].astype(BF16)
            w_down_s[lo:lo + MXU_WIDTH, :] = wd_ref[...].astype(BF16)


def _ffn_kernel(x_ref, mod_ref, gain_ref, wg_ref, wu_ref, wd_ref, fgain_ref, o_ref, w_in_s, w_down_s, *,
                sub, d_ff, bounds, final, w_steps):
    step = pl.program_id(0)
    _stage_ffn_weights(step, wg_ref, wu_ref, wd_ref, w_in_s, w_down_s, d_ff)

    @pl.when(step >= w_steps)
    def _():
        o_ref[...] = _ffn_body(x_ref[...], mod_ref, gain_ref, w_in_s, w_down_s, fgain_ref,
                               sub=sub, d_ff=d_ff, bounds=bounds, final=final)


def _ffn_body(x, mod_ref, gain_ref, w_in_ref, w_down_ref, fgain_ref, *, sub, d_ff, bounds, final):
    shift = mod_ref[0, 3 * sub + 0:3 * sub + 1, :]
    scale = mod_ref[0, 3 * sub + 1:3 * sub + 2, :]
    gate = mod_ref[0, 3 * sub + 2:3 * sub + 3, :]
    hb = _norm_mod(x, gain_ref[...], shift, scale).astype(BF16)
    acc = None
    for lo, hi in zip(bounds[:-1], bounds[1:]):
        g = _dot(hb, w_in_ref[:, lo:hi])
        u = _dot(hb, w_in_ref[:, d_ff + lo:d_ff + hi])
        act = (_silu(g) * u).astype(BF16)
        part = _dot(act, w_down_ref[lo:hi, :])
        acc = part if acc is None else acc + part
    xn = x + (0.5 * gate) * acc
    if final:
        xn = xn * lax.rsqrt(jnp.mean(xn * xn, axis=-1, keepdims=True) + EPS) * fgain_ref[...]
    return xn


def _ffn(x2d, mod3, gain, w_in_all, w_down_all, fgain, *, layer, sub, tokens_per_batch, final, tm=1024):
    n, d = x2d.shape
    d_ff = w_down_all.shape[1]
    tiles_per_batch = tokens_per_batch // tm
    w_steps, w_specs, w_scratch = _ffn_weight_specs(layer, d, d_ff)

    def tile(i):
        return jnp.maximum(i - w_steps, 0)

    return pl.pallas_call(
        functools.partial(_ffn_kernel, sub=sub, d_ff=d_ff, bounds=_mxu_aligned_bounds(d_ff, FFN_SECTIONS), final=final,
                          w_steps=w_steps),
        grid=(w_steps + n // tm,),
        in_specs=[
            pl.BlockSpec((tm, d), lambda i: (tile(i), 0)),
            pl.BlockSpec((1, 3 * N_SUB, d), lambda i: (tile(i) // tiles_per_batch, 0, 0)),
            pl.BlockSpec((1, d), lambda i: (0, 0)),
            *w_specs,
            pl.BlockSpec((1, d), lambda i: (0, 0)),
        ],
        out_specs=pl.BlockSpec((tm, d), lambda i: (tile(i), 0)),
        out_shape=jax.ShapeDtypeStruct((n, d), F32),
        scratch_shapes=w_scratch,
        compiler_params=pltpu.CompilerParams(dimension_semantics=("arbitrary",), vmem_limit_bytes=52 * MIB),
        name="ffn_final" if final else "ffn",
    )(x2d, mod3, gain.reshape(1, d), w_in_all, w_in_all, w_down_all, fgain.reshape(1, d))


MIX_W_CHUNK = 512


DEC_OFF = {"g": 0, "gd": 4}
DEC_TILES = 6
ACT_OFF = {"q": 0, "k": 4, "v": 8, "sg": 12, "gq": 16, "gk": 18, "gv": 20, "gsg": 24}
ACT_TILES = 28


def _inproj_kernel(x_ref, mod_ref, gain_ref, wt_ref, lbl_ref, wup_ref, bup_ref, dec_ref, act_ref, r_ref, stats_ref,
                   wt_s, *, sub, layer, n_cols, w_steps):
    step = pl.program_id(0)
    for j in range(w_steps):
        @pl.when(step == j)
        def _(lo=j * MIX_W_CHUNK):
            w = wt_ref[...]
            if lo + MIX_W_CHUNK > n_cols:
                row = lo + lax.broadcasted_iota(jnp.int32, w.shape, 0)
                w = jnp.where(row < n_cols, w, 0.0)
            wt_s[lo:lo + MIX_W_CHUNK, :] = w.astype(BF16)

    @pl.when(step >= w_steps)
    def _():
        shift = mod_ref[0, 3 * sub + 0:3 * sub + 1, :]
        scale = mod_ref[0, 3 * sub + 1:3 * sub + 2, :]
        hb = _norm_mod(x_ref[...], gain_ref[...], shift, scale).astype(BF16)
        tm = hb.shape[0]
        hg_qk, hg_w = HG_HEADS * HG_DK, HG_HEADS * HG_DV
        gla_k, gla_v = GLA_HEADS * GLA_DK, GLA_HEADS * GLA_DV
        n_hg = 2 * hg_qk + 2 * hg_w
        n_rec = n_hg + 2 * gla_k + 2 * gla_v

        def put(name, val):
            if name in DEC_OFF:
                lo = DEC_OFF[name] * LANES
                dec_ref[:, lo:lo + val.shape[1]] = val
            else:
                lo = ACT_OFF[name] * LANES
                act_ref[:, lo:lo + val.shape[1]] = val.astype(act_ref.dtype)

        def chunk_min(g):
            tot = jnp.sum(g.reshape(tm // REC_CHUNK, REC_CHUNK, g.shape[1]), axis=1)
            return functools.reduce(jnp.minimum, [tot[:, j:j + LANES] for j in range(0, g.shape[1], LANES)])

        def mag_max(v):
            m = jnp.max(jnp.abs(v).reshape(tm // SUBLANES, SUBLANES, v.shape[1]), axis=0)
            return functools.reduce(jnp.maximum, [m[:, j:j + LANES] for j in range(0, v.shape[1], LANES)])

        ph = _dot_nt(hb, wt_s[:n_hg, :])
        lbl = lbl_ref[...]
        ex = jnp.exp(lbl - jnp.max(lbl, axis=0, keepdims=True))
        lb = jnp.sum(ex[:layer + 1], axis=0, keepdims=True) / jnp.sum(ex, axis=0, keepdims=True)
        c1 = 0.5 * (1.0 - lb)
        q = _silu_tanh(ph[:, :hg_qk])
        p = c1 * jnp.tanh(0.5 * ph[:, hg_qk:2 * hg_qk])
        g = jnp.log2((1.0 - c1) + p)
        put("q", q)
        put("k", c1 - p)
        put("g", g)
        put("v", ph[:, 2 * hg_qk:2 * hg_qk + hg_w])
        put("sg", _silu_tanh(ph[:, 2 * hg_qk + hg_w:]))

        pg = _dot_nt(hb, wt_s[n_hg:n_rec, :])
        code = _dot_nt(hb, wt_s[n_rec:n_rec + LANES, :])
        logits = _dot(code.astype(BF16), wup_ref[...].astype(BF16)) + bup_ref[...]
        gd = (jnp.minimum(logits, 0.0) - jnp.log(1.0 + jnp.exp(-jnp.abs(logits)))) * (LOG2E / GLA_NORMALIZER)
        gq = pg[:, :gla_k] * (GLA_DK ** -0.5)
        gk = pg[:, gla_k:2 * gla_k]
        put("gq", gq)
        put("gk", gk)
        put("gd", gd)
        put("gv", pg[:, 2 * gla_k:2 * gla_k + gla_v])
        put("gsg", _silu_tanh(pg[:, 2 * gla_k + gla_v:]))

        r_ref[...] = _dot_nt(hb, wt_s[n_rec + GLA_RANK:n_rec + GLA_RANK + r_ref.shape[1], :]).astype(r_ref.dtype)
        stats_ref[0, 0] = jnp.minimum(chunk_min(g), chunk_min(gd))
        stats_ref[0, 1] = functools.reduce(jnp.maximum, [mag_max(q), mag_max(gq), mag_max(gk)])


def _inproj(x2d, mod3, gain, wt_all, lb_logits, wup_pad, bup, *, layer, sub, tokens_per_batch, tm=512):
    n, d = x2d.shape
    n_cols = wt_all.shape[1]
    n_rec = 2 * HG_HEADS * HG_DK + 2 * HG_HEADS * HG_DV + 2 * GLA_HEADS * GLA_DK + 2 * GLA_HEADS * GLA_DV
    n_r = n_cols - n_rec - GLA_RANK
    w_steps = pl.cdiv(n_cols, MIX_W_CHUNK)
    assert n_rec + LANES <= w_steps * MIX_W_CHUNK and tm // REC_CHUNK == SUBLANES
    tiles_per_batch = tokens_per_batch // tm

    def tile(i):
        return jnp.maximum(i - w_steps, 0)

    return pl.pallas_call(
        functools.partial(_inproj_kernel, sub=sub, layer=layer, n_cols=n_cols, w_steps=w_steps),
        grid=(w_steps + n // tm,),
        in_specs=[
            pl.BlockSpec((tm, d), lambda i: (tile(i), 0)),
            pl.BlockSpec((1, 3 * N_SUB, d), lambda i: (tile(i) // tiles_per_batch, 0, 0)),
            pl.BlockSpec((1, d), lambda i: (0, 0)),
            pl.BlockSpec((None, MIX_W_CHUNK, d), lambda i: (layer, jnp.minimum(i, w_steps - 1), 0)),
            pl.BlockSpec(lb_logits.shape, lambda i: (0, 0)),
            pl.BlockSpec(wup_pad.shape, lambda i: (0, 0)),
            pl.BlockSpec((1, bup.shape[0]), lambda i: (0, 0)),
        ],
        out_specs=[
            pl.BlockSpec((tm, DEC_TILES * LANES), lambda i: (tile(i), 0)),
            pl.BlockSpec((tm, ACT_TILES * LANES), lambda i: (tile(i), 0)),
            pl.BlockSpec((tm, n_r), lambda i: (tile(i), 0)),
            pl.BlockSpec((1, 2, SUBLANES, LANES), lambda i: (tile(i), 0, 0, 0)),
        ],
        out_shape=[
            jax.ShapeDtypeStruct((n, DEC_TILES * LANES), F32),
            jax.ShapeDtypeStruct((n, ACT_TILES * LANES), BF16),
            jax.ShapeDtypeStruct((n, n_r), BF16),
            jax.ShapeDtypeStruct((n // tm, 2, SUBLANES, LANES), F32),
        ],
        scratch_shapes=[pltpu.VMEM((w_steps * MIX_W_CHUNK, d), BF16)],
        compiler_params=pltpu.CompilerParams(dimension_semantics=("arbitrary",), vmem_limit_bytes=56 * MIB),
        name="inproj",
    )(x2d, mod3, gain.reshape(1, d), wt_all, lb_logits, wup_pad, bup.reshape(1, -1))


def _load_perm(ref, base):
    return [ref[pl.ds(base + PERM_GROUP * c + a, SUBLANES, stride=PERM_STRIDE), :]
            for c in range(N_GROUPS) for a in range(PERM_STRIDE)]


def _store_perm(ref, base, val):
    for c in range(N_GROUPS):
        for a in range(PERM_STRIDE):
            u = PERM_STRIDE * c + a
            ref[pl.ds(base + PERM_GROUP * c + a, SUBLANES, stride=PERM_STRIDE), :] = (
                val[SUBLANES * u:SUBLANES * (u + 1), :])


def _shift_down(v, d, sub_iota):
    return jnp.where(sub_iota >= d, pltpu.roll(v, d, 0), 0.0)


def _cumsum_perm(g, sub_iota):
    out = []
    carry = None
    for c in range(N_GROUPS):
        pre = [g[PERM_STRIDE * c]]
        for a in range(1, PERM_STRIDE):
            pre.append(pre[-1] + g[PERM_STRIDE * c + a])
        inc = pre[-1]
        inc = inc + _shift_down(inc, 1, sub_iota)
        inc = inc + _shift_down(inc, 2, sub_iota)
        inc = inc + _shift_down(inc, 4, sub_iota)
        exc = _shift_down(inc, 1, sub_iota)
        if carry is not None:
            exc = exc + carry
        out.extend(p + exc for p in pre)
        tot = inc[SUBLANES - 1:SUBLANES, :]
        carry = tot if carry is None else carry + tot
    return jnp.concatenate(out, axis=0), carry


def _perm_token_ids(shape, dim):
    r = lax.broadcasted_iota(jnp.int32, shape, dim)
    u = r >> 3
    return PERM_GROUP * (u >> 2) + PERM_STRIDE * (r & 7) + (u & 3)


def _silu_tanh(v):
    h = 0.5 * v
    return h * jnp.tanh(h) + h


def _finish(o, gain, act_gate):
    return o * lax.rsqrt(jnp.mean(o * o, axis=-1, keepdims=True) + EPS) * gain * act_gate


def _rec_kernel(safe_ref, q0, q1, k0, k1, g0, g1, v0, v1, sg0, sg1, gq, gk, gd, gv0, gv1, gsg0, gsg1,
                hgn_ref, glan_ref, yh_ref, yg_ref, st_s, e_s, fb_s, oi_s, *, tt, unroll):
    nc = tt // REC_CHUNK
    nb = q0.shape[0]

    @pl.when(pl.program_id(1) == 0)
    def _():
        st_s[...] = jnp.zeros_like(st_s)

    safe = safe_ref[pl.program_id(1)] != 0

    sub_iota = lax.broadcasted_iota(jnp.int32, (SUBLANES, LANES), 0)
    lane = lax.broadcasted_iota(jnp.int32, (REC_CHUNK, LANES), 1)
    causal = (lax.broadcasted_iota(jnp.int32, (REC_CHUNK, REC_CHUNK), 1)
              <= lax.broadcasted_iota(jnp.int32, (REC_CHUNK, REC_CHUNK), 0))
    tok_of_row = lax.broadcasted_iota(jnp.int32, (REC_CHUNK, LANES), 0)
    hgn = hgn_ref[...]
    glan = glan_ref[...]

    head_masks = (lane < GLA_DK, lane >= GLA_DK)
    groups = []
    for bi in range(nb):
        groups += [
            ((q0.at[bi], k0.at[bi], g0.at[bi]), st_s.at[bi, 0],
             [(None, v0.at[bi], sg0.at[bi], hgn, yh_ref.at[0, bi])]),
            ((q1.at[bi], k1.at[bi], g1.at[bi]), st_s.at[bi, 1],
             [(None, v1.at[bi], sg1.at[bi], hgn, yh_ref.at[1, bi])]),
            ((gq.at[bi], gk.at[bi], gd.at[bi]), st_s.at[bi, 2],
             [(head_masks[0], gv0.at[bi], gsg0.at[bi], glan, yg_ref.at[0, bi]),
              (head_masks[1], gv1.at[bi], gsg1.at[bi], glan, yg_ref.at[1, bi])]),
        ]

    def natural(ref, base):
        return ref[pl.ds(base, REC_CHUNK), :]

    def decay_logs(g_ref, base):
        return _cumsum_perm(_load_perm(g_ref, base), sub_iota)

    def to_natural(slot, val):
        _store_perm(e_s.at[slot], 0, val)
        return e_s[slot]

    def masked(x, mask):
        return x if mask is None else jnp.where(mask, x, jnp.zeros_like(x))

    def fast_chunk(ci, carry):
        staged = []
        for sub in range(unroll):
            base = pl.multiple_of((ci * unroll + sub) * REC_CHUNK, REC_CHUNK)
            for gi, ((q_ref, k_ref, g_ref), st_view, heads) in enumerate(groups):
                b, b_last = decay_logs(g_ref, base)
                half = 0.5 * b_last
                d = b - half
                em = jnp.exp2(half)
                slot = 2 * (sub * len(groups) + gi)
                qt_b = (natural(q_ref, base).astype(F32) * to_natural(slot, jnp.exp2(d))).astype(BF16)
                kt_b = (natural(k_ref, base).astype(F32) * to_natural(slot + 1, jnp.exp2(-d))).astype(BF16)
                vbs = [natural(v_ref, base) for _, v_ref, _, _, _ in heads]
                q_all = jnp.concatenate([masked(qt_b, m) for m, *_ in heads], axis=0)
                k_all = jnp.concatenate([masked(kt_b, m) for m, *_ in heads], axis=0)
                v_all = jnp.concatenate(vbs, axis=0)
                state_e = st_view[...] * em
                scores = _dot_nt(q_all, kt_b)
                o_inter = _dot_nt(q_all, state_e.astype(BF16))
                st_view[...] = (state_e + _dot_tn(v_all, k_all)) * em
                for h, (_, _, gate_ref, gain, y_view) in enumerate(heads):
                    rows = slice(h * REC_CHUNK, (h + 1) * REC_CHUNK)
                    staged.append((scores[rows], o_inter[rows], vbs[h], base, gate_ref, gain, y_view))
        for scores, o_inter, vb, base, gate_ref, gain, y_view in staged:
            s = jnp.where(causal, scores, 0.0).astype(BF16)
            o = o_inter + _dot(s, vb)
            gate = natural(gate_ref, base).astype(F32)
            y_view[pl.ds(base, REC_CHUNK), :] = _finish(o, gain, gate).astype(y_view.dtype)
        return carry

    def exact_chunk(ci, carry):
        base = pl.multiple_of(ci * REC_CHUNK, REC_CHUNK)
        for (q_ref, k_ref, g_ref), st_view, heads in groups:
            b_perm, b_last = decay_logs(g_ref, base)
            b = to_natural(0, b_perm)
            q_all = natural(q_ref, base).astype(F32)
            k = natural(k_ref, base).astype(F32)
            state = st_view[...]
            new_state = state * jnp.exp2(b_last)
            kd = k * jnp.exp2(b_last - b)
            for qmask, v_ref, gate_ref, gain, y_view in heads:
                q = masked(q_all, qmask)
                vb = natural(v_ref, base)
                gate = natural(gate_ref, base).astype(F32)
                fb_s[0] = b
                fb_s[1] = q
                fb_s[2] = k
                fb_s[3] = vb.astype(F32)

                def row(r, c2):
                    bt = fb_s[0, pl.ds(r, 1), :]
                    qt = fb_s[1, pl.ds(r, 1), :]
                    w = jnp.exp2(jnp.where(tok_of_row <= r, bt - fb_s[0], -jnp.inf))
                    sc = jnp.sum(qt * w * fb_s[2], axis=-1, keepdims=True)
                    oi_s[pl.ds(r, 1), :] = jnp.sum(sc * fb_s[3], axis=0, keepdims=True)
                    return c2

                lax.fori_loop(0, REC_CHUNK, row, 0)
                o = _dot_nt((q * jnp.exp2(b)).astype(BF16), state.astype(BF16)) + oi_s[...]
                new_state = new_state + _dot_tn(vb, masked(kd, qmask).astype(BF16))
                y_view[pl.ds(base, REC_CHUNK), :] = _finish(o, gain, gate).astype(y_view.dtype)
            st_view[...] = new_state
        return carry

    @pl.when(safe)
    def _():
        lax.fori_loop(0, nc // unroll, fast_chunk, 0)

    @pl.when(jnp.logical_not(safe))
    def _():
        lax.fori_loop(0, nc, exact_chunk, 0)


def _safe_flags(stats, *, batch, tokens_per_batch, tt):
    tiles_per_batch = stats.shape[0] // batch
    chunk_tot = jnp.min(stats[:, 0], axis=-1).reshape(batch, tokens_per_batch // tt, -1)
    tile_mag = jnp.max(stats[:, 1], axis=(-2, -1)).reshape(batch, tiles_per_batch)
    step_tot = jnp.min(chunk_tot, axis=(0, 2))
    step_mag = jnp.repeat(jnp.max(tile_mag, axis=0), (tokens_per_batch // tt) // tiles_per_batch)
    return jnp.logical_and(step_tot >= -2.0 * SAFE_EXP2, step_mag <= SAFE_MAG).astype(jnp.int32)


def _recurrence(dec, act, stats, hg_norm, gla_norm, *, batch, tokens_per_batch, tt=512):
    n = dec.shape[0]
    nt = tokens_per_batch // tt
    pairs = HG_HEADS // 2
    dec3 = dec.reshape(batch, tokens_per_batch, dec.shape[1])
    act3 = act.reshape(batch, tokens_per_batch, act.shape[1])
    safe = _safe_flags(stats, batch=batch, tokens_per_batch=tokens_per_batch, tt=tt)

    def tile_of(name, p, e):
        off = DEC_OFF[name] if name in DEC_OFF else ACT_OFF[name]
        return off + (p if e is None else 2 * p + e)

    def spec(name, e=None):
        return pl.BlockSpec((batch, tt, LANES), lambda p, i, flags: (0, i, tile_of(name, p, e)))

    names = [("q", 0), ("q", 1), ("k", 0), ("k", 1), ("g", 0), ("g", 1), ("v", 0), ("v", 1), ("sg", 0), ("sg", 1),
             ("gq", None), ("gk", None), ("gd", None), ("gv", 0), ("gv", 1), ("gsg", 0), ("gsg", 1)]
    in_specs = [spec(name, e) for name, e in names] + [
        pl.BlockSpec((1, LANES), lambda p, i, flags: (0, 0)),
        pl.BlockSpec((1, LANES), lambda p, i, flags: (0, 0)),
    ]
    operands = [dec3 if name in DEC_OFF else act3 for name, _ in names]
    n_groups = 3 * batch
    out_spec = pl.BlockSpec((2, batch, tt, LANES), lambda p, i, flags: (p, 0, i, 0))
    y_hg, y_gla = pl.pallas_call(
        functools.partial(_rec_kernel, tt=tt, unroll=REC_UNROLL),
        grid_spec=pltpu.PrefetchScalarGridSpec(
            num_scalar_prefetch=1,
            grid=(pairs, nt),
            in_specs=in_specs,
            out_specs=[out_spec, out_spec],
            scratch_shapes=[
                pltpu.VMEM((batch, 3, LANES, LANES), F32),
                pltpu.VMEM((2 * REC_UNROLL * n_groups, REC_CHUNK, LANES), F32),
                pltpu.VMEM((4, REC_CHUNK, LANES), F32),
                pltpu.VMEM((REC_CHUNK, LANES), F32),
            ],
        ),
        out_shape=[jax.ShapeDtypeStruct((HG_HEADS, batch, tokens_per_batch, HG_DV), BF16),
                   jax.ShapeDtypeStruct((GLA_HEADS, batch, tokens_per_batch, GLA_DV), BF16)],
        compiler_params=pltpu.CompilerParams(
            dimension_semantics=("arbitrary", "arbitrary"), vmem_limit_bytes=48 * MIB),
        name="recurrence",
    )(safe, *operands, hg_norm.reshape(1, HG_DV), gla_norm.reshape(1, GLA_DV))
    return y_hg.reshape(HG_HEADS, n, HG_DV), y_gla.reshape(GLA_HEADS, n, GLA_DV)


def _merge_ffn_kernel(x_ref, mod_ref, yh_ref, yg_ref, rh_ref, rg_ref, wuh_ref, wug_ref, wo_ref,
                      gain_ref, wg_ref, wu_ref, wd_ref, fgain_ref, o_ref, w_in_s, w_down_s, *,
                      sub_mix, sub_ffn, d_ff, bounds, final, w_steps):
    step = pl.program_id(0)
    _stage_ffn_weights(step, wg_ref, wu_ref, wd_ref, w_in_s, w_down_s, d_ff)

    @pl.when(step >= w_steps)
    def _():
        gate = mod_ref[0, 3 * sub_mix + 2:3 * sub_mix + 3, :]

        def up(y_ref, w_ref):
            y = jnp.concatenate([y_ref[h].astype(BF16) for h in range(y_ref.shape[0])], axis=-1)
            return _dot(y, w_ref[...])

        merged = (jax.nn.sigmoid(rh_ref[...].astype(F32)) * up(yh_ref, wuh_ref)
                  + jax.nn.sigmoid(rg_ref[...].astype(F32)) * up(yg_ref, wug_ref))
        x = x_ref[...] + gate * _dot(merged.astype(BF16), wo_ref[...])
        o_ref[...] = _ffn_body(x, mod_ref, gain_ref, w_in_s, w_down_s, fgain_ref,
                               sub=sub_ffn, d_ff=d_ff, bounds=bounds, final=final)


def _merge_ffn(x2d, mod3, y_hg, y_gla, r, w_up_hg, w_up_gla, w_out, gain, w_in_all, w_down_all, fgain, *,
               layer, sub_mix, sub_ffn, tokens_per_batch, final, tm=512):
    n, d = x2d.shape
    d_ff = w_down_all.shape[1]
    tiles_per_batch = tokens_per_batch // tm
    w_steps, w_specs, w_scratch = _ffn_weight_specs(layer, d, d_ff)
    const = dict(pipeline_mode=pl.Buffered(1))

    def tile(i):
        return jnp.maximum(i - w_steps, 0)

    return pl.pallas_call(
        functools.partial(_merge_ffn_kernel, sub_mix=sub_mix, sub_ffn=sub_ffn, d_ff=d_ff,
                          bounds=_mxu_aligned_bounds(d_ff, FFN_SECTIONS), final=final, w_steps=w_steps),
        grid=(w_steps + n // tm,),
        in_specs=[
            pl.BlockSpec((tm, d), lambda i: (tile(i), 0)),
            pl.BlockSpec((1, 3 * N_SUB, d), lambda i: (tile(i) // tiles_per_batch, 0, 0)),
            pl.BlockSpec((HG_HEADS, tm, HG_DV), lambda i: (0, tile(i), 0)),
            pl.BlockSpec((GLA_HEADS, tm, GLA_DV), lambda i: (0, tile(i), 0)),
            pl.BlockSpec((tm, d), lambda i: (tile(i), 0)),
            pl.BlockSpec((tm, d), lambda i: (tile(i), 1)),
            pl.BlockSpec(w_up_hg.shape, lambda i: (0, 0), **const),
            pl.BlockSpec(w_up_gla.shape, lambda i: (0, 0), **const),
            pl.BlockSpec(w_out.shape, lambda i: (0, 0), **const),
            pl.BlockSpec((1, d), lambda i: (0, 0)),
            *w_specs,
            pl.BlockSpec((1, d), lambda i: (0, 0)),
        ],
        out_specs=pl.BlockSpec((tm, d), lambda i: (tile(i), 0)),
        out_shape=jax.ShapeDtypeStruct((n, d), F32),
        scratch_shapes=w_scratch,
        compiler_params=pltpu.CompilerParams(dimension_semantics=("arbitrary",), vmem_limit_bytes=58 * MIB),
        name="merge_ffn",
    )(x2d, mod3, y_hg, y_gla, r, r, w_up_hg, w_up_gla, w_out, gain.reshape(1, d), w_in_all, w_in_all,
      w_down_all, fgain.reshape(1, d))


def kernel(x, c, w_ada, b_ada, norm_gains, ffn1_w_in, ffn1_w_down, w_in_mix, w_gk_up, b_gk_up, lb_logits,
           hg_norm, gla_norm, w_up_hg, w_up_gla, w_out, ffn2_w_in, ffn2_w_down, final_norm):
    batch, seq, d = x.shape
    depth = w_ada.shape[0]

    x2d = x.reshape(batch * seq, d)
    c_pad = jnp.pad(c, ((0, SUBLANES - batch % SUBLANES if batch % SUBLANES else 0), (0, 0)))
    for l in range(depth):
        mod = _ada(c_pad, w_ada[l], b_ada[l])[:batch].reshape(batch, 3 * N_SUB, d)

        x2d = _ffn(x2d, mod, norm_gains[l, 0], ffn1_w_in, ffn1_w_down, final_norm, layer=l, sub=0,
                   tokens_per_batch=seq, final=False)

        wup_pad = jnp.pad(w_gk_up[l], ((0, LANES - GLA_RANK), (0, 0)))
        dec, act, r, stats = _inproj(x2d, mod, norm_gains[l, 1], jnp.swapaxes(w_in_mix, 1, 2), lb_logits, wup_pad,
                                b_gk_up[l], layer=l, sub=1, tokens_per_batch=seq)

        y_hg, y_gla = _recurrence(dec, act, stats, hg_norm[l], gla_norm[l], batch=batch, tokens_per_batch=seq)

        x2d = _merge_ffn(x2d, mod, y_hg, y_gla, r, w_up_hg[l].astype(BF16), w_up_gla[l].astype(BF16),
                         w_out[l].astype(BF16), norm_gains[l, 2], ffn2_w_in, ffn2_w_down, final_norm, layer=l,
                         sub_mix=1, sub_ffn=2, tokens_per_batch=seq, final=l == depth - 1)
    if depth == 0:
        raise ValueError("depth must be >= 1")
    return x2d.reshape(batch, seq, d)
```

```python
import functools

import jax
import jax.numpy as jnp
from jax import lax
from jax.experimental import pallas as pl
from jax.experimental.pallas import tpu as pltpu

F32 = jnp.float32
BF16 = jnp.bfloat16

EPS = 1e-6
N_SUB = 3
HG_HEADS = 4
HG_DK = 128
HG_DV = 128
GLA_HEADS = 4
GLA_DK = 64
GLA_DV = 128
GLA_RANK = 16
GLA_NORMALIZER = 16.0

LANES = 128
SUBLANES = 8
MXU_WIDTH = 256
MIB = 1024 * 1024

TOKEN_TILE = 512
REC_TILE = 512
ADA_STEPS = 4
VMEM_LIMIT = {"ffn": 52 * MIB, "inproj": 56 * MIB, "recurrence": 48 * MIB, "merge_ffn": 58 * MIB}

REC_CHUNK = 64
PERM_STRIDE = 4
PERM_GROUP = PERM_STRIDE * SUBLANES
N_GROUPS = REC_CHUNK // PERM_GROUP
REC_UNROLL = 8
LOG2E = 1.4426950408889634
SAFE_EXP2 = 100.0
SAFE_MAG = 1e7


def _dot(a, b):
    return jnp.dot(a, b, preferred_element_type=F32)


def _dot_nt(a, b):
    return lax.dot_general(a, b, (((1,), (1,)), ((), ())), preferred_element_type=F32)


def _dot_tn(a, b):
    return lax.dot_general(a, b, (((0,), (0,)), ((), ())), preferred_element_type=F32)


def _silu(v):
    return v * jax.nn.sigmoid(v)


def _norm_mod(x, gain, shift, scale):
    y = x * lax.rsqrt(jnp.mean(x * x, axis=-1, keepdims=True) + EPS) * gain
    return y * (1.0 + scale) + shift


def _ada_kernel(c_ref, w_ref, b_ref, o_ref):
    cond = _silu(c_ref[...]).astype(BF16)
    o_ref[...] = _dot(cond, w_ref[...].astype(BF16)) + b_ref[...]


def _ada(c_pad, w, b):
    rows, d = c_pad.shape
    n = w.shape[1]
    bn = n // ADA_STEPS
    return pl.pallas_call(
        _ada_kernel,
        grid=(n // bn,),
        in_specs=[
            pl.BlockSpec((rows, d), lambda j: (0, 0)),
            pl.BlockSpec((d, bn), lambda j: (0, j)),
            pl.BlockSpec((1, bn), lambda j: (0, j)),
        ],
        out_specs=pl.BlockSpec((rows, bn), lambda j: (0, j)),
        out_shape=jax.ShapeDtypeStruct((rows, n), F32),
        name="adaln",
    )(c_pad, w, b.reshape(1, n))


def _mxu_aligned_bounds(width, parts):
    if width % MXU_WIDTH:
        return (0, width)
    tiles = width // MXU_WIDTH
    return tuple(MXU_WIDTH * ((tiles * p + parts - 1) // parts) for p in range(parts)) + (width,)


def _ffn_weight_specs(layer, d, d_ff):
    assert d_ff % MXU_WIDTH == 0
    steps = d_ff // MXU_WIDTH

    def chunk(i):
        return jnp.minimum(i, steps - 1)

    specs = [
        pl.BlockSpec((None, d, MXU_WIDTH), lambda i: (layer, 0, chunk(i))),
        pl.BlockSpec((None, d, MXU_WIDTH), lambda i: (layer, 0, steps + chunk(i))),
        pl.BlockSpec((None, MXU_WIDTH, d), lambda i: (layer, chunk(i), 0)),
    ]
    scratch = [pltpu.VMEM((d, 2 * d_ff), BF16), pltpu.VMEM((d_ff, d), BF16)]
    return steps, specs, scratch


def _stage_ffn_weights(step, wg_ref, wu_ref, wd_ref, w_in_s, w_down_s, d_ff):
    for j in range(d_ff // MXU_WIDTH):
        @pl.when(step == j)
        def _(lo=j * MXU_WIDTH):
            w_in_s[:, lo:lo + MXU_WIDTH] = wg_ref[...].astype(BF16)
            w_in_s[:, d_ff + lo:d_ff + lo + MXU_WIDTH] = wu_ref[...].astype(BF16)
            w_down_s[lo:lo + MXU_WIDTH, :] = wd_ref[...].astype(BF16)


def _ffn_kernel(x_ref, mod_ref, gain_ref, wg_ref, wu_ref, wd_ref, fgain_ref, o_ref, w_in_s, w_down_s, *,
                sub, d_ff, bounds, final, w_steps):
    step = pl.program_id(0)
    _stage_ffn_weights(step, wg_ref, wu_ref, wd_ref, w_in_s, w_down_s, d_ff)

    @pl.when(step >= w_steps)
    def _():
        o_ref[...] = _ffn_body(x_ref[...], mod_ref, gain_ref, w_in_s, w_down_s, fgain_ref,
                               sub=sub, d_ff=d_ff, bounds=bounds, final=final)


def _ffn_body(x, mod_ref, gain_ref, w_in_ref, w_down_ref, fgain_ref, *, sub, d_ff, bounds, final):
    shift = mod_ref[0, 3 * sub + 0:3 * sub + 1, :]
    scale = mod_ref[0, 3 * sub + 1:3 * sub + 2, :]
    gate = mod_ref[0, 3 * sub + 2:3 * sub + 3, :]
    hb = _norm_mod(x, gain_ref[...], shift, scale).astype(BF16)
    acc = None
    for lo, hi in zip(bounds[:-1], bounds[1:]):
        g = _dot(hb, w_in_ref[:, lo:hi])
        u = _dot(hb, w_in_ref[:, d_ff + lo:d_ff + hi])
        act = (_silu(g) * u).astype(BF16)
        part = _dot(act, w_down_ref[lo:hi, :])
        acc = part if acc is None else acc + part
    xn = x + (0.5 * gate) * acc
    if final:
        xn = xn * lax.rsqrt(jnp.mean(xn * xn, axis=-1, keepdims=True) + EPS) * fgain_ref[...]
    return xn


def _ffn(x2d, mod3, gain, w_in_all, w_down_all, fgain, *, layer, sub, tokens_per_batch, final, tm=TOKEN_TILE):
    n, d = x2d.shape
    d_ff = w_down_all.shape[1]
    tiles_per_batch = tokens_per_batch // tm
    w_steps, w_specs, w_scratch = _ffn_weight_specs(layer, d, d_ff)

    def tile(i):
        return jnp.maximum(i - w_steps, 0)

    return pl.pallas_call(
        functools.partial(_ffn_kernel, sub=sub, d_ff=d_ff, bounds=_mxu_aligned_bounds(d_ff, 2), final=final,
                          w_steps=w_steps),
        grid=(w_steps + n // tm,),
        in_specs=[
            pl.BlockSpec((tm, d), lambda i: (tile(i), 0)),
            pl.BlockSpec((1, 3 * N_SUB, d), lambda i: (tile(i) // tiles_per_batch, 0, 0)),
            pl.BlockSpec((1, d), lambda i: (0, 0)),
            *w_specs,
            pl.BlockSpec((1, d), lambda i: (0, 0)),
        ],
        out_specs=pl.BlockSpec((tm, d), lambda i: (tile(i), 0)),
        out_shape=jax.ShapeDtypeStruct((n, d), F32),
        scratch_shapes=w_scratch,
        compiler_params=pltpu.CompilerParams(dimension_semantics=("arbitrary",), vmem_limit_bytes=VMEM_LIMIT["ffn"]),
        name="ffn_final" if final else "ffn",
    )(x2d, mod3, gain.reshape(1, d), w_in_all, w_in_all, w_down_all, fgain.reshape(1, d))


MIX_W_CHUNK = 512


DEC_OFF = {"g": 0, "gd": 4}
DEC_TILES = 6
ACT_OFF = {"q": 0, "k": 4, "v": 8, "sg": 12, "gq": 16, "gk": 18, "gv": 20, "gsg": 24}
ACT_TILES = 28


def _inproj_kernel(x_ref, mod_ref, gain_ref, wt_ref, lbl_ref, wup_ref, bup_ref, hgn_ref, glan_ref,
                   dec_ref, act_ref, r_ref, stats_ref, wt_s, *, sub, layer, n_cols, w_steps):
    step = pl.program_id(0)
    for j in range(w_steps):
        @pl.when(step == j)
        def _(lo=j * MIX_W_CHUNK):
            w = wt_ref[...]
            if lo + MIX_W_CHUNK > n_cols:
                row = lo + lax.broadcasted_iota(jnp.int32, w.shape, 0)
                w = jnp.where(row < n_cols, w, 0.0)
            wt_s[lo:lo + MIX_W_CHUNK, :] = w.astype(BF16)

    @pl.when(step >= w_steps)
    def _():
        shift = mod_ref[0, 3 * sub + 0:3 * sub + 1, :]
        scale = mod_ref[0, 3 * sub + 1:3 * sub + 2, :]
        hb = _norm_mod(x_ref[...], gain_ref[...], shift, scale).astype(BF16)
        tm = hb.shape[0]
        hg_qk, hg_w = HG_HEADS * HG_DK, HG_HEADS * HG_DV
        gla_k, gla_v = GLA_HEADS * GLA_DK, GLA_HEADS * GLA_DV
        n_hg = 2 * hg_qk + 2 * hg_w
        n_rec = n_hg + 2 * gla_k + 2 * gla_v

        def put(name, val):
            if name in DEC_OFF:
                lo = DEC_OFF[name] * LANES
                dec_ref[:, lo:lo + val.shape[1]] = val
            else:
                lo = ACT_OFF[name] * LANES
                val = val.astype(act_ref.dtype)
                act_ref[:, lo:lo + val.shape[1]] = val
            return val

        def head_gain(norm_ref, heads):
            gain = norm_ref[...]
            return jnp.concatenate([gain] * heads, axis=1) * (gain.shape[1] ** 0.5)

        def chunk_min(g):
            tot = jnp.sum(g.reshape(tm // REC_CHUNK, REC_CHUNK, g.shape[1]), axis=1)
            return functools.reduce(jnp.minimum, [tot[:, j:j + LANES] for j in range(0, g.shape[1], LANES)])

        def mag_max(v):
            rows = 2 * SUBLANES
            m = jnp.max(jnp.abs(v).reshape(tm // rows, rows, v.shape[1]), axis=0)
            m = functools.reduce(jnp.maximum, [m[:, j:j + LANES] for j in range(0, v.shape[1], LANES)]).astype(F32)
            return jnp.maximum(m[:SUBLANES], m[SUBLANES:])

        ph = _dot_nt(hb, wt_s[:n_hg, :])
        lbl = lbl_ref[...]
        ex = jnp.exp(lbl - jnp.max(lbl, axis=0, keepdims=True))
        lb = jnp.sum(ex[:layer + 1], axis=0, keepdims=True) / jnp.sum(ex, axis=0, keepdims=True)
        c1 = 0.5 * (1.0 - lb)
        q = _silu_tanh(ph[:, :hg_qk])
        p = c1 * jnp.tanh(0.5 * ph[:, hg_qk:2 * hg_qk])
        g = jnp.log2((1.0 - c1) + p)
        q_b = put("q", q)
        put("k", c1 - p)
        put("g", g)
        put("v", ph[:, 2 * hg_qk:2 * hg_qk + hg_w])
        put("sg", _silu_tanh(ph[:, 2 * hg_qk + hg_w:]) * head_gain(hgn_ref, HG_HEADS))

        pg = _dot_nt(hb, wt_s[n_hg:n_rec, :])
        code = _dot_nt(hb, wt_s[n_rec:n_rec + LANES, :])
        logits = _dot(code.astype(BF16), wup_ref[...].astype(BF16)) + bup_ref[...]
        gd = (jnp.minimum(logits, 0.0) - jnp.log(1.0 + jnp.exp(-jnp.abs(logits)))) * (LOG2E / GLA_NORMALIZER)
        gq = pg[:, :gla_k] * (GLA_DK ** -0.5)
        gk = pg[:, gla_k:2 * gla_k]
        gq_b = put("gq", gq)
        gk_b = put("gk", gk)
        put("gd", gd)
        put("gv", pg[:, 2 * gla_k:2 * gla_k + gla_v])
        put("gsg", _silu_tanh(pg[:, 2 * gla_k + gla_v:]) * head_gain(glan_ref, GLA_HEADS))

        r_ref[...] = _dot_nt(hb, wt_s[n_rec + GLA_RANK:n_rec + GLA_RANK + r_ref.shape[1], :]).astype(r_ref.dtype)
        stats_ref[0, 0] = jnp.minimum(chunk_min(g), chunk_min(gd))
        stats_ref[0, 1] = functools.reduce(jnp.maximum, [mag_max(q_b), mag_max(gq_b), mag_max(gk_b)])


def _inproj(x2d, mod3, gain, wt_all, lb_logits, wup_pad, bup, hg_norm, gla_norm, *, layer, sub, tokens_per_batch,
            tm=TOKEN_TILE):
    n, d = x2d.shape
    n_cols = wt_all.shape[1]
    n_rec = 2 * HG_HEADS * HG_DK + 2 * HG_HEADS * HG_DV + 2 * GLA_HEADS * GLA_DK + 2 * GLA_HEADS * GLA_DV
    n_r = n_cols - n_rec - GLA_RANK
    w_steps = pl.cdiv(n_cols, MIX_W_CHUNK)
    assert n_rec + LANES <= w_steps * MIX_W_CHUNK and tm // REC_CHUNK == SUBLANES
    tiles_per_batch = tokens_per_batch // tm

    def tile(i):
        return jnp.maximum(i - w_steps, 0)

    return pl.pallas_call(
        functools.partial(_inproj_kernel, sub=sub, layer=layer, n_cols=n_cols, w_steps=w_steps),
        grid=(w_steps + n // tm,),
        in_specs=[
            pl.BlockSpec((tm, d), lambda i: (tile(i), 0)),
            pl.BlockSpec((1, 3 * N_SUB, d), lambda i: (tile(i) // tiles_per_batch, 0, 0)),
            pl.BlockSpec((1, d), lambda i: (0, 0)),
            pl.BlockSpec((None, MIX_W_CHUNK, d), lambda i: (layer, jnp.minimum(i, w_steps - 1), 0)),
            pl.BlockSpec(lb_logits.shape, lambda i: (0, 0)),
            pl.BlockSpec(wup_pad.shape, lambda i: (0, 0)),
            pl.BlockSpec((1, bup.shape[0]), lambda i: (0, 0)),
            pl.BlockSpec((1, HG_DV), lambda i: (0, 0)),
            pl.BlockSpec((1, GLA_DV), lambda i: (0, 0)),
        ],
        out_specs=[
            pl.BlockSpec((tm, DEC_TILES * LANES), lambda i: (tile(i), 0)),
            pl.BlockSpec((tm, ACT_TILES * LANES), lambda i: (tile(i), 0)),
            pl.BlockSpec((tm, n_r), lambda i: (tile(i), 0)),
            pl.BlockSpec((1, 2, SUBLANES, LANES), lambda i: (tile(i), 0, 0, 0)),
        ],
        out_shape=[
            jax.ShapeDtypeStruct((n, DEC_TILES * LANES), F32),
            jax.ShapeDtypeStruct((n, ACT_TILES * LANES), BF16),
            jax.ShapeDtypeStruct((n, n_r), BF16),
            jax.ShapeDtypeStruct((n // tm, 2, SUBLANES, LANES), F32),
        ],
        scratch_shapes=[pltpu.VMEM((w_steps * MIX_W_CHUNK, d), BF16)],
        compiler_params=pltpu.CompilerParams(dimension_semantics=("arbitrary",), vmem_limit_bytes=VMEM_LIMIT["inproj"]),
        name="inproj",
    )(x2d, mod3, gain.reshape(1, d), wt_all, lb_logits, wup_pad, bup.reshape(1, -1), hg_norm.reshape(1, HG_DV),
      gla_norm.reshape(1, GLA_DV))


def _load_perm(ref, base):
    return [ref[pl.ds(base + PERM_GROUP * c + a, SUBLANES, stride=PERM_STRIDE), :]
            for c in range(N_GROUPS) for a in range(PERM_STRIDE)]


def _store_perm(ref, base, val):
    for c in range(N_GROUPS):
        for a in range(PERM_STRIDE):
            u = PERM_STRIDE * c + a
            ref[pl.ds(base + PERM_GROUP * c + a, SUBLANES, stride=PERM_STRIDE), :] = (
                val[SUBLANES * u:SUBLANES * (u + 1), :])


def _shift_down(v, d, sub_iota):
    return jnp.where(sub_iota >= d, pltpu.roll(v, d, 0), 0.0)


def _cumsum_perm(g, sub_iota):
    out = []
    carry = None
    for c in range(N_GROUPS):
        pre = [g[PERM_STRIDE * c]]
        for a in range(1, PERM_STRIDE):
            pre.append(pre[-1] + g[PERM_STRIDE * c + a])
        inc = pre[-1]
        inc = inc + _shift_down(inc, 1, sub_iota)
        inc = inc + _shift_down(inc, 2, sub_iota)
        inc = inc + _shift_down(inc, 4, sub_iota)
        exc = _shift_down(inc, 1, sub_iota)
        if carry is not None:
            exc = exc + carry
        out.extend(p + exc for p in pre)
        tot = inc[SUBLANES - 1:SUBLANES, :]
        carry = tot if carry is None else carry + tot
    return jnp.concatenate(out, axis=0), carry


def _silu_tanh(v):
    h = 0.5 * v
    return h * jnp.tanh(h) + h


def _finish(o, gate_b):
    r = lax.rsqrt(jnp.sum(o * o, axis=-1, keepdims=True) + o.shape[-1] * EPS)
    return (o * r).astype(BF16) * gate_b


def _rec_kernel(safe_ref, q0, q1, k0, k1, g0, g1, v0, v1, sg0, sg1, gq, gk, gd, gv0, gv1, gsg0, gsg1,
                yh_ref, yg_ref, st_s, e_s, fb_s, oi_s, *, tt, unroll):
    nc = tt // REC_CHUNK
    nb = q0.shape[0]

    @pl.when(pl.program_id(1) == 0)
    def _():
        st_s[...] = jnp.zeros_like(st_s)

    safe = safe_ref[pl.program_id(1)] != 0

    sub_iota = lax.broadcasted_iota(jnp.int32, (SUBLANES, LANES), 0)
    lane = lax.broadcasted_iota(jnp.int32, (REC_CHUNK, LANES), 1)
    causal = (lax.broadcasted_iota(jnp.int32, (REC_CHUNK, REC_CHUNK), 1)
              <= lax.broadcasted_iota(jnp.int32, (REC_CHUNK, REC_CHUNK), 0))
    tok_of_row = lax.broadcasted_iota(jnp.int32, (REC_CHUNK, LANES), 0)

    head_masks = (lane < GLA_DK, lane >= GLA_DK)
    groups = []
    for bi in range(nb):
        groups += [
            ((q0.at[bi], k0.at[bi], g0.at[bi]), st_s.at[bi, 0],
             [(None, v0.at[bi], sg0.at[bi], yh_ref.at[0, bi])]),
            ((q1.at[bi], k1.at[bi], g1.at[bi]), st_s.at[bi, 1],
             [(None, v1.at[bi], sg1.at[bi], yh_ref.at[1, bi])]),
            ((gq.at[bi], gk.at[bi], gd.at[bi]), st_s.at[bi, 2],
             [(head_masks[0], gv0.at[bi], gsg0.at[bi], yg_ref.at[0, bi]),
              (head_masks[1], gv1.at[bi], gsg1.at[bi], yg_ref.at[1, bi])]),
        ]

    def natural(ref, base):
        return ref[pl.ds(base, REC_CHUNK), :]

    def decay_logs(g_ref, base):
        return _cumsum_perm(_load_perm(g_ref, base), sub_iota)

    def to_natural(slot, val):
        _store_perm(e_s.at[slot], 0, val)
        return e_s[slot]

    def masked(x, mask):
        return x if mask is None else jnp.where(mask, x, jnp.zeros_like(x))

    def fast_chunk(ci, carry):
        staged = []
        for sub in range(unroll):
            base = pl.multiple_of((ci * unroll + sub) * REC_CHUNK, REC_CHUNK)
            for gi, ((q_ref, k_ref, g_ref), st_view, heads) in enumerate(groups):
                b, b_last = decay_logs(g_ref, base)
                half = 0.5 * b_last
                d = b - half
                em = jnp.exp2(half)
                slot = 2 * (sub * len(groups) + gi)
                qt_b = (natural(q_ref, base).astype(F32) * to_natural(slot, jnp.exp2(d))).astype(BF16)
                kt_b = (natural(k_ref, base).astype(F32) * to_natural(slot + 1, jnp.exp2(-d))).astype(BF16)
                vbs = [natural(v_ref, base) for _, v_ref, _, _ in heads]
                q_all = jnp.concatenate([masked(qt_b, m) for m, *_ in heads], axis=0)
                k_all = jnp.concatenate([masked(kt_b, m) for m, *_ in heads], axis=0)
                v_all = jnp.concatenate(vbs, axis=0)
                state_e = st_view[...] * em
                scores = _dot_nt(q_all, kt_b)
                o_inter = _dot_nt(q_all, state_e.astype(BF16))
                st_view[...] = (state_e + _dot_tn(v_all, k_all)) * em
                for h, (_, _, gate_ref, y_view) in enumerate(heads):
                    rows = slice(h * REC_CHUNK, (h + 1) * REC_CHUNK)
                    staged.append((scores[rows], o_inter[rows], vbs[h], base, gate_ref, y_view))
        for scores, o_inter, vb, base, gate_ref, y_view in staged:
            s = jnp.where(causal, scores, 0.0).astype(BF16)
            o = o_inter + _dot(s, vb)
            y_view[pl.ds(base, REC_CHUNK), :] = _finish(o, natural(gate_ref, base))
        return carry

    def exact_chunk(ci, carry):
        base = pl.multiple_of(ci * REC_CHUNK, REC_CHUNK)
        for (q_ref, k_ref, g_ref), st_view, heads in groups:
            b_perm, b_last = decay_logs(g_ref, base)
            b = to_natural(0, b_perm)
            q_all = natural(q_ref, base).astype(F32)
            k = natural(k_ref, base).astype(F32)
            state = st_view[...]
            new_state = state * jnp.exp2(b_last)
            kd = k * jnp.exp2(b_last - b)
            for qmask, v_ref, gate_ref, y_view in heads:
                q = masked(q_all, qmask)
                vb = natural(v_ref, base)
                fb_s[0] = b
                fb_s[1] = q
                fb_s[2] = k
                fb_s[3] = vb.astype(F32)

                def row(r, c2):
                    bt = fb_s[0, pl.ds(r, 1), :]
                    qt = fb_s[1, pl.ds(r, 1), :]
                    w = jnp.exp2(jnp.where(tok_of_row <= r, bt - fb_s[0], -jnp.inf))
                    sc = jnp.sum(qt * w * fb_s[2], axis=-1, keepdims=True)
                    oi_s[pl.ds(r, 1), :] = jnp.sum(sc * fb_s[3], axis=0, keepdims=True)
                    return c2

                lax.fori_loop(0, REC_CHUNK, row, 0)
                o = _dot_nt((q * jnp.exp2(b)).astype(BF16), state.astype(BF16)) + oi_s[...]
                new_state = new_state + _dot_tn(vb, masked(kd, qmask).astype(BF16))
                y_view[pl.ds(base, REC_CHUNK), :] = _finish(o, natural(gate_ref, base))
            st_view[...] = new_state
        return carry

    @pl.when(safe)
    def _():
        lax.fori_loop(0, nc // unroll, fast_chunk, 0)

    @pl.when(jnp.logical_not(safe))
    def _():
        lax.fori_loop(0, nc, exact_chunk, 0)


def _safe_flags(stats, *, batch, tokens_per_batch, tt):
    tiles_per_batch = stats.shape[0] // batch
    chunk_tot = jnp.min(stats[:, 0], axis=-1).reshape(batch, tokens_per_batch // tt, -1)
    tile_mag = jnp.max(stats[:, 1], axis=(-2, -1)).reshape(batch, tiles_per_batch)
    step_tot = jnp.min(chunk_tot, axis=(0, 2))
    step_mag = jnp.repeat(jnp.max(tile_mag, axis=0), (tokens_per_batch // tt) // tiles_per_batch)
    return jnp.logical_and(step_tot >= -2.0 * SAFE_EXP2, step_mag <= SAFE_MAG).astype(jnp.int32)


def _recurrence(dec, act, stats, *, batch, tokens_per_batch, tt=REC_TILE):
    n = dec.shape[0]
    nt = tokens_per_batch // tt
    pairs = HG_HEADS // 2
    dec3 = dec.reshape(batch, tokens_per_batch, dec.shape[1])
    act3 = act.reshape(batch, tokens_per_batch, act.shape[1])
    safe = _safe_flags(stats, batch=batch, tokens_per_batch=tokens_per_batch, tt=tt)

    def tile_of(name, p, e):
        off = DEC_OFF[name] if name in DEC_OFF else ACT_OFF[name]
        return off + (p if e is None else 2 * p + e)

    def spec(name, e=None):
        return pl.BlockSpec((batch, tt, LANES), lambda p, i, flags: (0, i, tile_of(name, p, e)))

    names = [("q", 0), ("q", 1), ("k", 0), ("k", 1), ("g", 0), ("g", 1), ("v", 0), ("v", 1), ("sg", 0), ("sg", 1),
             ("gq", None), ("gk", None), ("gd", None), ("gv", 0), ("gv", 1), ("gsg", 0), ("gsg", 1)]
    in_specs = [spec(name, e) for name, e in names]
    operands = [dec3 if name in DEC_OFF else act3 for name, _ in names]
    n_groups = 3 * batch
    out_spec = pl.BlockSpec((2, batch, tt, LANES), lambda p, i, flags: (p, 0, i, 0))
    y_hg, y_gla = pl.pallas_call(
        functools.partial(_rec_kernel, tt=tt, unroll=REC_UNROLL),
        grid_spec=pltpu.PrefetchScalarGridSpec(
            num_scalar_prefetch=1,
            grid=(pairs, nt),
            in_specs=in_specs,
            out_specs=[out_spec, out_spec],
            scratch_shapes=[
                pltpu.VMEM((batch, 3, LANES, LANES), F32),
                pltpu.VMEM((2 * REC_UNROLL * n_groups, REC_CHUNK, LANES), F32),
                pltpu.VMEM((4, REC_CHUNK, LANES), F32),
                pltpu.VMEM((REC_CHUNK, LANES), F32),
            ],
        ),
        out_shape=[jax.ShapeDtypeStruct((HG_HEADS, batch, tokens_per_batch, HG_DV), BF16),
                   jax.ShapeDtypeStruct((GLA_HEADS, batch, tokens_per_batch, GLA_DV), BF16)],
        compiler_params=pltpu.CompilerParams(
            dimension_semantics=("arbitrary", "arbitrary"), vmem_limit_bytes=VMEM_LIMIT["recurrence"]),
        name="recurrence",
    )(safe, *operands)
    return y_hg.reshape(HG_HEADS, n, HG_DV), y_gla.reshape(GLA_HEADS, n, GLA_DV)


def _merge_ffn_kernel(x_ref, mod_ref, yh_ref, yg_ref, rh_ref, rg_ref, wuh_ref, wug_ref, wo_ref,
                      gain_ref, wg_ref, wu_ref, wd_ref, fgain_ref, o_ref, w_in_s, w_down_s, *,
                      sub_mix, sub_ffn, d_ff, bounds, final, w_steps):
    step = pl.program_id(0)
    _stage_ffn_weights(step, wg_ref, wu_ref, wd_ref, w_in_s, w_down_s, d_ff)

    @pl.when(step >= w_steps)
    def _():
        gate = mod_ref[0, 3 * sub_mix + 2:3 * sub_mix + 3, :]

        def up(y_ref, w_ref):
            y = jnp.concatenate([y_ref[h].astype(BF16) for h in range(y_ref.shape[0])], axis=-1)
            return _dot(y, w_ref[...])

        merged = (jax.nn.sigmoid(rh_ref[...].astype(F32)) * up(yh_ref, wuh_ref)
                  + jax.nn.sigmoid(rg_ref[...].astype(F32)) * up(yg_ref, wug_ref))
        x = x_ref[...] + gate * _dot(merged.astype(BF16), wo_ref[...])
        o_ref[...] = _ffn_body(x, mod_ref, gain_ref, w_in_s, w_down_s, fgain_ref,
                               sub=sub_ffn, d_ff=d_ff, bounds=bounds, final=final)


def _merge_ffn(x2d, mod3, y_hg, y_gla, r, w_up_hg, w_up_gla, w_out, gain, w_in_all, w_down_all, fgain, *,
               layer, sub_mix, sub_ffn, tokens_per_batch, final, tm=TOKEN_TILE):
    n, d = x2d.shape
    d_ff = w_down_all.shape[1]
    tiles_per_batch = tokens_per_batch // tm
    w_steps, w_specs, w_scratch = _ffn_weight_specs(layer, d, d_ff)
    const = dict(pipeline_mode=pl.Buffered(1))

    def tile(i):
        return jnp.maximum(i - w_steps, 0)

    return pl.pallas_call(
        functools.partial(_merge_ffn_kernel, sub_mix=sub_mix, sub_ffn=sub_ffn, d_ff=d_ff,
                          bounds=_mxu_aligned_bounds(d_ff, 2), final=final, w_steps=w_steps),
        grid=(w_steps + n // tm,),
        in_specs=[
            pl.BlockSpec((tm, d), lambda i: (tile(i), 0)),
            pl.BlockSpec((1, 3 * N_SUB, d), lambda i: (tile(i) // tiles_per_batch, 0, 0)),
            pl.BlockSpec((HG_HEADS, tm, HG_DV), lambda i: (0, tile(i), 0)),
            pl.BlockSpec((GLA_HEADS, tm, GLA_DV), lambda i: (0, tile(i), 0)),
            pl.BlockSpec((tm, d), lambda i: (tile(i), 0)),
            pl.BlockSpec((tm, d), lambda i: (tile(i), 1)),
            pl.BlockSpec(w_up_hg.shape, lambda i: (0, 0), **const),
            pl.BlockSpec(w_up_gla.shape, lambda i: (0, 0), **const),
            pl.BlockSpec(w_out.shape, lambda i: (0, 0), **const),
            pl.BlockSpec((1, d), lambda i: (0, 0)),
            *w_specs,
            pl.BlockSpec((1, d), lambda i: (0, 0)),
        ],
        out_specs=pl.BlockSpec((tm, d), lambda i: (tile(i), 0)),
        out_shape=jax.ShapeDtypeStruct((n, d), F32),
        scratch_shapes=w_scratch,
        compiler_params=pltpu.CompilerParams(dimension_semantics=("arbitrary",), vmem_limit_bytes=VMEM_LIMIT["merge_ffn"]),
        name="merge_ffn",
    )(x2d, mod3, y_hg, y_gla, r, r, w_up_hg, w_up_gla, w_out, gain.reshape(1, d), w_in_all, w_in_all,
      w_down_all, fgain.reshape(1, d))


def kernel(x, c, w_ada, b_ada, norm_gains, ffn1_w_in, ffn1_w_down, w_in_mix, w_gk_up, b_gk_up, lb_logits,
           hg_norm, gla_norm, w_up_hg, w_up_gla, w_out, ffn2_w_in, ffn2_w_down, final_norm):
    batch, seq, d = x.shape
    depth = w_ada.shape[0]

    x2d = x.reshape(batch * seq, d)
    c_pad = jnp.pad(c, ((0, SUBLANES - batch % SUBLANES if batch % SUBLANES else 0), (0, 0)))
    for l in range(depth):
        mod = _ada(c_pad, w_ada[l], b_ada[l])[:batch].reshape(batch, 3 * N_SUB, d)

        x2d = _ffn(x2d, mod, norm_gains[l, 0], ffn1_w_in, ffn1_w_down, final_norm, layer=l, sub=0,
                   tokens_per_batch=seq, final=False)

        wup_pad = jnp.pad(w_gk_up[l], ((0, LANES - GLA_RANK), (0, 0)))
        dec, act, r, stats = _inproj(x2d, mod, norm_gains[l, 1], jnp.swapaxes(w_in_mix, 1, 2), lb_logits, wup_pad,
                                     b_gk_up[l], hg_norm[l], gla_norm[l], layer=l, sub=1, tokens_per_batch=seq)

        y_hg, y_gla = _recurrence(dec, act, stats, batch=batch, tokens_per_batch=seq)

        x2d = _merge_ffn(x2d, mod, y_hg, y_gla, r, w_up_hg[l].astype(BF16), w_up_gla[l].astype(BF16),
                         w_out[l].astype(BF16), norm_gains[l, 2], ffn2_w_in, ffn2_w_down, final_norm, layer=l,
                         sub_mix=1, sub_ffn=2, tokens_per_batch=seq, final=l == depth - 1)
    return x2d.reshape(batch, seq, d)
```

```python
import functools

import jax
import jax.numpy as jnp
from jax import lax
from jax.experimental import pallas as pl
from jax.experimental.pallas import tpu as pltpu

F32 = jnp.float32
BF16 = jnp.bfloat16

EPS = 1e-6
N_SUB = 3
HG_HEADS = 4
HG_DK = 128
HG_DV = 128
GLA_HEADS = 4
GLA_DK = 64
GLA_DV = 128
GLA_RANK = 16
GLA_NORMALIZER = 16.0

LANES = 128
SUBLANES = 8
MXU_WIDTH = 256
MIB = 1024 * 1024

TOKEN_TILE = 512
REC_TILE = 512
ADA_STEPS = 4
VMEM_LIMIT = {"ffn": 52 * MIB, "inproj": 56 * MIB, "recurrence": 48 * MIB, "merge_ffn": 58 * MIB}

REC_CHUNK = 64
PERM_STRIDE = 4
PERM_GROUP = PERM_STRIDE * SUBLANES
N_GROUPS = REC_CHUNK // PERM_GROUP
REC_UNROLL = 8
LOG2E = 1.4426950408889634
SAFE_EXP2 = 100.0
SAFE_MAG = 1e7


def _dot(a, b):
    return jnp.dot(a, b, preferred_element_type=F32)


def _dot_nt(a, b):
    return lax.dot_general(a, b, (((1,), (1,)), ((), ())), preferred_element_type=F32)


def _dot_tn(a, b):
    return lax.dot_general(a, b, (((0,), (0,)), ((), ())), preferred_element_type=F32)


def _silu(v):
    return v * jax.nn.sigmoid(v)


def _norm_mod(x, gain, shift, scale):
    y = x * lax.rsqrt(jnp.mean(x * x, axis=-1, keepdims=True) + EPS) * gain
    return y * (1.0 + scale) + shift


def _ada_kernel(c_ref, w_ref, b_ref, o_ref):
    cond = _silu(c_ref[...]).astype(BF16)
    o_ref[...] = _dot(cond, w_ref[...].astype(BF16)) + b_ref[...]


def _ada(c_pad, w, b):
    rows, d = c_pad.shape
    n = w.shape[1]
    bn = n // ADA_STEPS
    return pl.pallas_call(
        _ada_kernel,
        grid=(n // bn,),
        in_specs=[
            pl.BlockSpec((rows, d), lambda j: (0, 0)),
            pl.BlockSpec((d, bn), lambda j: (0, j)),
            pl.BlockSpec((1, bn), lambda j: (0, j)),
        ],
        out_specs=pl.BlockSpec((rows, bn), lambda j: (0, j)),
        out_shape=jax.ShapeDtypeStruct((rows, n), F32),
        name="adaln",
    )(c_pad, w, b.reshape(1, n))


def _mxu_aligned_bounds(width, parts):
    if width % MXU_WIDTH:
        return (0, width)
    tiles = width // MXU_WIDTH
    return tuple(MXU_WIDTH * ((tiles * p + parts - 1) // parts) for p in range(parts)) + (width,)


def _ffn_weight_specs(layer, d, d_ff):
    assert d_ff % MXU_WIDTH == 0
    steps = d_ff // MXU_WIDTH

    def chunk(i):
        return jnp.minimum(i, steps - 1)

    specs = [
        pl.BlockSpec((None, d, MXU_WIDTH), lambda i: (layer, 0, chunk(i))),
        pl.BlockSpec((None, d, MXU_WIDTH), lambda i: (layer, 0, steps + chunk(i))),
        pl.BlockSpec((None, MXU_WIDTH, d), lambda i: (layer, chunk(i), 0)),
    ]
    scratch = [pltpu.VMEM((d, 2 * d_ff), BF16), pltpu.VMEM((d_ff, d), BF16)]
    return steps, specs, scratch


def _stage_ffn_weights(step, wg_ref, wu_ref, wd_ref, w_in_s, w_down_s, d_ff):
    for j in range(d_ff // MXU_WIDTH):
        @pl.when(step == j)
        def _(lo=j * MXU_WIDTH):
            w_in_s[:, lo:lo + MXU_WIDTH] = wg_ref[...].astype(BF16)
            w_in_s[:, d_ff + lo:d_ff + lo + MXU_WIDTH] = wu_ref[...].astype(BF16)
            w_down_s[lo:lo + MXU_WIDTH, :] = wd_ref[...].astype(BF16)


class _SideCast:
    def __init__(self, weight_all, layer, n_tiles):
        self.weight_all, self.layer = weight_all, layer
        self.valid_rows, self.cols = weight_all.shape[1:]
        self.rows = -(-(-(-self.valid_rows // n_tiles)) // 16) * 16
        self.n_chunks = -(-self.valid_rows // self.rows)

    def chunk(self, tile):
        return jnp.minimum(tile, self.n_chunks - 1)

    def in_spec(self, tile_of_step):
        return pl.BlockSpec((None, self.rows, self.cols), lambda i: (self.layer, self.chunk(tile_of_step(i)), 0))

    def out_spec(self, tile_of_step):
        return pl.BlockSpec((self.rows, self.cols), lambda i: (self.chunk(tile_of_step(i)), 0))

    def out_shape(self):
        return jax.ShapeDtypeStruct((self.n_chunks * self.rows, self.cols), BF16)

    def cast(self, tile, in_ref, out_ref):
        w = in_ref[...]
        if self.n_chunks * self.rows > self.valid_rows:
            row = self.chunk(tile) * self.rows + lax.broadcasted_iota(jnp.int32, w.shape, 0)
            w = jnp.where(row < self.valid_rows, w, 0.0)
        out_ref[...] = w.astype(BF16)


def _ffn_kernel(*refs, sub, d_ff, bounds, final, w_steps, sides):
    x_ref, mod_ref, gain_ref, wg_ref, wu_ref, wd_ref, fgain_ref = refs[:7]
    side_in = refs[7:7 + len(sides)]
    o_ref = refs[7 + len(sides)]
    side_out = refs[8 + len(sides):8 + 2 * len(sides)]
    w_in_s, w_down_s = refs[8 + 2 * len(sides):]
    step = pl.program_id(0)
    _stage_ffn_weights(step, wg_ref, wu_ref, wd_ref, w_in_s, w_down_s, d_ff)

    @pl.when(step >= w_steps)
    def _():
        o_ref[...] = _ffn_body(x_ref[...], mod_ref, gain_ref, w_in_s, w_down_s, fgain_ref,
                               sub=sub, d_ff=d_ff, bounds=bounds, final=final)
        for side, in_ref, out_ref in zip(sides, side_in, side_out):
            side.cast(step - w_steps, in_ref, out_ref)


def _ffn_body(x, mod_ref, gain_ref, w_in_ref, w_down_ref, fgain_ref, *, sub, d_ff, bounds, final):
    shift = mod_ref[0, 3 * sub + 0:3 * sub + 1, :]
    scale = mod_ref[0, 3 * sub + 1:3 * sub + 2, :]
    gate = mod_ref[0, 3 * sub + 2:3 * sub + 3, :]
    hb = _norm_mod(x, gain_ref[...], shift, scale).astype(BF16)
    acc = None
    for lo, hi in zip(bounds[:-1], bounds[1:]):
        g = _dot(hb, w_in_ref[:, lo:hi])
        u = _dot(hb, w_in_ref[:, d_ff + lo:d_ff + hi])
        act = (_silu(g) * u).astype(BF16)
        part = _dot(act, w_down_ref[lo:hi, :])
        acc = part if acc is None else acc + part
    xn = x + (0.5 * gate) * acc
    if final:
        xn = xn * lax.rsqrt(jnp.mean(xn * xn, axis=-1, keepdims=True) + EPS) * fgain_ref[...]
    return xn


def _ffn(x2d, mod3, gain, w_in_all, w_down_all, fgain, side_weights, *, layer, sub, tokens_per_batch, final,
         tm=TOKEN_TILE):
    n, d = x2d.shape
    d_ff = w_down_all.shape[1]
    tiles_per_batch = tokens_per_batch // tm
    w_steps, w_specs, w_scratch = _ffn_weight_specs(layer, d, d_ff)
    sides = [_SideCast(w, layer, n // tm) for w in side_weights]

    def tile(i):
        return jnp.maximum(i - w_steps, 0)

    outs = pl.pallas_call(
        functools.partial(_ffn_kernel, sub=sub, d_ff=d_ff, bounds=_mxu_aligned_bounds(d_ff, 2), final=final,
                          w_steps=w_steps, sides=sides),
        grid=(w_steps + n // tm,),
        in_specs=[
            pl.BlockSpec((tm, d), lambda i: (tile(i), 0)),
            pl.BlockSpec((1, 3 * N_SUB, d), lambda i: (tile(i) // tiles_per_batch, 0, 0)),
            pl.BlockSpec((1, d), lambda i: (0, 0)),
            *w_specs,
            pl.BlockSpec((1, d), lambda i: (0, 0)),
            *[s.in_spec(tile) for s in sides],
        ],
        out_specs=[pl.BlockSpec((tm, d), lambda i: (tile(i), 0)), *[s.out_spec(tile) for s in sides]],
        out_shape=[jax.ShapeDtypeStruct((n, d), F32), *[s.out_shape() for s in sides]],
        scratch_shapes=w_scratch,
        compiler_params=pltpu.CompilerParams(dimension_semantics=("arbitrary",), vmem_limit_bytes=VMEM_LIMIT["ffn"]),
        name="ffn_final" if final else "ffn",
    )(x2d, mod3, gain.reshape(1, d), w_in_all, w_in_all, w_down_all, fgain.reshape(1, d), *side_weights)
    return outs[0], outs[1:]


MIX_W_CHUNK = 512


DEC_OFF = {"g": 0, "gd": 4}
DEC_TILES = 6
ACT_OFF = {"q": 0, "k": 4, "v": 8, "sg": 12, "gq": 16, "gk": 18, "gv": 20, "gsg": 24}
ACT_TILES = 28


def _inproj_kernel(x_ref, mod_ref, gain_ref, wt_s, lbl_ref, wup_ref, bup_ref, dec_ref, act_ref, r_ref, stats_ref, *,
                   sub, layer):
    if True:
        shift = mod_ref[0, 3 * sub + 0:3 * sub + 1, :]
        scale = mod_ref[0, 3 * sub + 1:3 * sub + 2, :]
        hb = _norm_mod(x_ref[...], gain_ref[...], shift, scale).astype(BF16)
        tm = hb.shape[0]
        hg_qk, hg_w = HG_HEADS * HG_DK, HG_HEADS * HG_DV
        gla_k, gla_v = GLA_HEADS * GLA_DK, GLA_HEADS * GLA_DV
        n_hg = 2 * hg_qk + 2 * hg_w
        n_rec = n_hg + 2 * gla_k + 2 * gla_v

        def put(name, val):
            if name in DEC_OFF:
                lo = DEC_OFF[name] * LANES
                dec_ref[:, lo:lo + val.shape[1]] = val
            else:
                lo = ACT_OFF[name] * LANES
                val = val.astype(act_ref.dtype)
                act_ref[:, lo:lo + val.shape[1]] = val
            return val

        def chunk_min(g):
            tot = jnp.sum(g.reshape(tm // REC_CHUNK, REC_CHUNK, g.shape[1]), axis=1)
            return functools.reduce(jnp.minimum, [tot[:, j:j + LANES] for j in range(0, g.shape[1], LANES)])

        def mag_max(v):
            rows = 2 * SUBLANES
            m = jnp.max(jnp.abs(v).reshape(tm // rows, rows, v.shape[1]), axis=0)
            m = functools.reduce(jnp.maximum, [m[:, j:j + LANES] for j in range(0, v.shape[1], LANES)]).astype(F32)
            return jnp.maximum(m[:SUBLANES], m[SUBLANES:])

        ph = _dot_nt(hb, wt_s[:n_hg, :])
        lbl = lbl_ref[...]
        ex = jnp.exp(lbl - jnp.max(lbl, axis=0, keepdims=True))
        lb = jnp.sum(ex[:layer + 1], axis=0, keepdims=True) / jnp.sum(ex, axis=0, keepdims=True)
        c1 = 0.5 * (1.0 - lb)
        q = _silu_tanh(ph[:, :hg_qk])
        p = c1 * jnp.tanh(0.5 * ph[:, hg_qk:2 * hg_qk])
        g = jnp.log2((1.0 - c1) + p)
        q_b = put("q", q)
        put("k", c1 - p)
        put("g", g)
        put("v", ph[:, 2 * hg_qk:2 * hg_qk + hg_w])
        put("sg", _silu_tanh(ph[:, 2 * hg_qk + hg_w:]))

        pg = _dot_nt(hb, wt_s[n_hg:n_rec, :])
        code = _dot_nt(hb, wt_s[n_rec:n_rec + LANES, :])
        logits = _dot(code.astype(BF16), wup_ref[...].astype(BF16)) + bup_ref[...]
        gd = (jnp.minimum(logits, 0.0) - jnp.log(1.0 + jnp.exp(-jnp.abs(logits)))) * (LOG2E / GLA_NORMALIZER)
        gq = pg[:, :gla_k] * (GLA_DK ** -0.5)
        gk = pg[:, gla_k:2 * gla_k]
        gq_b = put("gq", gq)
        gk_b = put("gk", gk)
        put("gd", gd)
        put("gv", pg[:, 2 * gla_k:2 * gla_k + gla_v])
        put("gsg", _silu_tanh(pg[:, 2 * gla_k + gla_v:]))

        r_ref[...] = _dot_nt(hb, wt_s[n_rec + GLA_RANK:n_rec + GLA_RANK + r_ref.shape[1], :]).astype(r_ref.dtype)
        stats_ref[0, 0] = jnp.minimum(chunk_min(g), chunk_min(gd))
        stats_ref[0, 1] = functools.reduce(jnp.maximum, [mag_max(q_b), mag_max(gq_b), mag_max(gk_b)])


def _inproj(x2d, mod3, gain, wt_b, n_channels, lb_logits, wup_pad, bup, *, layer, sub, tokens_per_batch,
            tm=TOKEN_TILE):
    n, d = x2d.shape
    n_rec = 2 * HG_HEADS * HG_DK + 2 * HG_HEADS * HG_DV + 2 * GLA_HEADS * GLA_DK + 2 * GLA_HEADS * GLA_DV
    n_r = n_channels - n_rec - GLA_RANK
    assert n_rec + LANES <= wt_b.shape[0] and tm // REC_CHUNK == SUBLANES
    tiles_per_batch = tokens_per_batch // tm

    return pl.pallas_call(
        functools.partial(_inproj_kernel, sub=sub, layer=layer),
        grid=(n // tm,),
        in_specs=[
            pl.BlockSpec((tm, d), lambda i: (i, 0)),
            pl.BlockSpec((1, 3 * N_SUB, d), lambda i: (i // tiles_per_batch, 0, 0)),
            pl.BlockSpec((1, d), lambda i: (0, 0)),
            pl.BlockSpec(wt_b.shape, lambda i: (0, 0), pipeline_mode=pl.Buffered(1)),
            pl.BlockSpec(lb_logits.shape, lambda i: (0, 0)),
            pl.BlockSpec(wup_pad.shape, lambda i: (0, 0)),
            pl.BlockSpec((1, bup.shape[0]), lambda i: (0, 0)),
        ],
        out_specs=[
            pl.BlockSpec((tm, DEC_TILES * LANES), lambda i: (i, 0)),
            pl.BlockSpec((tm, ACT_TILES * LANES), lambda i: (i, 0)),
            pl.BlockSpec((tm, n_r), lambda i: (i, 0)),
            pl.BlockSpec((1, 2, SUBLANES, LANES), lambda i: (i, 0, 0, 0)),
        ],
        out_shape=[
            jax.ShapeDtypeStruct((n, DEC_TILES * LANES), F32),
            jax.ShapeDtypeStruct((n, ACT_TILES * LANES), BF16),
            jax.ShapeDtypeStruct((n, n_r), BF16),
            jax.ShapeDtypeStruct((n // tm, 2, SUBLANES, LANES), F32),
        ],
        compiler_params=pltpu.CompilerParams(dimension_semantics=("arbitrary",), vmem_limit_bytes=VMEM_LIMIT["inproj"]),
        name="inproj",
    )(x2d, mod3, gain.reshape(1, d), wt_b, lb_logits, wup_pad, bup.reshape(1, -1))


def _load_perm(ref, base):
    return [ref[pl.ds(base + PERM_GROUP * c + a, SUBLANES, stride=PERM_STRIDE), :]
            for c in range(N_GROUPS) for a in range(PERM_STRIDE)]


def _store_perm(ref, base, val):
    for c in range(N_GROUPS):
        for a in range(PERM_STRIDE):
            u = PERM_STRIDE * c + a
            ref[pl.ds(base + PERM_GROUP * c + a, SUBLANES, stride=PERM_STRIDE), :] = (
                val[SUBLANES * u:SUBLANES * (u + 1), :])


def _shift_down(v, d, sub_iota):
    return jnp.where(sub_iota >= d, pltpu.roll(v, d, 0), 0.0)


def _cumsum_perm(g, sub_iota):
    out = []
    carry = None
    for c in range(N_GROUPS):
        pre = [g[PERM_STRIDE * c]]
        for a in range(1, PERM_STRIDE):
            pre.append(pre[-1] + g[PERM_STRIDE * c + a])
        inc = pre[-1]
        inc = inc + _shift_down(inc, 1, sub_iota)
        inc = inc + _shift_down(inc, 2, sub_iota)
        inc = inc + _shift_down(inc, 4, sub_iota)
        exc = _shift_down(inc, 1, sub_iota)
        if carry is not None:
            exc = exc + carry
        out.extend(p + exc for p in pre)
        tot = inc[SUBLANES - 1:SUBLANES, :]
        carry = tot if carry is None else carry + tot
    return jnp.concatenate(out, axis=0), carry


def _silu_tanh(v):
    h = 0.5 * v
    return h * jnp.tanh(h) + h


def _finish(o, gain, act_gate):
    return o * lax.rsqrt(jnp.mean(o * o, axis=-1, keepdims=True) + EPS) * gain * act_gate


def _rec_kernel(safe_ref, q0, q1, k0, k1, g0, g1, v0, v1, sg0, sg1, gq, gk, gd, gv0, gv1, gsg0, gsg1,
                hgn_ref, glan_ref, yh_ref, yg_ref, st_s, e_s, fb_s, oi_s, *, tt, unroll):
    nc = tt // REC_CHUNK
    nb = q0.shape[0]

    @pl.when(pl.program_id(1) == 0)
    def _():
        st_s[...] = jnp.zeros_like(st_s)

    safe = safe_ref[pl.program_id(1)] != 0

    sub_iota = lax.broadcasted_iota(jnp.int32, (SUBLANES, LANES), 0)
    lane = lax.broadcasted_iota(jnp.int32, (REC_CHUNK, LANES), 1)
    causal = (lax.broadcasted_iota(jnp.int32, (REC_CHUNK, REC_CHUNK), 1)
              <= lax.broadcasted_iota(jnp.int32, (REC_CHUNK, REC_CHUNK), 0))
    tok_of_row = lax.broadcasted_iota(jnp.int32, (REC_CHUNK, LANES), 0)
    hgn = hgn_ref[...]
    glan = glan_ref[...]

    head_masks = (lane < GLA_DK, lane >= GLA_DK)
    groups = []
    for bi in range(nb):
        groups += [
            ((q0.at[bi], k0.at[bi], g0.at[bi]), st_s.at[bi, 0],
             [(None, v0.at[bi], sg0.at[bi], hgn, yh_ref.at[0, bi])]),
            ((q1.at[bi], k1.at[bi], g1.at[bi]), st_s.at[bi, 1],
             [(None, v1.at[bi], sg1.at[bi], hgn, yh_ref.at[1, bi])]),
            ((gq.at[bi], gk.at[bi], gd.at[bi]), st_s.at[bi, 2],
             [(head_masks[0], gv0.at[bi], gsg0.at[bi], glan, yg_ref.at[0, bi]),
              (head_masks[1], gv1.at[bi], gsg1.at[bi], glan, yg_ref.at[1, bi])]),
        ]

    def natural(ref, base):
        return ref[pl.ds(base, REC_CHUNK), :]

    def decay_logs(g_ref, base):
        return _cumsum_perm(_load_perm(g_ref, base), sub_iota)

    def to_natural(slot, val):
        _store_perm(e_s.at[slot], 0, val)
        return e_s[slot]

    def masked(x, mask):
        return x if mask is None else jnp.where(mask, x, jnp.zeros_like(x))

    def fast_chunk(ci, carry):
        staged = []
        for sub in range(unroll):
            base = pl.multiple_of((ci * unroll + sub) * REC_CHUNK, REC_CHUNK)
            for gi, ((q_ref, k_ref, g_ref), st_view, heads) in enumerate(groups):
                b, b_last = decay_logs(g_ref, base)
                half = 0.5 * b_last
                d = b - half
                em = jnp.exp2(half)
                slot = 2 * (sub * len(groups) + gi)
                qt_b = (natural(q_ref, base).astype(F32) * to_natural(slot, jnp.exp2(d))).astype(BF16)
                kt_b = (natural(k_ref, base).astype(F32) * to_natural(slot + 1, jnp.exp2(-d))).astype(BF16)
                vbs = [natural(v_ref, base) for _, v_ref, _, _, _ in heads]
                q_all = jnp.concatenate([masked(qt_b, m) for m, *_ in heads], axis=0)
                k_all = jnp.concatenate([masked(kt_b, m) for m, *_ in heads], axis=0)
                v_all = jnp.concatenate(vbs, axis=0)
                state_e = st_view[...] * em
                scores = _dot_nt(q_all, kt_b)
                o_inter = _dot_nt(q_all, state_e.astype(BF16))
                st_view[...] = (state_e + _dot_tn(v_all, k_all)) * em
                for h, (_, _, gate_ref, gain, y_view) in enumerate(heads):
                    rows = slice(h * REC_CHUNK, (h + 1) * REC_CHUNK)
                    staged.append((scores[rows], o_inter[rows], vbs[h], base, gate_ref, gain, y_view))
        for scores, o_inter, vb, base, gate_ref, gain, y_view in staged:
            s = jnp.where(causal, scores, 0.0).astype(BF16)
            o = o_inter + _dot(s, vb)
            gate = natural(gate_ref, base).astype(F32)
            y_view[pl.ds(base, REC_CHUNK), :] = _finish(o, gain, gate).astype(y_view.dtype)
        return carry

    def exact_chunk(ci, carry):
        base = pl.multiple_of(ci * REC_CHUNK, REC_CHUNK)
        for (q_ref, k_ref, g_ref), st_view, heads in groups:
            b_perm, b_last = decay_logs(g_ref, base)
            b = to_natural(0, b_perm)
            q_all = natural(q_ref, base).astype(F32)
            k = natural(k_ref, base).astype(F32)
            state = st_view[...]
            new_state = state * jnp.exp2(b_last)
            kd = k * jnp.exp2(b_last - b)
            for qmask, v_ref, gate_ref, gain, y_view in heads:
                q = masked(q_all, qmask)
                vb = natural(v_ref, base)
                gate = natural(gate_ref, base).astype(F32)
                fb_s[0] = b
                fb_s[1] = q
                fb_s[2] = k
                fb_s[3] = vb.astype(F32)

                def row(r, c2):
                    bt = fb_s[0, pl.ds(r, 1), :]
                    qt = fb_s[1, pl.ds(r, 1), :]
                    w = jnp.exp2(jnp.where(tok_of_row <= r, bt - fb_s[0], -jnp.inf))
                    sc = jnp.sum(qt * w * fb_s[2], axis=-1, keepdims=True)
                    oi_s[pl.ds(r, 1), :] = jnp.sum(sc * fb_s[3], axis=0, keepdims=True)
                    return c2

                lax.fori_loop(0, REC_CHUNK, row, 0)
                o = _dot_nt((q * jnp.exp2(b)).astype(BF16), state.astype(BF16)) + oi_s[...]
                new_state = new_state + _dot_tn(vb, masked(kd, qmask).astype(BF16))
                y_view[pl.ds(base, REC_CHUNK), :] = _finish(o, gain, gate).astype(y_view.dtype)
            st_view[...] = new_state
        return carry

    @pl.when(safe)
    def _():
        lax.fori_loop(0, nc // unroll, fast_chunk, 0)

    @pl.when(jnp.logical_not(safe))
    def _():
        lax.fori_loop(0, nc, exact_chunk, 0)


def _safe_flags(stats, *, batch, tokens_per_batch, tt):
    tiles_per_batch = stats.shape[0] // batch
    chunk_tot = jnp.min(stats[:, 0], axis=-1).reshape(batch, tokens_per_batch // tt, -1)
    tile_mag = jnp.max(stats[:, 1], axis=(-2, -1)).reshape(batch, tiles_per_batch)
    step_tot = jnp.min(chunk_tot, axis=(0, 2))
    step_mag = jnp.repeat(jnp.max(tile_mag, axis=0), (tokens_per_batch // tt) // tiles_per_batch)
    return jnp.logical_and(step_tot >= -2.0 * SAFE_EXP2, step_mag <= SAFE_MAG).astype(jnp.int32)


def _recurrence(dec, act, stats, hg_norm, gla_norm, *, batch, tokens_per_batch, tt=REC_TILE):
    n = dec.shape[0]
    nt = tokens_per_batch // tt
    pairs = HG_HEADS // 2
    dec3 = dec.reshape(batch, tokens_per_batch, dec.shape[1])
    act3 = act.reshape(batch, tokens_per_batch, act.shape[1])
    safe = _safe_flags(stats, batch=batch, tokens_per_batch=tokens_per_batch, tt=tt)

    def tile_of(name, p, e):
        off = DEC_OFF[name] if name in DEC_OFF else ACT_OFF[name]
        return off + (p if e is None else 2 * p + e)

    def spec(name, e=None):
        return pl.BlockSpec((batch, tt, LANES), lambda p, i, flags: (0, i, tile_of(name, p, e)))

    names = [("q", 0), ("q", 1), ("k", 0), ("k", 1), ("g", 0), ("g", 1), ("v", 0), ("v", 1), ("sg", 0), ("sg", 1),
             ("gq", None), ("gk", None), ("gd", None), ("gv", 0), ("gv", 1), ("gsg", 0), ("gsg", 1)]
    in_specs = [spec(name, e) for name, e in names] + [
        pl.BlockSpec((1, LANES), lambda p, i, flags: (0, 0)),
        pl.BlockSpec((1, LANES), lambda p, i, flags: (0, 0)),
    ]
    operands = [dec3 if name in DEC_OFF else act3 for name, _ in names]
    n_groups = 3 * batch
    out_spec = pl.BlockSpec((2, batch, tt, LANES), lambda p, i, flags: (p, 0, i, 0))
    y_hg, y_gla = pl.pallas_call(
        functools.partial(_rec_kernel, tt=tt, unroll=REC_UNROLL),
        grid_spec=pltpu.PrefetchScalarGridSpec(
            num_scalar_prefetch=1,
            grid=(pairs, nt),
            in_specs=in_specs,
            out_specs=[out_spec, out_spec],
            scratch_shapes=[
                pltpu.VMEM((batch, 3, LANES, LANES), F32),
                pltpu.VMEM((2 * REC_UNROLL * n_groups, REC_CHUNK, LANES), F32),
                pltpu.VMEM((4, REC_CHUNK, LANES), F32),
                pltpu.VMEM((REC_CHUNK, LANES), F32),
            ],
        ),
        out_shape=[jax.ShapeDtypeStruct((HG_HEADS, batch, tokens_per_batch, HG_DV), BF16),
                   jax.ShapeDtypeStruct((GLA_HEADS, batch, tokens_per_batch, GLA_DV), BF16)],
        compiler_params=pltpu.CompilerParams(
            dimension_semantics=("arbitrary", "arbitrary"), vmem_limit_bytes=VMEM_LIMIT["recurrence"]),
        name="recurrence",
    )(safe, *operands, hg_norm.reshape(1, HG_DV), gla_norm.reshape(1, GLA_DV))
    return y_hg.reshape(HG_HEADS, n, HG_DV), y_gla.reshape(GLA_HEADS, n, GLA_DV)


def _merge_ffn_kernel(x_ref, mod_ref, yh_ref, yg_ref, rh_ref, rg_ref, wuh_ref, wug_ref, wo_ref,
                      gain_ref, w_in_s, w_down_s, fgain_ref, o_ref, *, sub_mix, sub_ffn, d_ff, bounds, final):
    if True:
        gate = mod_ref[0, 3 * sub_mix + 2:3 * sub_mix + 3, :]

        def up(y_ref, w_ref):
            y = jnp.concatenate([y_ref[h].astype(BF16) for h in range(y_ref.shape[0])], axis=-1)
            return _dot(y, w_ref[...])

        merged = (jax.nn.sigmoid(rh_ref[...].astype(F32)) * up(yh_ref, wuh_ref)
                  + jax.nn.sigmoid(rg_ref[...].astype(F32)) * up(yg_ref, wug_ref))
        x = x_ref[...] + gate * _dot(merged.astype(BF16), wo_ref[...])
        o_ref[...] = _ffn_body(x, mod_ref, gain_ref, w_in_s, w_down_s, fgain_ref,
                               sub=sub_ffn, d_ff=d_ff, bounds=bounds, final=final)


def _merge_ffn(x2d, mod3, y_hg, y_gla, r, w_up_hg, w_up_gla, w_out, gain, w_in_b, w_down_b, d_ff, fgain, *,
               sub_mix, sub_ffn, tokens_per_batch, final, tm=TOKEN_TILE):
    n, d = x2d.shape
    tiles_per_batch = tokens_per_batch // tm
    const = dict(pipeline_mode=pl.Buffered(1))

    return pl.pallas_call(
        functools.partial(_merge_ffn_kernel, sub_mix=sub_mix, sub_ffn=sub_ffn, d_ff=d_ff,
                          bounds=_mxu_aligned_bounds(d_ff, 2), final=final),
        grid=(n // tm,),
        in_specs=[
            pl.BlockSpec((tm, d), lambda i: (i, 0)),
            pl.BlockSpec((1, 3 * N_SUB, d), lambda i: (i // tiles_per_batch, 0, 0)),
            pl.BlockSpec((HG_HEADS, tm, HG_DV), lambda i: (0, i, 0)),
            pl.BlockSpec((GLA_HEADS, tm, GLA_DV), lambda i: (0, i, 0)),
            pl.BlockSpec((tm, d), lambda i: (i, 0)),
            pl.BlockSpec((tm, d), lambda i: (i, 1)),
            pl.BlockSpec(w_up_hg.shape, lambda i: (0, 0), **const),
            pl.BlockSpec(w_up_gla.shape, lambda i: (0, 0), **const),
            pl.BlockSpec(w_out.shape, lambda i: (0, 0), **const),
            pl.BlockSpec((1, d), lambda i: (0, 0)),
            pl.BlockSpec(w_in_b.shape, lambda i: (0, 0), **const),
            pl.BlockSpec(w_down_b.shape, lambda i: (0, 0), **const),
            pl.BlockSpec((1, d), lambda i: (0, 0)),
        ],
        out_specs=pl.BlockSpec((tm, d), lambda i: (i, 0)),
        out_shape=jax.ShapeDtypeStruct((n, d), F32),
        compiler_params=pltpu.CompilerParams(dimension_semantics=("arbitrary",), vmem_limit_bytes=VMEM_LIMIT["merge_ffn"]),
        name="merge_ffn",
    )(x2d, mod3, y_hg, y_gla, r, r, w_up_hg, w_up_gla, w_out, gain.reshape(1, d), w_in_b, w_down_b,
      fgain.reshape(1, d))


def kernel(x, c, w_ada, b_ada, norm_gains, ffn1_w_in, ffn1_w_down, w_in_mix, w_gk_up, b_gk_up, lb_logits,
           hg_norm, gla_norm, w_up_hg, w_up_gla, w_out, ffn2_w_in, ffn2_w_down, final_norm):
    batch, seq, d = x.shape
    depth = w_ada.shape[0]

    x2d = x.reshape(batch * seq, d)
    c_pad = jnp.pad(c, ((0, SUBLANES - batch % SUBLANES if batch % SUBLANES else 0), (0, 0)))
    for l in range(depth):
        mod = _ada(c_pad, w_ada[l], b_ada[l])[:batch].reshape(batch, 3 * N_SUB, d)

        x2d, (w_mix_b, w_in2_b, w_down2_b) = _ffn(
            x2d, mod, norm_gains[l, 0], ffn1_w_in, ffn1_w_down, final_norm,
            (jnp.swapaxes(w_in_mix, 1, 2), ffn2_w_in, ffn2_w_down), layer=l, sub=0, tokens_per_batch=seq, final=False)

        wup_pad = jnp.pad(w_gk_up[l], ((0, LANES - GLA_RANK), (0, 0)))
        dec, act, r, stats = _inproj(x2d, mod, norm_gains[l, 1], w_mix_b, w_in_mix.shape[2], lb_logits, wup_pad,
                                     b_gk_up[l], layer=l, sub=1, tokens_per_batch=seq)

        y_hg, y_gla = _recurrence(dec, act, stats, hg_norm[l], gla_norm[l], batch=batch, tokens_per_batch=seq)

        x2d = _merge_ffn(x2d, mod, y_hg, y_gla, r, w_up_hg[l].astype(BF16), w_up_gla[l].astype(BF16),
                         w_out[l].astype(BF16), norm_gains[l, 2], w_in2_b, w_down2_b, ffn2_w_down.shape[1],
                         final_norm, sub_mix=1, sub_ffn=2, tokens_per_batch=seq, final=l == depth - 1)
    return x2d.reshape(batch, seq, d)
```

```python
import functools

import jax
import jax.numpy as jnp
from jax import lax
from jax.experimental import pallas as pl
from jax.experimental.pallas import tpu as pltpu

F32 = jnp.float32
BF16 = jnp.bfloat16

EPS = 1e-6
N_SUB = 3
HG_HEADS = 4
HG_DK = 128
HG_DV = 128
GLA_HEADS = 4
GLA_DK = 64
GLA_DV = 128
GLA_RANK = 16
GLA_NORMALIZER = 16.0

LANES = 128
SUBLANES = 8
MXU_WIDTH = 256
MIB = 1024 * 1024

TOKEN_TILE = 512
REC_TILE = 512
ADA_STEPS = 4
VMEM_LIMIT = {"ffn": 52 * MIB, "inproj": 56 * MIB, "recurrence": 48 * MIB, "merge_ffn": 58 * MIB}

REC_CHUNK = 64
PERM_STRIDE = 4
PERM_GROUP = PERM_STRIDE * SUBLANES
N_GROUPS = REC_CHUNK // PERM_GROUP
REC_UNROLL = 8
LOG2E = 1.4426950408889634
SAFE_EXP2 = 100.0
SAFE_MAG = 500.0


def _dot(a, b):
    return jnp.dot(a, b, preferred_element_type=F32)


def _dot_nt(a, b):
    return lax.dot_general(a, b, (((1,), (1,)), ((), ())), preferred_element_type=F32)


def _dot_tn(a, b):
    return lax.dot_general(a, b, (((0,), (0,)), ((), ())), preferred_element_type=F32)


def _silu(v):
    return v * jax.nn.sigmoid(v)


def _norm_mod(x, gain, shift, scale):
    y = x * lax.rsqrt(jnp.mean(x * x, axis=-1, keepdims=True) + EPS) * gain
    return y * (1.0 + scale) + shift


def _ada_kernel(c_ref, w_ref, b_ref, o_ref):
    cond = _silu(c_ref[...]).astype(BF16)
    o_ref[...] = _dot(cond, w_ref[...].astype(BF16)) + b_ref[...]


def _ada(c_pad, w, b):
    rows, d = c_pad.shape
    n = w.shape[1]
    bn = n // ADA_STEPS
    return pl.pallas_call(
        _ada_kernel,
        grid=(n // bn,),
        in_specs=[
            pl.BlockSpec((rows, d), lambda j: (0, 0)),
            pl.BlockSpec((d, bn), lambda j: (0, j)),
            pl.BlockSpec((1, bn), lambda j: (0, j)),
        ],
        out_specs=pl.BlockSpec((rows, bn), lambda j: (0, j)),
        out_shape=jax.ShapeDtypeStruct((rows, n), F32),
        name="adaln",
    )(c_pad, w, b.reshape(1, n))


def _mxu_aligned_bounds(width, parts):
    if width % MXU_WIDTH:
        return (0, width)
    tiles = width // MXU_WIDTH
    return tuple(MXU_WIDTH * ((tiles * p + parts - 1) // parts) for p in range(parts)) + (width,)


def _ffn_weight_specs(layer, d, d_ff):
    assert d_ff % MXU_WIDTH == 0
    steps = d_ff // MXU_WIDTH

    def chunk(i):
        return jnp.minimum(i, steps - 1)

    specs = [
        pl.BlockSpec((None, d, MXU_WIDTH), lambda i: (layer, 0, chunk(i))),
        pl.BlockSpec((None, d, MXU_WIDTH), lambda i: (layer, 0, steps + chunk(i))),
        pl.BlockSpec((None, MXU_WIDTH, d), lambda i: (layer, chunk(i), 0)),
    ]
    scratch = [pltpu.VMEM((d, 2 * d_ff), BF16), pltpu.VMEM((d_ff, d), BF16)]
    return steps, specs, scratch


def _stage_ffn_weights(step, wg_ref, wu_ref, wd_ref, w_in_s, w_down_s, d_ff):
    for j in range(d_ff // MXU_WIDTH):
        @pl.when(step == j)
        def _(lo=j * MXU_WIDTH):
            w_in_s[:, lo:lo + MXU_WIDTH] = wg_ref[...].astype(BF16)
            w_in_s[:, d_ff + lo:d_ff + lo + MXU_WIDTH] = wu_ref[...].astype(BF16)
            w_down_s[lo:lo + MXU_WIDTH, :] = wd_ref[...].astype(BF16)


class _SideCast:
    def __init__(self, weight_all, layer, n_tiles):
        self.weight_all, self.layer = weight_all, layer
        self.valid_rows, self.cols = weight_all.shape[1:]
        least = -(-(-(-self.valid_rows // n_tiles)) // 16) * 16
        self.rows = next((r for r in range(least, 2 * least + 1, 16) if self.valid_rows % r == 0), least)
        self.n_chunks = -(-self.valid_rows // self.rows)

    def chunk(self, tile):
        return jnp.minimum(tile, self.n_chunks - 1)

    def in_spec(self, tile_of_step):
        return pl.BlockSpec((None, self.rows, self.cols), lambda i: (self.layer, self.chunk(tile_of_step(i)), 0))

    def out_spec(self, tile_of_step):
        return pl.BlockSpec((self.rows, self.cols), lambda i: (self.chunk(tile_of_step(i)), 0))

    def out_shape(self):
        return jax.ShapeDtypeStruct((self.n_chunks * self.rows, self.cols), BF16)

    def cast(self, tile, in_ref, out_ref):
        w = in_ref[...]
        if self.n_chunks * self.rows > self.valid_rows:
            row = self.chunk(tile) * self.rows + lax.broadcasted_iota(jnp.int32, w.shape, 0)
            w = jnp.where(row < self.valid_rows, w, 0.0)
        out_ref[...] = w.astype(BF16)


def _ffn_kernel(*refs, sub, d_ff, bounds, final, w_steps, sides):
    x_ref, mod_ref, gain_ref, wg_ref, wu_ref, wd_ref, fgain_ref = refs[:7]
    side_in = refs[7:7 + len(sides)]
    o_ref = refs[7 + len(sides)]
    side_out = refs[8 + len(sides):8 + 2 * len(sides)]
    w_in_s, w_down_s = refs[8 + 2 * len(sides):]
    step = pl.program_id(0)
    _stage_ffn_weights(step, wg_ref, wu_ref, wd_ref, w_in_s, w_down_s, d_ff)

    @pl.when(step >= w_steps)
    def _():
        o_ref[...] = _ffn_body(x_ref[...], mod_ref, gain_ref, w_in_s, w_down_s, fgain_ref,
                               sub=sub, d_ff=d_ff, bounds=bounds, final=final)
        for side, in_ref, out_ref in zip(sides, side_in, side_out):
            side.cast(step - w_steps, in_ref, out_ref)


def _ffn_body(x, mod_ref, gain_ref, w_in_ref, w_down_ref, fgain_ref, *, sub, d_ff, bounds, final):
    shift = mod_ref[0, 3 * sub + 0:3 * sub + 1, :]
    scale = mod_ref[0, 3 * sub + 1:3 * sub + 2, :]
    gate = mod_ref[0, 3 * sub + 2:3 * sub + 3, :]
    hb = _norm_mod(x, gain_ref[...], shift, scale).astype(BF16)
    acc = None
    for lo, hi in zip(bounds[:-1], bounds[1:]):
        g = _dot(hb, w_in_ref[:, lo:hi])
        u = _dot(hb, w_in_ref[:, d_ff + lo:d_ff + hi])
        act = (_silu(g) * u).astype(BF16)
        part = _dot(act, w_down_ref[lo:hi, :])
        acc = part if acc is None else acc + part
    xn = x + (0.5 * gate) * acc
    if final:
        xn = xn * lax.rsqrt(jnp.mean(xn * xn, axis=-1, keepdims=True) + EPS) * fgain_ref[...]
    return xn


def _ffn(x2d, mod3, gain, w_in_all, w_down_all, fgain, side_weights, *, layer, sub, tokens_per_batch, final,
         tm=TOKEN_TILE):
    n, d = x2d.shape
    d_ff = w_down_all.shape[1]
    tiles_per_batch = tokens_per_batch // tm
    w_steps, w_specs, w_scratch = _ffn_weight_specs(layer, d, d_ff)
    sides = [_SideCast(w, layer, n // tm) for w in side_weights]

    def tile(i):
        return jnp.maximum(i - w_steps, 0)

    outs = pl.pallas_call(
        functools.partial(_ffn_kernel, sub=sub, d_ff=d_ff, bounds=_mxu_aligned_bounds(d_ff, 2), final=final,
                          w_steps=w_steps, sides=sides),
        grid=(w_steps + n // tm,),
        in_specs=[
            pl.BlockSpec((tm, d), lambda i: (tile(i), 0)),
            pl.BlockSpec((1, 3 * N_SUB, d), lambda i: (tile(i) // tiles_per_batch, 0, 0)),
            pl.BlockSpec((1, d), lambda i: (0, 0)),
            *w_specs,
            pl.BlockSpec((1, d), lambda i: (0, 0)),
            *[s.in_spec(tile) for s in sides],
        ],
        out_specs=[pl.BlockSpec((tm, d), lambda i: (tile(i), 0)), *[s.out_spec(tile) for s in sides]],
        out_shape=[jax.ShapeDtypeStruct((n, d), F32), *[s.out_shape() for s in sides]],
        scratch_shapes=w_scratch,
        compiler_params=pltpu.CompilerParams(dimension_semantics=("arbitrary",), vmem_limit_bytes=VMEM_LIMIT["ffn"]),
        name="ffn_final" if final else "ffn",
    )(x2d, mod3, gain.reshape(1, d), w_in_all, w_in_all, w_down_all, fgain.reshape(1, d), *side_weights)
    return outs[0], outs[1:]


DEC_OFF = {"g": 0, "gd": 4}
DEC_TILES = 6
ACT_OFF = {"q": 0, "k": 4, "v": 8, "sg": 12, "gq": 16, "gk": 18, "gv": 20, "gsg": 24}
ACT_TILES = 28


def _inproj_kernel(x_ref, mod_ref, gain_ref, wt_ref, lbl_ref, wup_ref, bup_ref, dec_ref, act_ref, r_ref, stats_ref, *,
                   sub, layer):
    shift = mod_ref[0, 3 * sub + 0:3 * sub + 1, :]
    scale = mod_ref[0, 3 * sub + 1:3 * sub + 2, :]
    hb = _norm_mod(x_ref[...], gain_ref[...], shift, scale).astype(BF16)
    tm = hb.shape[0]
    hg_qk, hg_w = HG_HEADS * HG_DK, HG_HEADS * HG_DV
    gla_k, gla_v = GLA_HEADS * GLA_DK, GLA_HEADS * GLA_DV
    n_hg = 2 * hg_qk + 2 * hg_w
    n_rec = n_hg + 2 * gla_k + 2 * gla_v

    def put(name, val):
        if name in DEC_OFF:
            lo = DEC_OFF[name] * LANES
            dec_ref[:, lo:lo + val.shape[1]] = val
        else:
            lo = ACT_OFF[name] * LANES
            val = val.astype(act_ref.dtype)
            act_ref[:, lo:lo + val.shape[1]] = val
        return val

    def chunk_min(g):
        tot = jnp.sum(g.reshape(tm // REC_CHUNK, REC_CHUNK, g.shape[1]), axis=1)
        return functools.reduce(jnp.minimum, [tot[:, j:j + LANES] for j in range(0, g.shape[1], LANES)])

    def mag_max(v):
        rows = 2 * SUBLANES
        m = jnp.max(jnp.abs(v).reshape(tm // rows, rows, v.shape[1]), axis=0)
        m = functools.reduce(jnp.maximum, [m[:, j:j + LANES] for j in range(0, v.shape[1], LANES)]).astype(F32)
        return jnp.maximum(m[:SUBLANES], m[SUBLANES:])

    ph = _dot_nt(hb, wt_ref[:n_hg, :])
    lbl = lbl_ref[...]
    ex = jnp.exp(lbl - jnp.max(lbl, axis=0, keepdims=True))
    lb = jnp.sum(ex[:layer + 1], axis=0, keepdims=True) / jnp.sum(ex, axis=0, keepdims=True)
    c1 = 0.5 * (1.0 - lb)
    q = _silu_tanh(ph[:, :hg_qk])
    p = c1 * jnp.tanh(0.5 * ph[:, hg_qk:2 * hg_qk])
    g = jnp.log2((1.0 - c1) + p)
    q_b = put("q", q)
    put("k", c1 - p)
    put("g", g)
    v_b = put("v", ph[:, 2 * hg_qk:2 * hg_qk + hg_w])
    put("sg", _silu_tanh(ph[:, 2 * hg_qk + hg_w:]))

    pg = _dot_nt(hb, wt_ref[n_hg:n_rec, :])
    code = _dot_nt(hb, wt_ref[n_rec:n_rec + LANES, :])
    logits = _dot(code.astype(BF16), wup_ref[...].astype(BF16)) + bup_ref[...]
    gd = (jnp.minimum(logits, 0.0) - jnp.log(1.0 + jnp.exp(-jnp.abs(logits)))) * (LOG2E / GLA_NORMALIZER)
    gq = pg[:, :gla_k] * (GLA_DK ** -0.5)
    gk = pg[:, gla_k:2 * gla_k]
    gq_b = put("gq", gq)
    gk_b = put("gk", gk)
    put("gd", gd)
    gv_b = put("gv", pg[:, 2 * gla_k:2 * gla_k + gla_v])
    put("gsg", _silu_tanh(pg[:, 2 * gla_k + gla_v:]))

    r_ref[...] = _dot_nt(hb, wt_ref[n_rec + GLA_RANK:n_rec + GLA_RANK + r_ref.shape[1], :]).astype(r_ref.dtype)
    stats_ref[0, 0] = jnp.minimum(chunk_min(g), chunk_min(gd))
    stats_ref[0, 1] = functools.reduce(jnp.maximum, [mag_max(a) for a in (q_b, v_b, gq_b, gk_b, gv_b)])


def _inproj(x2d, mod3, gain, wt_b, n_channels, lb_logits, wup_pad, bup, *, layer, sub, tokens_per_batch,
            tm=TOKEN_TILE):
    n, d = x2d.shape
    n_rec = 2 * HG_HEADS * HG_DK + 2 * HG_HEADS * HG_DV + 2 * GLA_HEADS * GLA_DK + 2 * GLA_HEADS * GLA_DV
    n_r = n_channels - n_rec - GLA_RANK
    assert n_rec + LANES <= wt_b.shape[0] and tm // REC_CHUNK == SUBLANES
    tiles_per_batch = tokens_per_batch // tm

    return pl.pallas_call(
        functools.partial(_inproj_kernel, sub=sub, layer=layer),
        grid=(n // tm,),
        in_specs=[
            pl.BlockSpec((tm, d), lambda i: (i, 0)),
            pl.BlockSpec((1, 3 * N_SUB, d), lambda i: (i // tiles_per_batch, 0, 0)),
            pl.BlockSpec((1, d), lambda i: (0, 0)),
            pl.BlockSpec(wt_b.shape, lambda i: (0, 0), pipeline_mode=pl.Buffered(1)),
            pl.BlockSpec(lb_logits.shape, lambda i: (0, 0)),
            pl.BlockSpec(wup_pad.shape, lambda i: (0, 0)),
            pl.BlockSpec((1, bup.shape[0]), lambda i: (0, 0)),
        ],
        out_specs=[
            pl.BlockSpec((tm, DEC_TILES * LANES), lambda i: (i, 0)),
            pl.BlockSpec((tm, ACT_TILES * LANES), lambda i: (i, 0)),
            pl.BlockSpec((tm, n_r), lambda i: (i, 0)),
            pl.BlockSpec((1, 2, SUBLANES, LANES), lambda i: (i, 0, 0, 0)),
        ],
        out_shape=[
            jax.ShapeDtypeStruct((n, DEC_TILES * LANES), F32),
            jax.ShapeDtypeStruct((n, ACT_TILES * LANES), BF16),
            jax.ShapeDtypeStruct((n, n_r), BF16),
            jax.ShapeDtypeStruct((n // tm, 2, SUBLANES, LANES), F32),
        ],
        compiler_params=pltpu.CompilerParams(dimension_semantics=("arbitrary",), vmem_limit_bytes=VMEM_LIMIT["inproj"]),
        name="inproj",
    )(x2d, mod3, gain.reshape(1, d), wt_b, lb_logits, wup_pad, bup.reshape(1, -1))


def _load_perm(ref, base):
    return [ref[pl.ds(base + PERM_GROUP * c + a, SUBLANES, stride=PERM_STRIDE), :]
            for c in range(N_GROUPS) for a in range(PERM_STRIDE)]


def _store_perm(ref, base, val):
    for c in range(N_GROUPS):
        for a in range(PERM_STRIDE):
            u = PERM_STRIDE * c + a
            ref[pl.ds(base + PERM_GROUP * c + a, SUBLANES, stride=PERM_STRIDE), :] = (
                val[SUBLANES * u:SUBLANES * (u + 1), :])


def _shift_down(v, d, sub_iota):
    return jnp.where(sub_iota >= d, pltpu.roll(v, d, 0), 0.0)


def _cumsum_perm(g, sub_iota):
    out = []
    carry = None
    for c in range(N_GROUPS):
        pre = [g[PERM_STRIDE * c]]
        for a in range(1, PERM_STRIDE):
            pre.append(pre[-1] + g[PERM_STRIDE * c + a])
        inc = pre[-1]
        inc = inc + _shift_down(inc, 1, sub_iota)
        inc = inc + _shift_down(inc, 2, sub_iota)
        inc = inc + _shift_down(inc, 4, sub_iota)
        exc = _shift_down(inc, 1, sub_iota)
        if carry is not None:
            exc = exc + carry
        out.extend(p + exc for p in pre)
        tot = inc[SUBLANES - 1:SUBLANES, :]
        carry = tot if carry is None else carry + tot
    return jnp.concatenate(out, axis=0), carry


def _silu_tanh(v):
    h = 0.5 * v
    return h * jnp.tanh(h) + h


def _finish(o, gain, act_gate):
    return o * lax.rsqrt(jnp.mean(o * o, axis=-1, keepdims=True) + EPS) * gain * act_gate


def _rec_kernel(safe_ref, q0, q1, k0, k1, g0, g1, v0, v1, sg0, sg1, gq, gk, gd, gv0, gv1, gsg0, gsg1,
                hgn_ref, glan_ref, yh_ref, yg_ref, st_s, e_s, fb_s, oi_s, *, tt, unroll):
    nc = tt // REC_CHUNK
    nb = q0.shape[0]

    @pl.when(pl.program_id(1) == 0)
    def _():
        st_s[...] = jnp.zeros_like(st_s)

    safe = safe_ref[pl.program_id(1)] != 0

    sub_iota = lax.broadcasted_iota(jnp.int32, (SUBLANES, LANES), 0)
    lane = lax.broadcasted_iota(jnp.int32, (REC_CHUNK, LANES), 1)
    causal = (lax.broadcasted_iota(jnp.int32, (REC_CHUNK, REC_CHUNK), 1)
              <= lax.broadcasted_iota(jnp.int32, (REC_CHUNK, REC_CHUNK), 0))
    tok_of_row = lax.broadcasted_iota(jnp.int32, (REC_CHUNK, LANES), 0)
    hgn = hgn_ref[...]
    glan = glan_ref[...]

    head_masks = (lane < GLA_DK, lane >= GLA_DK)
    groups = []
    for bi in range(nb):
        groups += [
            ((q0.at[bi], k0.at[bi], g0.at[bi]), st_s.at[bi, 0],
             [(None, v0.at[bi], sg0.at[bi], hgn, yh_ref.at[0, bi])]),
            ((q1.at[bi], k1.at[bi], g1.at[bi]), st_s.at[bi, 1],
             [(None, v1.at[bi], sg1.at[bi], hgn, yh_ref.at[1, bi])]),
            ((gq.at[bi], gk.at[bi], gd.at[bi]), st_s.at[bi, 2],
             [(head_masks[0], gv0.at[bi], gsg0.at[bi], glan, yg_ref.at[0, bi]),
              (head_masks[1], gv1.at[bi], gsg1.at[bi], glan, yg_ref.at[1, bi])]),
        ]

    def natural(ref, base):
        return ref[pl.ds(base, REC_CHUNK), :]

    def decay_logs(g_ref, base):
        return _cumsum_perm(_load_perm(g_ref, base), sub_iota)

    def to_natural(slot, val):
        _store_perm(e_s.at[slot], 0, val)
        return e_s[slot]

    def masked(x, mask):
        return x if mask is None else jnp.where(mask, x, jnp.zeros_like(x))

    def fast_chunk(ci, carry):
        staged = []
        for sub in range(unroll):
            base = pl.multiple_of((ci * unroll + sub) * REC_CHUNK, REC_CHUNK)
            for gi, ((q_ref, k_ref, g_ref), st_view, heads) in enumerate(groups):
                b, b_last = decay_logs(g_ref, base)
                half = 0.5 * b_last
                d = b - half
                em = jnp.exp2(half)
                slot = 2 * (sub * len(groups) + gi)
                qt_b = (natural(q_ref, base).astype(F32) * to_natural(slot, jnp.exp2(d))).astype(BF16)
                kt_b = (natural(k_ref, base).astype(F32) * to_natural(slot + 1, jnp.exp2(-d))).astype(BF16)
                vbs = [natural(v_ref, base) for _, v_ref, _, _, _ in heads]
                q_all = jnp.concatenate([masked(qt_b, m) for m, *_ in heads], axis=0)
                k_all = jnp.concatenate([masked(kt_b, m) for m, *_ in heads], axis=0)
                v_all = jnp.concatenate(vbs, axis=0)
                state_e = st_view[...] * em
                scores = _dot_nt(q_all, kt_b)
                o_inter = _dot_nt(q_all, state_e.astype(BF16))
                st_view[...] = (state_e + _dot_tn(v_all, k_all)) * em
                for h, (_, _, gate_ref, gain, y_view) in enumerate(heads):
                    rows = slice(h * REC_CHUNK, (h + 1) * REC_CHUNK)
                    staged.append((scores[rows], o_inter[rows], vbs[h], base, gate_ref, gain, y_view))
        for scores, o_inter, vb, base, gate_ref, gain, y_view in staged:
            s = jnp.where(causal, scores, 0.0).astype(BF16)
            o = o_inter + _dot(s, vb)
            gate = natural(gate_ref, base).astype(F32)
            y_view[pl.ds(base, REC_CHUNK), :] = _finish(o, gain, gate).astype(y_view.dtype)
        return carry

    def exact_chunk(ci, carry):
        base = pl.multiple_of(ci * REC_CHUNK, REC_CHUNK)
        for (q_ref, k_ref, g_ref), st_view, heads in groups:
            b_perm, b_last = decay_logs(g_ref, base)
            b = to_natural(0, b_perm)
            q_all = natural(q_ref, base).astype(F32)
            k = natural(k_ref, base).astype(F32)
            state = st_view[...]
            new_state = state * jnp.exp2(b_last)
            kd = k * jnp.exp2(b_last - b)
            for qmask, v_ref, gate_ref, gain, y_view in heads:
                q = masked(q_all, qmask)
                vb = natural(v_ref, base)
                gate = natural(gate_ref, base).astype(F32)
                fb_s[0] = b
                fb_s[1] = q
                fb_s[2] = k
                fb_s[3] = vb.astype(F32)

                def row(r, c2):
                    bt = fb_s[0, pl.ds(r, 1), :]
                    qt = fb_s[1, pl.ds(r, 1), :]
                    w = jnp.exp2(jnp.where(tok_of_row <= r, bt - fb_s[0], -jnp.inf))
                    sc = jnp.sum(qt * w * fb_s[2], axis=-1, keepdims=True)
                    oi_s[pl.ds(r, 1), :] = jnp.sum(sc * fb_s[3], axis=0, keepdims=True)
                    return c2

                lax.fori_loop(0, REC_CHUNK, row, 0)
                o = _dot_nt((q * jnp.exp2(b)).astype(BF16), state.astype(BF16)) + oi_s[...]
                new_state = new_state + _dot_tn(vb, masked(kd, qmask).astype(BF16))
                y_view[pl.ds(base, REC_CHUNK), :] = _finish(o, gain, gate).astype(y_view.dtype)
            st_view[...] = new_state
        return carry

    @pl.when(safe)
    def _():
        lax.fori_loop(0, nc // unroll, fast_chunk, 0)

    @pl.when(jnp.logical_not(safe))
    def _():
        lax.fori_loop(0, nc, exact_chunk, 0)


def _safe_flags(stats, *, batch, tokens_per_batch, tt):
    tiles_per_batch = stats.shape[0] // batch
    chunk_tot = jnp.min(stats[:, 0], axis=-1).reshape(batch, tokens_per_batch // tt, -1)
    tile_mag = jnp.max(stats[:, 1], axis=(-2, -1)).reshape(batch, tiles_per_batch)
    step_tot = jnp.min(chunk_tot, axis=(0, 2))
    step_mag = jnp.repeat(jnp.max(tile_mag, axis=0), (tokens_per_batch // tt) // tiles_per_batch)
    return jnp.logical_and(step_tot >= -2.0 * SAFE_EXP2, step_mag <= SAFE_MAG).astype(jnp.int32)


def _recurrence(dec, act, stats, hg_norm, gla_norm, *, batch, tokens_per_batch, tt=REC_TILE):
    n = dec.shape[0]
    nt = tokens_per_batch // tt
    pairs = HG_HEADS // 2
    dec3 = dec.reshape(batch, tokens_per_batch, dec.shape[1])
    act3 = act.reshape(batch, tokens_per_batch, act.shape[1])
    safe = _safe_flags(stats, batch=batch, tokens_per_batch=tokens_per_batch, tt=tt)

    def tile_of(name, p, e):
        off = DEC_OFF[name] if name in DEC_OFF else ACT_OFF[name]
        return off + (p if e is None else 2 * p + e)

    def spec(name, e=None):
        return pl.BlockSpec((batch, tt, LANES), lambda p, i, flags: (0, i, tile_of(name, p, e)))

    names = [("q", 0), ("q", 1), ("k", 0), ("k", 1), ("g", 0), ("g", 1), ("v", 0), ("v", 1), ("sg", 0), ("sg", 1),
             ("gq", None), ("gk", None), ("gd", None), ("gv", 0), ("gv", 1), ("gsg", 0), ("gsg", 1)]
    in_specs = [spec(name, e) for name, e in names] + [
        pl.BlockSpec((1, LANES), lambda p, i, flags: (0, 0)),
        pl.BlockSpec((1, LANES), lambda p, i, flags: (0, 0)),
    ]
    operands = [dec3 if name in DEC_OFF else act3 for name, _ in names]
    n_groups = 3 * batch
    out_spec = pl.BlockSpec((2, batch, tt, LANES), lambda p, i, flags: (p, 0, i, 0))
    y_hg, y_gla = pl.pallas_call(
        functools.partial(_rec_kernel, tt=tt, unroll=REC_UNROLL),
        grid_spec=pltpu.PrefetchScalarGridSpec(
            num_scalar_prefetch=1,
            grid=(pairs, nt),
            in_specs=in_specs,
            out_specs=[out_spec, out_spec],
            scratch_shapes=[
                pltpu.VMEM((batch, 3, LANES, LANES), F32),
                pltpu.VMEM((2 * REC_UNROLL * n_groups, REC_CHUNK, LANES), F32),
                pltpu.VMEM((4, REC_CHUNK, LANES), F32),
                pltpu.VMEM((REC_CHUNK, LANES), F32),
            ],
        ),
        out_shape=[jax.ShapeDtypeStruct((HG_HEADS, batch, tokens_per_batch, HG_DV), BF16),
                   jax.ShapeDtypeStruct((GLA_HEADS, batch, tokens_per_batch, GLA_DV), BF16)],
        compiler_params=pltpu.CompilerParams(
            dimension_semantics=("arbitrary", "arbitrary"), vmem_limit_bytes=VMEM_LIMIT["recurrence"]),
        name="recurrence",
    )(safe, *operands, hg_norm.reshape(1, HG_DV), gla_norm.reshape(1, GLA_DV))
    return y_hg.reshape(HG_HEADS, n, HG_DV), y_gla.reshape(GLA_HEADS, n, GLA_DV)


def _merge_ffn_kernel(x_ref, mod_ref, yh_ref, yg_ref, rh_ref, rg_ref, wuh_ref, wug_ref, wo_ref,
                      gain_ref, w_in_ref, w_down_ref, fgain_ref, o_ref, *, sub_mix, sub_ffn, d_ff, bounds, final):
    gate = mod_ref[0, 3 * sub_mix + 2:3 * sub_mix + 3, :]

    def up(y_ref, w_ref):
        y = jnp.concatenate([y_ref[h].astype(BF16) for h in range(y_ref.shape[0])], axis=-1)
        return _dot(y, w_ref[...])

    merged = (jax.nn.sigmoid(rh_ref[...].astype(F32)) * up(yh_ref, wuh_ref)
              + jax.nn.sigmoid(rg_ref[...].astype(F32)) * up(yg_ref, wug_ref))
    x = x_ref[...] + gate * _dot(merged.astype(BF16), wo_ref[...])
    o_ref[...] = _ffn_body(x, mod_ref, gain_ref, w_in_ref, w_down_ref, fgain_ref,
                           sub=sub_ffn, d_ff=d_ff, bounds=bounds, final=final)


def _merge_ffn(x2d, mod3, y_hg, y_gla, r, w_up_hg, w_up_gla, w_out, gain, w_in_b, w_down_b, d_ff, fgain, *,
               sub_mix, sub_ffn, tokens_per_batch, final, tm=TOKEN_TILE):
    n, d = x2d.shape
    tiles_per_batch = tokens_per_batch // tm
    const = dict(pipeline_mode=pl.Buffered(1))

    return pl.pallas_call(
        functools.partial(_merge_ffn_kernel, sub_mix=sub_mix, sub_ffn=sub_ffn, d_ff=d_ff,
                          bounds=_mxu_aligned_bounds(d_ff, 2), final=final),
        grid=(n // tm,),
        in_specs=[
            pl.BlockSpec((tm, d), lambda i: (i, 0)),
            pl.BlockSpec((1, 3 * N_SUB, d), lambda i: (i // tiles_per_batch, 0, 0)),
            pl.BlockSpec((HG_HEADS, tm, HG_DV), lambda i: (0, i, 0)),
            pl.BlockSpec((GLA_HEADS, tm, GLA_DV), lambda i: (0, i, 0)),
            pl.BlockSpec((tm, d), lambda i: (i, 0)),
            pl.BlockSpec((tm, d), lambda i: (i, 1)),
            pl.BlockSpec(w_up_hg.shape, lambda i: (0, 0), **const),
            pl.BlockSpec(w_up_gla.shape, lambda i: (0, 0), **const),
            pl.BlockSpec(w_out.shape, lambda i: (0, 0), **const),
            pl.BlockSpec((1, d), lambda i: (0, 0)),
            pl.BlockSpec(w_in_b.shape, lambda i: (0, 0), **const),
            pl.BlockSpec(w_down_b.shape, lambda i: (0, 0), **const),
            pl.BlockSpec((1, d), lambda i: (0, 0)),
        ],
        out_specs=pl.BlockSpec((tm, d), lambda i: (i, 0)),
        out_shape=jax.ShapeDtypeStruct((n, d), F32),
        compiler_params=pltpu.CompilerParams(dimension_semantics=("arbitrary",), vmem_limit_bytes=VMEM_LIMIT["merge_ffn"]),
        name="merge_ffn",
    )(x2d, mod3, y_hg, y_gla, r, r, w_up_hg, w_up_gla, w_out, gain.reshape(1, d), w_in_b, w_down_b,
      fgain.reshape(1, d))


def kernel(x, c, w_ada, b_ada, norm_gains, ffn1_w_in, ffn1_w_down, w_in_mix, w_gk_up, b_gk_up, lb_logits,
           hg_norm, gla_norm, w_up_hg, w_up_gla, w_out, ffn2_w_in, ffn2_w_down, final_norm):
    batch, seq, d = x.shape
    depth = w_ada.shape[0]

    x2d = x.reshape(batch * seq, d)
    c_pad = jnp.pad(c, ((0, SUBLANES - batch % SUBLANES if batch % SUBLANES else 0), (0, 0)))
    for l in range(depth):
        mod = _ada(c_pad, w_ada[l], b_ada[l])[:batch].reshape(batch, 3 * N_SUB, d)

        x2d, (w_mix_b, w_in2_b, w_down2_b) = _ffn(
            x2d, mod, norm_gains[l, 0], ffn1_w_in, ffn1_w_down, final_norm,
            (jnp.swapaxes(w_in_mix, 1, 2), ffn2_w_in, ffn2_w_down), layer=l, sub=0, tokens_per_batch=seq, final=False)

        wup_pad = jnp.pad(w_gk_up[l], ((0, LANES - GLA_RANK), (0, 0)))
        dec, act, r, stats = _inproj(x2d, mod, norm_gains[l, 1], w_mix_b, w_in_mix.shape[2], lb_logits, wup_pad,
                                     b_gk_up[l], layer=l, sub=1, tokens_per_batch=seq)

        y_hg, y_gla = _recurrence(dec, act, stats, hg_norm[l], gla_norm[l], batch=batch, tokens_per_batch=seq)

        x2d = _merge_ffn(x2d, mod, y_hg, y_gla, r, w_up_hg[l].astype(BF16), w_up_gla[l].astype(BF16),
                         w_out[l].astype(BF16), norm_gains[l, 2], w_in2_b, w_down2_b, ffn2_w_down.shape[1],
                         final_norm, sub_mix=1, sub_ffn=2, tokens_per_batch=seq, final=l == depth - 1)
    return x2d.reshape(batch, seq, d)
```

```python
import functools

import jax
import jax.numpy as jnp
from jax import lax
from jax.experimental import pallas as pl
from jax.experimental.pallas import tpu as pltpu

F32 = jnp.float32
BF16 = jnp.bfloat16

EPS = 1e-6
N_SUB = 3
HG_HEADS = 4
HG_DK = 128
HG_DV = 128
GLA_HEADS = 4
GLA_DK = 64
GLA_DV = 128
GLA_RANK = 16
GLA_NORMALIZER = 16.0

LANES = 128
SUBLANES = 8
MXU_WIDTH = 256
MIB = 1024 * 1024

TOKEN_TILE = 512
REC_TILE = 512
ADA_STEPS = 8
VMEM_LIMIT = {"ffn": 52 * MIB, "inproj": 56 * MIB, "recurrence": 48 * MIB, "merge_ffn": 58 * MIB}

REC_CHUNK = 64
PERM_STRIDE = 4
PERM_GROUP = PERM_STRIDE * SUBLANES
N_GROUPS = REC_CHUNK // PERM_GROUP
REC_UNROLL = 8
LOG2E = 1.4426950408889634
SAFE_EXP2 = 100.0
SAFE_MAG = 500.0


def _dot(a, b):
    return jnp.dot(a, b, preferred_element_type=F32)


def _dot_nt(a, b):
    return lax.dot_general(a, b, (((1,), (1,)), ((), ())), preferred_element_type=F32)


def _dot_tn(a, b):
    return lax.dot_general(a, b, (((0,), (0,)), ((), ())), preferred_element_type=F32)


def _sigmoid(v):
    return 0.5 * jnp.tanh(0.5 * v) + 0.5


def _silu(v):
    h = 0.5 * v
    return h * jnp.tanh(h) + h


def _norm_mod(x, gain, shift, scale):
    y = x * lax.rsqrt(jnp.mean(x * x, axis=-1, keepdims=True) + EPS) * gain
    return y * (1.0 + scale) + shift


def _ada_kernel(c_ref, w_ref, b_ref, o_ref):
    @pl.when(pl.program_id(0) == 0)
    def _():
        o_ref[...] = jnp.broadcast_to(b_ref[...], o_ref.shape)

    cond = _silu(c_ref[...]).astype(BF16)
    o_ref[...] += _dot(cond, w_ref[...].astype(BF16))


def _ada(c_pad, w, b):
    rows, d = c_pad.shape
    n = w.shape[1]
    kc = d // ADA_STEPS
    return pl.pallas_call(
        _ada_kernel,
        grid=(ADA_STEPS,),
        in_specs=[
            pl.BlockSpec((rows, kc), lambda k: (0, k)),
            pl.BlockSpec((kc, n), lambda k: (k, 0)),
            pl.BlockSpec((1, n), lambda k: (0, 0)),
        ],
        out_specs=pl.BlockSpec((rows, n), lambda k: (0, 0)),
        out_shape=jax.ShapeDtypeStruct((rows, n), F32),
        compiler_params=pltpu.CompilerParams(dimension_semantics=("arbitrary",)),
        name="adaln",
    )(c_pad, w, b.reshape(1, n))


def _mxu_aligned_bounds(width, parts):
    if width % MXU_WIDTH:
        return (0, width)
    tiles = width // MXU_WIDTH
    return tuple(MXU_WIDTH * ((tiles * p + parts - 1) // parts) for p in range(parts)) + (width,)


def _ffn_weight_specs(layer, d, d_ff):
    assert d_ff % MXU_WIDTH == 0
    steps = d_ff // MXU_WIDTH

    def chunk(i):
        return jnp.minimum(i, steps - 1)

    specs = [
        pl.BlockSpec((None, d, MXU_WIDTH), lambda i: (layer, 0, chunk(i))),
        pl.BlockSpec((None, d, MXU_WIDTH), lambda i: (layer, 0, steps + chunk(i))),
        pl.BlockSpec((None, MXU_WIDTH, d), lambda i: (layer, chunk(i), 0)),
    ]
    scratch = [pltpu.VMEM((d, 2 * d_ff), BF16), pltpu.VMEM((d_ff, d), BF16)]
    return steps, specs, scratch


def _stage_ffn_weights(step, wg_ref, wu_ref, wd_ref, w_in_s, w_down_s, d_ff):
    for j in range(d_ff // MXU_WIDTH):
        @pl.when(step == j)
        def _(lo=j * MXU_WIDTH):
            w_in_s[:, lo:lo + MXU_WIDTH] = wg_ref[...].astype(BF16)
            w_in_s[:, d_ff + lo:d_ff + lo + MXU_WIDTH] = wu_ref[...].astype(BF16)
            w_down_s[lo:lo + MXU_WIDTH, :] = wd_ref[...].astype(BF16)


class _SideCast:
    def __init__(self, weight_all, layer, n_tiles):
        self.weight_all, self.layer = weight_all, layer
        self.valid_rows, self.cols = weight_all.shape[1:]
        least = -(-(-(-self.valid_rows // n_tiles)) // 16) * 16
        self.rows = next((r for r in range(least, 2 * least + 1, 16) if self.valid_rows % r == 0), least)
        self.n_chunks = -(-self.valid_rows // self.rows)

    def chunk(self, tile):
        return jnp.minimum(tile, self.n_chunks - 1)

    def in_spec(self, tile_of_step):
        return pl.BlockSpec((None, self.rows, self.cols), lambda i: (self.layer, self.chunk(tile_of_step(i)), 0))

    def out_spec(self, tile_of_step):
        return pl.BlockSpec((self.rows, self.cols), lambda i: (self.chunk(tile_of_step(i)), 0))

    def out_shape(self):
        return jax.ShapeDtypeStruct((self.n_chunks * self.rows, self.cols), BF16)

    def cast(self, tile, in_ref, out_ref):
        w = in_ref[...]
        if self.n_chunks * self.rows > self.valid_rows:
            row = self.chunk(tile) * self.rows + lax.broadcasted_iota(jnp.int32, w.shape, 0)
            w = jnp.where(row < self.valid_rows, w, 0.0)
        out_ref[...] = w.astype(BF16)


def _ffn_kernel(*refs, sub, d_ff, bounds, final, w_steps, sides):
    x_ref, mod_ref, gain_ref, wg_ref, wu_ref, wd_ref, fgain_ref = refs[:7]
    side_in = refs[7:7 + len(sides)]
    o_ref = refs[7 + len(sides)]
    side_out = refs[8 + len(sides):8 + 2 * len(sides)]
    w_in_s, w_down_s = refs[8 + 2 * len(sides):]
    step = pl.program_id(0)
    _stage_ffn_weights(step, wg_ref, wu_ref, wd_ref, w_in_s, w_down_s, d_ff)

    @pl.when(step >= w_steps)
    def _():
        o_ref[...] = _ffn_body(x_ref[...], mod_ref, gain_ref, w_in_s, w_down_s, fgain_ref,
                               sub=sub, d_ff=d_ff, bounds=bounds, final=final)
        for side, in_ref, out_ref in zip(sides, side_in, side_out):
            side.cast(step - w_steps, in_ref, out_ref)


def _ffn_body(x, mod_ref, gain_ref, w_in_ref, w_down_ref, fgain_ref, *, sub, d_ff, bounds, final):
    shift = mod_ref[0, 3 * sub + 0:3 * sub + 1, :]
    scale = mod_ref[0, 3 * sub + 1:3 * sub + 2, :]
    gate = mod_ref[0, 3 * sub + 2:3 * sub + 3, :]
    hb = _norm_mod(x, gain_ref[...], shift, scale).astype(BF16)
    acc = None
    for lo, hi in zip(bounds[:-1], bounds[1:]):
        g = _dot(hb, w_in_ref[:, lo:hi])
        u = _dot(hb, w_in_ref[:, d_ff + lo:d_ff + hi])
        act = (_silu(g) * u).astype(BF16)
        part = _dot(act, w_down_ref[lo:hi, :])
        acc = part if acc is None else acc + part
    xn = x + (0.5 * gate) * acc
    if final:
        xn = xn * lax.rsqrt(jnp.mean(xn * xn, axis=-1, keepdims=True) + EPS) * fgain_ref[...]
    return xn


def _ffn(x2d, mod3, gain, w_in_all, w_down_all, fgain, side_weights, *, layer, sub, tokens_per_batch, final,
         tm=TOKEN_TILE):
    n, d = x2d.shape
    d_ff = w_down_all.shape[1]
    tiles_per_batch = tokens_per_batch // tm
    w_steps, w_specs, w_scratch = _ffn_weight_specs(layer, d, d_ff)
    sides = [_SideCast(w, layer, n // tm) for w in side_weights]

    def tile(i):
        return jnp.maximum(i - w_steps, 0)

    outs = pl.pallas_call(
        functools.partial(_ffn_kernel, sub=sub, d_ff=d_ff, bounds=_mxu_aligned_bounds(d_ff, 2), final=final,
                          w_steps=w_steps, sides=sides),
        grid=(w_steps + n // tm,),
        in_specs=[
            pl.BlockSpec((tm, d), lambda i: (tile(i), 0)),
            pl.BlockSpec((1, 3 * N_SUB, d), lambda i: (tile(i) // tiles_per_batch, 0, 0)),
            pl.BlockSpec((1, d), lambda i: (0, 0)),
            *w_specs,
            pl.BlockSpec((1, d), lambda i: (0, 0)),
            *[s.in_spec(tile) for s in sides],
        ],
        out_specs=[pl.BlockSpec((tm, d), lambda i: (tile(i), 0)), *[s.out_spec(tile) for s in sides]],
        out_shape=[jax.ShapeDtypeStruct((n, d), F32), *[s.out_shape() for s in sides]],
        scratch_shapes=w_scratch,
        compiler_params=pltpu.CompilerParams(dimension_semantics=("arbitrary",), vmem_limit_bytes=VMEM_LIMIT["ffn"]),
        name="ffn_final" if final else "ffn",
    )(x2d, mod3, gain.reshape(1, d), w_in_all, w_in_all, w_down_all, fgain.reshape(1, d), *side_weights)
    return outs[0], outs[1:]


DEC_OFF = {"g": 0, "gd": 4}
DEC_TILES = 6
ACT_OFF = {"q": 0, "k": 4, "v": 8, "sg": 12, "gq": 16, "gk": 18, "gv": 20, "gsg": 24}
ACT_TILES = 28


def _inproj_kernel(x_ref, mod_ref, gain_ref, wt_ref, lbl_ref, wup_ref, bup_ref, dec_ref, act_ref, r_ref, stats_ref, *,
                   sub, layer):
    shift = mod_ref[0, 3 * sub + 0:3 * sub + 1, :]
    scale = mod_ref[0, 3 * sub + 1:3 * sub + 2, :]
    hb = _norm_mod(x_ref[...], gain_ref[...], shift, scale).astype(BF16)
    tm = hb.shape[0]
    hg_qk, hg_w = HG_HEADS * HG_DK, HG_HEADS * HG_DV
    gla_k, gla_v = GLA_HEADS * GLA_DK, GLA_HEADS * GLA_DV
    n_hg = 2 * hg_qk + 2 * hg_w
    n_rec = n_hg + 2 * gla_k + 2 * gla_v

    def put(name, val):
        if name in DEC_OFF:
            lo = DEC_OFF[name] * LANES
            dec_ref[:, lo:lo + val.shape[1]] = val
        else:
            lo = ACT_OFF[name] * LANES
            val = val.astype(act_ref.dtype)
            act_ref[:, lo:lo + val.shape[1]] = val
        return val

    def chunk_min(g):
        tot = jnp.sum(g.reshape(tm // REC_CHUNK, REC_CHUNK, g.shape[1]), axis=1)
        return functools.reduce(jnp.minimum, [tot[:, j:j + LANES] for j in range(0, g.shape[1], LANES)])

    def mag_max(v):
        rows = 2 * SUBLANES
        m = jnp.max(jnp.abs(v).reshape(tm // rows, rows, v.shape[1]), axis=0)
        m = functools.reduce(jnp.maximum, [m[:, j:j + LANES] for j in range(0, v.shape[1], LANES)]).astype(F32)
        return jnp.maximum(m[:SUBLANES], m[SUBLANES:])

    ph = _dot_nt(hb, wt_ref[:n_hg, :])
    lbl = lbl_ref[...]
    ex = jnp.exp(lbl - jnp.max(lbl, axis=0, keepdims=True))
    lb = jnp.sum(ex[:layer + 1], axis=0, keepdims=True) / jnp.sum(ex, axis=0, keepdims=True)
    c1 = 0.5 * (1.0 - lb)
    q = _silu(ph[:, :hg_qk])
    p = c1 * jnp.tanh(0.5 * ph[:, hg_qk:2 * hg_qk])
    g = jnp.log2((1.0 - c1) + p)
    q_b = put("q", q)
    put("k", c1 - p)
    put("g", g)
    v_b = put("v", ph[:, 2 * hg_qk:2 * hg_qk + hg_w])
    put("sg", _silu(ph[:, 2 * hg_qk + hg_w:]))

    pg = _dot_nt(hb, wt_ref[n_hg:n_rec, :])
    code = _dot_nt(hb, wt_ref[n_rec:n_rec + LANES, :])
    logits = _dot(code.astype(BF16), wup_ref[...].astype(BF16)) + bup_ref[...]
    gd = (jnp.minimum(logits, 0.0) - jnp.log(1.0 + jnp.exp(-jnp.abs(logits)))) * (LOG2E / GLA_NORMALIZER)
    gq = pg[:, :gla_k] * (GLA_DK ** -0.5)
    gk = pg[:, gla_k:2 * gla_k]
    gq_b = put("gq", gq)
    gk_b = put("gk", gk)
    put("gd", gd)
    gv_b = put("gv", pg[:, 2 * gla_k:2 * gla_k + gla_v])
    put("gsg", _silu(pg[:, 2 * gla_k + gla_v:]))

    r_ref[...] = _dot_nt(hb, wt_ref[n_rec + GLA_RANK:n_rec + GLA_RANK + r_ref.shape[1], :]).astype(r_ref.dtype)
    stats_ref[0, 0] = jnp.minimum(chunk_min(g), chunk_min(gd))
    stats_ref[0, 1] = functools.reduce(jnp.maximum, [mag_max(a) for a in (q_b, v_b, gq_b, gk_b, gv_b)])


def _inproj(x2d, mod3, gain, wt_b, n_channels, lb_logits, wup_pad, bup, *, layer, sub, tokens_per_batch,
            tm=TOKEN_TILE):
    n, d = x2d.shape
    n_rec = 2 * HG_HEADS * HG_DK + 2 * HG_HEADS * HG_DV + 2 * GLA_HEADS * GLA_DK + 2 * GLA_HEADS * GLA_DV
    n_r = n_channels - n_rec - GLA_RANK
    assert n_rec + LANES <= wt_b.shape[0] and tm // REC_CHUNK == SUBLANES
    tiles_per_batch = tokens_per_batch // tm

    return pl.pallas_call(
        functools.partial(_inproj_kernel, sub=sub, layer=layer),
        grid=(n // tm,),
        in_specs=[
            pl.BlockSpec((tm, d), lambda i: (i, 0)),
            pl.BlockSpec((1, 3 * N_SUB, d), lambda i: (i // tiles_per_batch, 0, 0)),
            pl.BlockSpec((1, d), lambda i: (0, 0)),
            pl.BlockSpec(wt_b.shape, lambda i: (0, 0), pipeline_mode=pl.Buffered(1)),
            pl.BlockSpec(lb_logits.shape, lambda i: (0, 0)),
            pl.BlockSpec(wup_pad.shape, lambda i: (0, 0)),
            pl.BlockSpec((1, bup.shape[0]), lambda i: (0, 0)),
        ],
        out_specs=[
            pl.BlockSpec((tm, DEC_TILES * LANES), lambda i: (i, 0)),
            pl.BlockSpec((tm, ACT_TILES * LANES), lambda i: (i, 0)),
            pl.BlockSpec((tm, n_r), lambda i: (i, 0)),
            pl.BlockSpec((1, 2, SUBLANES, LANES), lambda i: (i, 0, 0, 0)),
        ],
        out_shape=[
            jax.ShapeDtypeStruct((n, DEC_TILES * LANES), F32),
            jax.ShapeDtypeStruct((n, ACT_TILES * LANES), BF16),
            jax.ShapeDtypeStruct((n, n_r), BF16),
            jax.ShapeDtypeStruct((n // tm, 2, SUBLANES, LANES), F32),
        ],
        compiler_params=pltpu.CompilerParams(dimension_semantics=("arbitrary",), vmem_limit_bytes=VMEM_LIMIT["inproj"]),
        name="inproj",
    )(x2d, mod3, gain.reshape(1, d), wt_b, lb_logits, wup_pad, bup.reshape(1, -1))


def _load_perm(ref, base):
    return [ref[pl.ds(base + PERM_GROUP * c + a, SUBLANES, stride=PERM_STRIDE), :]
            for c in range(N_GROUPS) for a in range(PERM_STRIDE)]


def _store_perm(ref, base, val):
    for c in range(N_GROUPS):
        for a in range(PERM_STRIDE):
            u = PERM_STRIDE * c + a
            ref[pl.ds(base + PERM_GROUP * c + a, SUBLANES, stride=PERM_STRIDE), :] = (
                val[SUBLANES * u:SUBLANES * (u + 1), :])


def _shift_down(v, d, sub_iota):
    return jnp.where(sub_iota >= d, pltpu.roll(v, d, 0), 0.0)


def _cumsum_perm(g, sub_iota):
    out = []
    carry = None
    for c in range(N_GROUPS):
        pre = [g[PERM_STRIDE * c]]
        for a in range(1, PERM_STRIDE):
            pre.append(pre[-1] + g[PERM_STRIDE * c + a])
        inc = pre[-1]
        inc = inc + _shift_down(inc, 1, sub_iota)
        inc = inc + _shift_down(inc, 2, sub_iota)
        inc = inc + _shift_down(inc, 4, sub_iota)
        exc = _shift_down(inc, 1, sub_iota)
        if carry is not None:
            exc = exc + carry
        out.extend(p + exc for p in pre)
        tot = inc[SUBLANES - 1:SUBLANES, :]
        carry = tot if carry is None else carry + tot
    return jnp.concatenate(out, axis=0), carry


def _finish(o, gain, act_gate):
    return o * lax.rsqrt(jnp.mean(o * o, axis=-1, keepdims=True) + EPS) * gain * act_gate


def _rec_kernel(safe_ref, q0, q1, k0, k1, g0, g1, v0, v1, sg0, sg1, gq, gk, gd, gv0, gv1, gsg0, gsg1,
                hgn_ref, glan_ref, yh_ref, yg_ref, st_s, e_s, fb_s, oi_s, *, tt, unroll):
    nc = tt // REC_CHUNK
    nb = q0.shape[0]

    @pl.when(pl.program_id(1) == 0)
    def _():
        st_s[...] = jnp.zeros_like(st_s)

    safe = safe_ref[pl.program_id(1)] != 0

    sub_iota = lax.broadcasted_iota(jnp.int32, (SUBLANES, LANES), 0)
    lane = lax.broadcasted_iota(jnp.int32, (REC_CHUNK, LANES), 1)
    causal = (lax.broadcasted_iota(jnp.int32, (REC_CHUNK, REC_CHUNK), 1)
              <= lax.broadcasted_iota(jnp.int32, (REC_CHUNK, REC_CHUNK), 0))
    tok_of_row = lax.broadcasted_iota(jnp.int32, (REC_CHUNK, LANES), 0)
    hgn = hgn_ref[...]
    glan = glan_ref[...]

    head_masks = (lane < GLA_DK, lane >= GLA_DK)
    groups = []
    for bi in range(nb):
        groups += [
            ((q0.at[bi], k0.at[bi], g0.at[bi]), st_s.at[bi, 0],
             [(None, v0.at[bi], sg0.at[bi], hgn, yh_ref.at[0, bi])]),
            ((q1.at[bi], k1.at[bi], g1.at[bi]), st_s.at[bi, 1],
             [(None, v1.at[bi], sg1.at[bi], hgn, yh_ref.at[1, bi])]),
            ((gq.at[bi], gk.at[bi], gd.at[bi]), st_s.at[bi, 2],
             [(head_masks[0], gv0.at[bi], gsg0.at[bi], glan, yg_ref.at[0, bi]),
              (head_masks[1], gv1.at[bi], gsg1.at[bi], glan, yg_ref.at[1, bi])]),
        ]

    def natural(ref, base):
        return ref[pl.ds(base, REC_CHUNK), :]

    def decay_logs(g_ref, base):
        return _cumsum_perm(_load_perm(g_ref, base), sub_iota)

    def to_natural(slot, val):
        _store_perm(e_s.at[slot], 0, val)
        return e_s[slot]

    def masked(x, mask):
        return x if mask is None else jnp.where(mask, x, jnp.zeros_like(x))

    def fast_chunk(ci, carry):
        staged = []
        for sub in range(unroll):
            base = pl.multiple_of((ci * unroll + sub) * REC_CHUNK, REC_CHUNK)
            for gi, ((q_ref, k_ref, g_ref), st_view, heads) in enumerate(groups):
                b, b_last = decay_logs(g_ref, base)
                half = 0.5 * b_last
                d = b - half
                em = jnp.exp2(half)
                slot = 2 * (sub * len(groups) + gi)
                qt_b = (natural(q_ref, base).astype(F32) * to_natural(slot, jnp.exp2(d))).astype(BF16)
                kt_b = (natural(k_ref, base).astype(F32) * to_natural(slot + 1, jnp.exp2(-d))).astype(BF16)
                vbs = [natural(v_ref, base) for _, v_ref, _, _, _ in heads]
                q_all = jnp.concatenate([masked(qt_b, m) for m, *_ in heads], axis=0)
                k_all = jnp.concatenate([masked(kt_b, m) for m, *_ in heads], axis=0)
                v_all = jnp.concatenate(vbs, axis=0)
                state_e = st_view[...] * em
                scores = _dot_nt(q_all, kt_b)
                o_inter = _dot_nt(q_all, state_e.astype(BF16))
                st_view[...] = (state_e + _dot_tn(v_all, k_all)) * em
                for h, (_, _, gate_ref, gain, y_view) in enumerate(heads):
                    rows = slice(h * REC_CHUNK, (h + 1) * REC_CHUNK)
                    staged.append((scores[rows], o_inter[rows], vbs[h], base, gate_ref, gain, y_view))
        for scores, o_inter, vb, base, gate_ref, gain, y_view in staged:
            s = jnp.where(causal, scores, 0.0).astype(BF16)
            o = o_inter + _dot(s, vb)
            gate = natural(gate_ref, base).astype(F32)
            y_view[pl.ds(base, REC_CHUNK), :] = _finish(o, gain, gate).astype(y_view.dtype)
        return carry

    def exact_chunk(ci, carry):
        base = pl.multiple_of(ci * REC_CHUNK, REC_CHUNK)
        for (q_ref, k_ref, g_ref), st_view, heads in groups:
            b_perm, b_last = decay_logs(g_ref, base)
            b = to_natural(0, b_perm)
            q_all = natural(q_ref, base).astype(F32)
            k = natural(k_ref, base).astype(F32)
            state = st_view[...]
            new_state = state * jnp.exp2(b_last)
            kd = k * jnp.exp2(b_last - b)
            for qmask, v_ref, gate_ref, gain, y_view in heads:
                q = masked(q_all, qmask)
                vb = natural(v_ref, base)
                gate = natural(gate_ref, base).astype(F32)
                fb_s[0] = b
                fb_s[1] = q
                fb_s[2] = k
                fb_s[3] = vb.astype(F32)

                def row(r, c2):
                    bt = fb_s[0, pl.ds(r, 1), :]
                    qt = fb_s[1, pl.ds(r, 1), :]
                    w = jnp.exp2(jnp.where(tok_of_row <= r, bt - fb_s[0], -jnp.inf))
                    sc = jnp.sum(qt * w * fb_s[2], axis=-1, keepdims=True)
                    oi_s[pl.ds(r, 1), :] = jnp.sum(sc * fb_s[3], axis=0, keepdims=True)
                    return c2

                lax.fori_loop(0, REC_CHUNK, row, 0)
                o = _dot_nt((q * jnp.exp2(b)).astype(BF16), state.astype(BF16)) + oi_s[...]
                new_state = new_state + _dot_tn(vb, masked(kd, qmask).astype(BF16))
                y_view[pl.ds(base, REC_CHUNK), :] = _finish(o, gain, gate).astype(y_view.dtype)
            st_view[...] = new_state
        return carry

    @pl.when(safe)
    def _():
        lax.fori_loop(0, nc // unroll, fast_chunk, 0)

    @pl.when(jnp.logical_not(safe))
    def _():
        lax.fori_loop(0, nc, exact_chunk, 0)


def _safe_flags(stats, *, batch, tokens_per_batch, tt):
    tiles_per_batch = stats.shape[0] // batch
    chunk_tot = jnp.min(stats[:, 0], axis=-1).reshape(batch, tokens_per_batch // tt, -1)
    tile_mag = jnp.max(stats[:, 1], axis=(-2, -1)).reshape(batch, tiles_per_batch)
    step_tot = jnp.min(chunk_tot, axis=(0, 2))
    step_mag = jnp.repeat(jnp.max(tile_mag, axis=0), (tokens_per_batch // tt) // tiles_per_batch)
    return jnp.logical_and(step_tot >= -2.0 * SAFE_EXP2, step_mag <= SAFE_MAG).astype(jnp.int32)


def _recurrence(dec, act, stats, hg_norm, gla_norm, *, batch, tokens_per_batch, tt=REC_TILE):
    n = dec.shape[0]
    nt = tokens_per_batch // tt
    pairs = HG_HEADS // 2
    dec3 = dec.reshape(batch, tokens_per_batch, dec.shape[1])
    act3 = act.reshape(batch, tokens_per_batch, act.shape[1])
    safe = _safe_flags(stats, batch=batch, tokens_per_batch=tokens_per_batch, tt=tt)

    def tile_of(name, p, e):
        off = DEC_OFF[name] if name in DEC_OFF else ACT_OFF[name]
        return off + (p if e is None else 2 * p + e)

    def spec(name, e=None):
        return pl.BlockSpec((batch, tt, LANES), lambda p, i, flags: (0, i, tile_of(name, p, e)))

    names = [("q", 0), ("q", 1), ("k", 0), ("k", 1), ("g", 0), ("g", 1), ("v", 0), ("v", 1), ("sg", 0), ("sg", 1),
             ("gq", None), ("gk", None), ("gd", None), ("gv", 0), ("gv", 1), ("gsg", 0), ("gsg", 1)]
    in_specs = [spec(name, e) for name, e in names] + [
        pl.BlockSpec((1, LANES), lambda p, i, flags: (0, 0)),
        pl.BlockSpec((1, LANES), lambda p, i, flags: (0, 0)),
    ]
    operands = [dec3 if name in DEC_OFF else act3 for name, _ in names]
    n_groups = 3 * batch
    out_spec = pl.BlockSpec((2, batch, tt, LANES), lambda p, i, flags: (p, 0, i, 0))
    y_hg, y_gla = pl.pallas_call(
        functools.partial(_rec_kernel, tt=tt, unroll=REC_UNROLL),
        grid_spec=pltpu.PrefetchScalarGridSpec(
            num_scalar_prefetch=1,
            grid=(pairs, nt),
            in_specs=in_specs,
            out_specs=[out_spec, out_spec],
            scratch_shapes=[
                pltpu.VMEM((batch, 3, LANES, LANES), F32),
                pltpu.VMEM((2 * REC_UNROLL * n_groups, REC_CHUNK, LANES), F32),
                pltpu.VMEM((4, REC_CHUNK, LANES), F32),
                pltpu.VMEM((REC_CHUNK, LANES), F32),
            ],
        ),
        out_shape=[jax.ShapeDtypeStruct((HG_HEADS, batch, tokens_per_batch, HG_DV), BF16),
                   jax.ShapeDtypeStruct((GLA_HEADS, batch, tokens_per_batch, GLA_DV), BF16)],
        compiler_params=pltpu.CompilerParams(
            dimension_semantics=("arbitrary", "arbitrary"), vmem_limit_bytes=VMEM_LIMIT["recurrence"]),
        name="recurrence",
    )(safe, *operands, hg_norm.reshape(1, HG_DV), gla_norm.reshape(1, GLA_DV))
    return y_hg.reshape(HG_HEADS, n, HG_DV), y_gla.reshape(GLA_HEADS, n, GLA_DV)


def _merge_ffn_kernel(x_ref, mod_ref, yh_ref, yg_ref, rh_ref, rg_ref, wuh_ref, wug_ref, wo_ref,
                      gain_ref, w_in_ref, w_down_ref, fgain_ref, o_ref, *, sub_mix, sub_ffn, d_ff, bounds, final):
    gate = mod_ref[0, 3 * sub_mix + 2:3 * sub_mix + 3, :]

    def up(y_ref, w_ref):
        y = jnp.concatenate([y_ref[h].astype(BF16) for h in range(y_ref.shape[0])], axis=-1)
        return _dot(y, w_ref[...])

    merged = (_sigmoid(rh_ref[...].astype(F32)) * up(yh_ref, wuh_ref)
              + _sigmoid(rg_ref[...].astype(F32)) * up(yg_ref, wug_ref))
    x = x_ref[...] + gate * _dot(merged.astype(BF16), wo_ref[...])
    o_ref[...] = _ffn_body(x, mod_ref, gain_ref, w_in_ref, w_down_ref, fgain_ref,
                           sub=sub_ffn, d_ff=d_ff, bounds=bounds, final=final)


def _merge_ffn(x2d, mod3, y_hg, y_gla, r, w_up_hg, w_up_gla, w_out, gain, w_in_b, w_down_b, d_ff, fgain, *,
               sub_mix, sub_ffn, tokens_per_batch, final, tm=TOKEN_TILE):
    n, d = x2d.shape
    tiles_per_batch = tokens_per_batch // tm
    const = dict(pipeline_mode=pl.Buffered(1))

    return pl.pallas_call(
        functools.partial(_merge_ffn_kernel, sub_mix=sub_mix, sub_ffn=sub_ffn, d_ff=d_ff,
                          bounds=_mxu_aligned_bounds(d_ff, 2), final=final),
        grid=(n // tm,),
        in_specs=[
            pl.BlockSpec((tm, d), lambda i: (i, 0)),
            pl.BlockSpec((1, 3 * N_SUB, d), lambda i: (i // tiles_per_batch, 0, 0)),
            pl.BlockSpec((HG_HEADS, tm, HG_DV), lambda i: (0, i, 0)),
            pl.BlockSpec((GLA_HEADS, tm, GLA_DV), lambda i: (0, i, 0)),
            pl.BlockSpec((tm, d), lambda i: (i, 0)),
            pl.BlockSpec((tm, d), lambda i: (i, 1)),
            pl.BlockSpec(w_up_hg.shape, lambda i: (0, 0), **const),
            pl.BlockSpec(w_up_gla.shape, lambda i: (0, 0), **const),
            pl.BlockSpec(w_out.shape, lambda i: (0, 0), **const),
            pl.BlockSpec((1, d), lambda i: (0, 0)),
            pl.BlockSpec(w_in_b.shape, lambda i: (0, 0), **const),
            pl.BlockSpec(w_down_b.shape, lambda i: (0, 0), **const),
            pl.BlockSpec((1, d), lambda i: (0, 0)),
        ],
        out_specs=pl.BlockSpec((tm, d), lambda i: (i, 0)),
        out_shape=jax.ShapeDtypeStruct((n, d), F32),
        compiler_params=pltpu.CompilerParams(dimension_semantics=("arbitrary",), vmem_limit_bytes=VMEM_LIMIT["merge_ffn"]),
        name="merge_ffn",
    )(x2d, mod3, y_hg, y_gla, r, r, w_up_hg, w_up_gla, w_out, gain.reshape(1, d), w_in_b, w_down_b,
      fgain.reshape(1, d))


def kernel(x, c, w_ada, b_ada, norm_gains, ffn1_w_in, ffn1_w_down, w_in_mix, w_gk_up, b_gk_up, lb_logits,
           hg_norm, gla_norm, w_up_hg, w_up_gla, w_out, ffn2_w_in, ffn2_w_down, final_norm):
    batch, seq, d = x.shape
    depth = w_ada.shape[0]

    x2d = x.reshape(batch * seq, d)
    c_pad = jnp.pad(c, ((0, SUBLANES - batch % SUBLANES if batch % SUBLANES else 0), (0, 0)))
    for l in range(depth):
        mod = _ada(c_pad, w_ada[l], b_ada[l])[:batch].reshape(batch, 3 * N_SUB, d)

        x2d, (w_mix_b, w_in2_b, w_down2_b) = _ffn(
            x2d, mod, norm_gains[l, 0], ffn1_w_in, ffn1_w_down, final_norm,
            (jnp.swapaxes(w_in_mix, 1, 2), ffn2_w_in, ffn2_w_down), layer=l, sub=0, tokens_per_batch=seq, final=False)

        wup_pad = jnp.pad(w_gk_up[l], ((0, LANES - GLA_RANK), (0, 0)))
        dec, act, r, stats = _inproj(x2d, mod, norm_gains[l, 1], w_mix_b, w_in_mix.shape[2], lb_logits, wup_pad,
                                     b_gk_up[l], layer=l, sub=1, tokens_per_batch=seq)

        y_hg, y_gla = _recurrence(dec, act, stats, hg_norm[l], gla_norm[l], batch=batch, tokens_per_batch=seq)

        x2d = _merge_ffn(x2d, mod, y_hg, y_gla, r, w_up_hg[l].astype(BF16), w_up_gla[l].astype(BF16),
                         w_out[l].astype(BF16), norm_gains[l, 2], w_in2_b, w_down2_b, ffn2_w_down.shape[1],
                         final_norm, sub_mix=1, sub_ffn=2, tokens_per_batch=seq, final=l == depth - 1)
    return x2d.reshape(batch, seq, d)
```

```python
import functools

import jax
import jax.numpy as jnp
from jax import lax
from jax.experimental import pallas as pl
from jax.experimental.pallas import tpu as pltpu

F32 = jnp.float32
BF16 = jnp.bfloat16

EPS = 1e-6
N_SUB = 3
HG_HEADS = 4
HG_DK = 128
HG_DV = 128
GLA_HEADS = 4
GLA_DK = 64
GLA_DV = 128
GLA_RANK = 16
GLA_NORMALIZER = 16.0

LANES = 128
SUBLANES = 8
MXU_WIDTH = 256
GATE_PRESCALE = 0.5
MIB = 1024 * 1024

TOKEN_TILE = 512
REC_TILE = 512
ADA_STEPS = 8
VMEM_LIMIT = {"ffn": 52 * MIB, "inproj": 56 * MIB, "recurrence": 48 * MIB, "merge_ffn": 58 * MIB}

REC_CHUNK = 64
PERM_STRIDE = 4
PERM_GROUP = PERM_STRIDE * SUBLANES
N_GROUPS = REC_CHUNK // PERM_GROUP
REC_UNROLL = 8
LOG2E = 1.4426950408889634
SAFE_EXP2 = 100.0
SAFE_MAG = 500.0


def _dot(a, b):
    return jnp.dot(a, b, preferred_element_type=F32)


def _dot_nt(a, b):
    return lax.dot_general(a, b, (((1,), (1,)), ((), ())), preferred_element_type=F32)


def _dot_tn(a, b):
    return lax.dot_general(a, b, (((0,), (0,)), ((), ())), preferred_element_type=F32)


def _silu(v):
    h = 0.5 * v
    return h * jnp.tanh(h) + h


def _norm_mod(x, gain, shift, scale):
    return x * lax.rsqrt(jnp.mean(x * x, axis=-1, keepdims=True) + EPS) * (gain * (1.0 + scale)) + shift


def _ada_kernel(c_ref, w_ref, b_ref, o_ref):
    @pl.when(pl.program_id(0) == 0)
    def _():
        o_ref[...] = jnp.broadcast_to(b_ref[...], o_ref.shape)

    cond = _silu(c_ref[...]).astype(BF16)
    o_ref[...] += _dot(cond, w_ref[...].astype(BF16))


def _ada(c_pad, w, b):
    rows, d = c_pad.shape
    n = w.shape[1]
    kc = d // ADA_STEPS
    return pl.pallas_call(
        _ada_kernel,
        grid=(ADA_STEPS,),
        in_specs=[
            pl.BlockSpec((rows, kc), lambda k: (0, k)),
            pl.BlockSpec((kc, n), lambda k: (k, 0)),
            pl.BlockSpec((1, n), lambda k: (0, 0)),
        ],
        out_specs=pl.BlockSpec((rows, n), lambda k: (0, 0)),
        out_shape=jax.ShapeDtypeStruct((rows, n), F32),
        compiler_params=pltpu.CompilerParams(dimension_semantics=("arbitrary",)),
        name="adaln",
    )(c_pad, w, b.reshape(1, n))


def _mxu_aligned_bounds(width, parts):
    if width % MXU_WIDTH:
        return (0, width)
    tiles = width // MXU_WIDTH
    return tuple(MXU_WIDTH * ((tiles * p + parts - 1) // parts) for p in range(parts)) + (width,)


def _ffn_weight_specs(layer, d, d_ff):
    assert d_ff % MXU_WIDTH == 0
    steps = d_ff // MXU_WIDTH

    def chunk(i):
        return jnp.minimum(i, steps - 1)

    specs = [
        pl.BlockSpec((None, d, MXU_WIDTH), lambda i: (layer, 0, chunk(i))),
        pl.BlockSpec((None, d, MXU_WIDTH), lambda i: (layer, 0, steps + chunk(i))),
        pl.BlockSpec((None, MXU_WIDTH, d), lambda i: (layer, chunk(i), 0)),
    ]
    scratch = [pltpu.VMEM((d, 2 * d_ff), BF16), pltpu.VMEM((d_ff, d), BF16)]
    return steps, specs, scratch


def _stage_ffn_weights(step, wg_ref, wu_ref, wd_ref, w_in_s, w_down_s, d_ff):
    for j in range(d_ff // MXU_WIDTH):
        @pl.when(step == j)
        def _(lo=j * MXU_WIDTH):
            w_in_s[:, lo:lo + MXU_WIDTH] = (GATE_PRESCALE * wg_ref[...]).astype(BF16)
            w_in_s[:, d_ff + lo:d_ff + lo + MXU_WIDTH] = wu_ref[...].astype(BF16)
            w_down_s[lo:lo + MXU_WIDTH, :] = wd_ref[...].astype(BF16)


class _SideCast:
    def __init__(self, weight_all, layer, n_tiles, prescaled_cols=0):
        self.weight_all, self.layer, self.prescaled_cols = weight_all, layer, prescaled_cols
        self.valid_rows, self.cols = weight_all.shape[1:]
        least = -(-(-(-self.valid_rows // n_tiles)) // 16) * 16
        self.rows = next((r for r in range(least, 2 * least + 1, 16) if self.valid_rows % r == 0), least)
        self.n_chunks = -(-self.valid_rows // self.rows)

    def chunk(self, tile):
        return jnp.minimum(tile, self.n_chunks - 1)

    def in_spec(self, tile_of_step):
        return pl.BlockSpec((None, self.rows, self.cols), lambda i: (self.layer, self.chunk(tile_of_step(i)), 0))

    def out_spec(self, tile_of_step):
        return pl.BlockSpec((self.rows, self.cols), lambda i: (self.chunk(tile_of_step(i)), 0))

    def out_shape(self):
        return jax.ShapeDtypeStruct((self.n_chunks * self.rows, self.cols), BF16)

    def cast(self, tile, in_ref, out_ref):
        w = in_ref[...]
        if self.n_chunks * self.rows > self.valid_rows:
            row = self.chunk(tile) * self.rows + lax.broadcasted_iota(jnp.int32, w.shape, 0)
            w = jnp.where(row < self.valid_rows, w, 0.0)
        if self.prescaled_cols:
            col = lax.broadcasted_iota(jnp.int32, (1, self.cols), 1)
            w = w * jnp.where(col < self.prescaled_cols, GATE_PRESCALE, 1.0)
        out_ref[...] = w.astype(BF16)


def _ffn_kernel(*refs, sub, d_ff, bounds, final, w_steps, sides):
    x_ref, mod_ref, gain_ref, wg_ref, wu_ref, wd_ref, fgain_ref = refs[:7]
    side_in = refs[7:7 + len(sides)]
    o_ref = refs[7 + len(sides)]
    side_out = refs[8 + len(sides):8 + 2 * len(sides)]
    w_in_s, w_down_s = refs[8 + 2 * len(sides):]
    step = pl.program_id(0)
    _stage_ffn_weights(step, wg_ref, wu_ref, wd_ref, w_in_s, w_down_s, d_ff)

    @pl.when(step >= w_steps)
    def _():
        o_ref[...] = _ffn_body(x_ref[...], mod_ref, gain_ref, w_in_s, w_down_s, fgain_ref,
                               sub=sub, d_ff=d_ff, bounds=bounds, final=final)
        for side, in_ref, out_ref in zip(sides, side_in, side_out):
            side.cast(step - w_steps, in_ref, out_ref)


def _ffn_body(x, mod_ref, gain_ref, w_in_ref, w_down_ref, fgain_ref, *, sub, d_ff, bounds, final):
    shift = mod_ref[0, 3 * sub + 0:3 * sub + 1, :]
    scale = mod_ref[0, 3 * sub + 1:3 * sub + 2, :]
    gate = mod_ref[0, 3 * sub + 2:3 * sub + 3, :]
    hb = _norm_mod(x, gain_ref[...], shift, scale).astype(BF16)
    acc = None
    for lo, hi in zip(bounds[:-1], bounds[1:]):
        h = _dot(hb, w_in_ref[:, lo:hi])
        u = _dot(hb, w_in_ref[:, d_ff + lo:d_ff + hi])
        h, u = h.astype(BF16), u.astype(BF16)
        act = (h * jnp.tanh(h) + h) * u
        part = _dot(act, w_down_ref[lo:hi, :])
        acc = part if acc is None else acc + part
    xn = x + (0.5 * gate) * acc
    if final:
        xn = xn * lax.rsqrt(jnp.mean(xn * xn, axis=-1, keepdims=True) + EPS) * fgain_ref[...]
    return xn


def _ffn(x2d, mod3, gain, w_in_all, w_down_all, fgain, side_weights, *, layer, sub, tokens_per_batch, final,
         tm=TOKEN_TILE):
    n, d = x2d.shape
    d_ff = w_down_all.shape[1]
    tiles_per_batch = tokens_per_batch // tm
    w_steps, w_specs, w_scratch = _ffn_weight_specs(layer, d, d_ff)
    sides = [_SideCast(w, layer, n // tm, prescaled) for w, prescaled in side_weights]

    def tile(i):
        return jnp.maximum(i - w_steps, 0)

    outs = pl.pallas_call(
        functools.partial(_ffn_kernel, sub=sub, d_ff=d_ff, bounds=_mxu_aligned_bounds(d_ff, 2), final=final,
                          w_steps=w_steps, sides=sides),
        grid=(w_steps + n // tm,),
        in_specs=[
            pl.BlockSpec((tm, d), lambda i: (tile(i), 0)),
            pl.BlockSpec((1, 3 * N_SUB, d), lambda i: (tile(i) // tiles_per_batch, 0, 0)),
            pl.BlockSpec((1, d), lambda i: (0, 0)),
            *w_specs,
            pl.BlockSpec((1, d), lambda i: (0, 0)),
            *[s.in_spec(tile) for s in sides],
        ],
        out_specs=[pl.BlockSpec((tm, d), lambda i: (tile(i), 0)), *[s.out_spec(tile) for s in sides]],
        out_shape=[jax.ShapeDtypeStruct((n, d), F32), *[s.out_shape() for s in sides]],
        scratch_shapes=w_scratch,
        compiler_params=pltpu.CompilerParams(dimension_semantics=("arbitrary",), vmem_limit_bytes=VMEM_LIMIT["ffn"]),
        name="ffn_final" if final else "ffn",
    )(x2d, mod3, gain.reshape(1, d), w_in_all, w_in_all, w_down_all, fgain.reshape(1, d),
      *[w for w, _ in side_weights])
    return outs[0], outs[1:]


DEC_OFF = {"g": 0, "gd": 4}
DEC_TILES = 6
ACT_OFF = {"q": 0, "k": 4, "v": 8, "sg": 12, "gq": 16, "gk": 18, "gv": 20, "gsg": 24}
ACT_TILES = 28


def _inproj_kernel(x_ref, mod_ref, gain_ref, wt_ref, lbl_ref, wup_ref, bup_ref, dec_ref, act_ref, r_ref, stats_ref, *,
                   sub, layer):
    shift = mod_ref[0, 3 * sub + 0:3 * sub + 1, :]
    scale = mod_ref[0, 3 * sub + 1:3 * sub + 2, :]
    hb = _norm_mod(x_ref[...], gain_ref[...], shift, scale).astype(BF16)
    tm = hb.shape[0]
    hg_qk, hg_w = HG_HEADS * HG_DK, HG_HEADS * HG_DV
    gla_k, gla_v = GLA_HEADS * GLA_DK, GLA_HEADS * GLA_DV
    n_hg = 2 * hg_qk + 2 * hg_w
    n_rec = n_hg + 2 * gla_k + 2 * gla_v

    def put(name, val):
        if name in DEC_OFF:
            lo = DEC_OFF[name] * LANES
            dec_ref[:, lo:lo + val.shape[1]] = val
        else:
            lo = ACT_OFF[name] * LANES
            val = val.astype(act_ref.dtype)
            act_ref[:, lo:lo + val.shape[1]] = val
        return val

    def chunk_min(g):
        tot = jnp.sum(g.reshape(tm // REC_CHUNK, REC_CHUNK, g.shape[1]), axis=1)
        return functools.reduce(jnp.minimum, [tot[:, j:j + LANES] for j in range(0, g.shape[1], LANES)])

    def mag_max(v):
        rows = 2 * SUBLANES
        m = jnp.max(jnp.abs(v).reshape(tm // rows, rows, v.shape[1]), axis=0)
        m = functools.reduce(jnp.maximum, [m[:, j:j + LANES] for j in range(0, v.shape[1], LANES)]).astype(F32)
        return jnp.maximum(m[:SUBLANES], m[SUBLANES:])

    ph = _dot_nt(hb, wt_ref[:n_hg, :])
    lbl = lbl_ref[...]
    ex = jnp.exp(lbl - jnp.max(lbl, axis=0, keepdims=True))
    lb = jnp.sum(ex[:layer + 1], axis=0, keepdims=True) / jnp.sum(ex, axis=0, keepdims=True)
    c1 = 0.5 * (1.0 - lb)
    q = _silu(ph[:, :hg_qk])
    p = c1 * jnp.tanh(0.5 * ph[:, hg_qk:2 * hg_qk])
    g = jnp.log2((1.0 - c1) + p)
    q_b = put("q", q)
    put("k", c1 - p)
    put("g", g)
    v_b = put("v", ph[:, 2 * hg_qk:2 * hg_qk + hg_w])
    put("sg", _silu(ph[:, 2 * hg_qk + hg_w:]))

    pg = _dot_nt(hb, wt_ref[n_hg:n_rec, :])
    code = _dot_nt(hb, wt_ref[n_rec:n_rec + LANES, :])
    logits = _dot(code.astype(BF16), wup_ref[...].astype(BF16)) + bup_ref[...]
    gd = (jnp.minimum(logits, 0.0) - jnp.log(1.0 + jnp.exp(-jnp.abs(logits)))) * (LOG2E / GLA_NORMALIZER)
    gq = pg[:, :gla_k] * (GLA_DK ** -0.5)
    gk = pg[:, gla_k:2 * gla_k]
    gq_b = put("gq", gq)
    gk_b = put("gk", gk)
    put("gd", gd)
    gv_b = put("gv", pg[:, 2 * gla_k:2 * gla_k + gla_v])
    put("gsg", _silu(pg[:, 2 * gla_k + gla_v:]))

    r_ref[...] = _dot_nt(hb, wt_ref[n_rec + GLA_RANK:n_rec + GLA_RANK + r_ref.shape[1], :]).astype(r_ref.dtype)
    stats_ref[0, 0] = jnp.minimum(chunk_min(g), chunk_min(gd))
    stats_ref[0, 1] = functools.reduce(jnp.maximum, [mag_max(a) for a in (q_b, v_b, gq_b, gk_b, gv_b)])


def _inproj(x2d, mod3, gain, wt_b, n_channels, lb_logits, wup_pad, bup, *, layer, sub, tokens_per_batch,
            tm=TOKEN_TILE):
    n, d = x2d.shape
    n_rec = 2 * HG_HEADS * HG_DK + 2 * HG_HEADS * HG_DV + 2 * GLA_HEADS * GLA_DK + 2 * GLA_HEADS * GLA_DV
    n_r = n_channels - n_rec - GLA_RANK
    assert n_rec + LANES <= wt_b.shape[0] and tm // REC_CHUNK == SUBLANES
    tiles_per_batch = tokens_per_batch // tm

    return pl.pallas_call(
        functools.partial(_inproj_kernel, sub=sub, layer=layer),
        grid=(n // tm,),
        in_specs=[
            pl.BlockSpec((tm, d), lambda i: (i, 0)),
            pl.BlockSpec((1, 3 * N_SUB, d), lambda i: (i // tiles_per_batch, 0, 0)),
            pl.BlockSpec((1, d), lambda i: (0, 0)),
            pl.BlockSpec(wt_b.shape, lambda i: (0, 0), pipeline_mode=pl.Buffered(1)),
            pl.BlockSpec(lb_logits.shape, lambda i: (0, 0)),
            pl.BlockSpec(wup_pad.shape, lambda i: (0, 0)),
            pl.BlockSpec((1, bup.shape[0]), lambda i: (0, 0)),
        ],
        out_specs=[
            pl.BlockSpec((tm, DEC_TILES * LANES), lambda i: (i, 0)),
            pl.BlockSpec((tm, ACT_TILES * LANES), lambda i: (i, 0)),
            pl.BlockSpec((tm, n_r), lambda i: (i, 0)),
            pl.BlockSpec((1, 2, SUBLANES, LANES), lambda i: (i, 0, 0, 0)),
        ],
        out_shape=[
            jax.ShapeDtypeStruct((n, DEC_TILES * LANES), F32),
            jax.ShapeDtypeStruct((n, ACT_TILES * LANES), BF16),
            jax.ShapeDtypeStruct((n, n_r), BF16),
            jax.ShapeDtypeStruct((n // tm, 2, SUBLANES, LANES), F32),
        ],
        compiler_params=pltpu.CompilerParams(dimension_semantics=("arbitrary",), vmem_limit_bytes=VMEM_LIMIT["inproj"]),
        name="inproj",
    )(x2d, mod3, gain.reshape(1, d), wt_b, lb_logits, wup_pad, bup.reshape(1, -1))


def _load_perm(ref, base):
    return [ref[pl.ds(base + PERM_GROUP * c + a, SUBLANES, stride=PERM_STRIDE), :]
            for c in range(N_GROUPS) for a in range(PERM_STRIDE)]


def _store_perm(ref, base, val):
    for c in range(N_GROUPS):
        for a in range(PERM_STRIDE):
            u = PERM_STRIDE * c + a
            ref[pl.ds(base + PERM_GROUP * c + a, SUBLANES, stride=PERM_STRIDE), :] = (
                val[SUBLANES * u:SUBLANES * (u + 1), :])


def _shift_down(v, d, sub_iota):
    return jnp.where(sub_iota >= d, pltpu.roll(v, d, 0), 0.0)


def _cumsum_perm(g, sub_iota):
    out = []
    carry = None
    for c in range(N_GROUPS):
        pre = [g[PERM_STRIDE * c]]
        for a in range(1, PERM_STRIDE):
            pre.append(pre[-1] + g[PERM_STRIDE * c + a])
        inc = pre[-1]
        inc = inc + _shift_down(inc, 1, sub_iota)
        inc = inc + _shift_down(inc, 2, sub_iota)
        inc = inc + _shift_down(inc, 4, sub_iota)
        exc = _shift_down(inc, 1, sub_iota)
        if carry is not None:
            exc = exc + carry
        out.extend(p + exc for p in pre)
        tot = inc[SUBLANES - 1:SUBLANES, :]
        carry = tot if carry is None else carry + tot
    return jnp.concatenate(out, axis=0), carry


def _finish(o, gain, act_gate):
    return o * lax.rsqrt(jnp.mean(o * o, axis=-1, keepdims=True) + EPS) * gain * act_gate


def _rec_kernel(safe_ref, q0, q1, k0, k1, g0, g1, v0, v1, sg0, sg1, gq, gk, gd, gv0, gv1, gsg0, gsg1,
                hgn_ref, glan_ref, yh_ref, yg_ref, st_s, e_s, fb_s, oi_s, *, tt, unroll):
    nc = tt // REC_CHUNK
    nb = q0.shape[0]

    @pl.when(pl.program_id(1) == 0)
    def _():
        st_s[...] = jnp.zeros_like(st_s)

    safe = safe_ref[pl.program_id(1)] != 0

    sub_iota = lax.broadcasted_iota(jnp.int32, (SUBLANES, LANES), 0)
    lane = lax.broadcasted_iota(jnp.int32, (REC_CHUNK, LANES), 1)
    causal = (lax.broadcasted_iota(jnp.int32, (REC_CHUNK, REC_CHUNK), 1)
              <= lax.broadcasted_iota(jnp.int32, (REC_CHUNK, REC_CHUNK), 0))
    tok_of_row = lax.broadcasted_iota(jnp.int32, (REC_CHUNK, LANES), 0)
    hgn = hgn_ref[...]
    glan = glan_ref[...]

    head_masks = (lane < GLA_DK, lane >= GLA_DK)
    groups = []
    for bi in range(nb):
        groups += [
            ((q0.at[bi], k0.at[bi], g0.at[bi]), st_s.at[bi, 0],
             [(None, v0.at[bi], sg0.at[bi], hgn, yh_ref.at[0, bi])]),
            ((q1.at[bi], k1.at[bi], g1.at[bi]), st_s.at[bi, 1],
             [(None, v1.at[bi], sg1.at[bi], hgn, yh_ref.at[1, bi])]),
            ((gq.at[bi], gk.at[bi], gd.at[bi]), st_s.at[bi, 2],
             [(head_masks[0], gv0.at[bi], gsg0.at[bi], glan, yg_ref.at[0, bi]),
              (head_masks[1], gv1.at[bi], gsg1.at[bi], glan, yg_ref.at[1, bi])]),
        ]

    def natural(ref, base):
        return ref[pl.ds(base, REC_CHUNK), :]

    def decay_logs(g_ref, base):
        return _cumsum_perm(_load_perm(g_ref, base), sub_iota)

    def to_natural(slot, val):
        _store_perm(e_s.at[slot], 0, val)
        return e_s[slot]

    def masked(x, mask):
        return x if mask is None else jnp.where(mask, x, jnp.zeros_like(x))

    def fast_chunk(ci, carry):
        staged = []
        for sub in range(unroll):
            base = pl.multiple_of((ci * unroll + sub) * REC_CHUNK, REC_CHUNK)
            for gi, ((q_ref, k_ref, g_ref), st_view, heads) in enumerate(groups):
                b, b_last = decay_logs(g_ref, base)
                half = 0.5 * b_last
                d = b - half
                em = jnp.exp2(half)
                slot = 2 * (sub * len(groups) + gi)
                qt_b = (natural(q_ref, base).astype(F32) * to_natural(slot, jnp.exp2(d))).astype(BF16)
                kt_b = (natural(k_ref, base).astype(F32) * to_natural(slot + 1, jnp.exp2(-d))).astype(BF16)
                vbs = [natural(v_ref, base) for _, v_ref, _, _, _ in heads]
                q_all = jnp.concatenate([masked(qt_b, m) for m, *_ in heads], axis=0)
                k_all = jnp.concatenate([masked(kt_b, m) for m, *_ in heads], axis=0)
                v_all = jnp.concatenate(vbs, axis=0)
                state_e = st_view[...] * em
                scores = _dot_nt(q_all, kt_b)
                o_inter = _dot_nt(q_all, state_e.astype(BF16))
                st_view[...] = (state_e + _dot_tn(v_all, k_all)) * em
                for h, (_, _, gate_ref, gain, y_view) in enumerate(heads):
                    rows = slice(h * REC_CHUNK, (h + 1) * REC_CHUNK)
                    staged.append((scores[rows], o_inter[rows], vbs[h], base, gate_ref, gain, y_view))
        for scores, o_inter, vb, base, gate_ref, gain, y_view in staged:
            s = jnp.where(causal, scores, 0.0).astype(BF16)
            o = o_inter + _dot(s, vb)
            gate = natural(gate_ref, base).astype(F32)
            y_view[pl.ds(base, REC_CHUNK), :] = _finish(o, gain, gate).astype(y_view.dtype)
        return carry

    def exact_chunk(ci, carry):
        base = pl.multiple_of(ci * REC_CHUNK, REC_CHUNK)
        for (q_ref, k_ref, g_ref), st_view, heads in groups:
            b_perm, b_last = decay_logs(g_ref, base)
            b = to_natural(0, b_perm)
            q_all = natural(q_ref, base).astype(F32)
            k = natural(k_ref, base).astype(F32)
            state = st_view[...]
            new_state = state * jnp.exp2(b_last)
            kd = k * jnp.exp2(b_last - b)
            for qmask, v_ref, gate_ref, gain, y_view in heads:
                q = masked(q_all, qmask)
                vb = natural(v_ref, base)
                gate = natural(gate_ref, base).astype(F32)
                fb_s[0] = b
                fb_s[1] = q
                fb_s[2] = k
                fb_s[3] = vb.astype(F32)

                def row(r, c2):
                    bt = fb_s[0, pl.ds(r, 1), :]
                    qt = fb_s[1, pl.ds(r, 1), :]
                    w = jnp.exp2(jnp.where(tok_of_row <= r, bt - fb_s[0], -jnp.inf))
                    sc = jnp.sum(qt * w * fb_s[2], axis=-1, keepdims=True)
                    oi_s[pl.ds(r, 1), :] = jnp.sum(sc * fb_s[3], axis=0, keepdims=True)
                    return c2

                lax.fori_loop(0, REC_CHUNK, row, 0)
                o = _dot_nt((q * jnp.exp2(b)).astype(BF16), state.astype(BF16)) + oi_s[...]
                new_state = new_state + _dot_tn(vb, masked(kd, qmask).astype(BF16))
                y_view[pl.ds(base, REC_CHUNK), :] = _finish(o, gain, gate).astype(y_view.dtype)
            st_view[...] = new_state
        return carry

    @pl.when(safe)
    def _():
        lax.fori_loop(0, nc // unroll, fast_chunk, 0)

    @pl.when(jnp.logical_not(safe))
    def _():
        lax.fori_loop(0, nc, exact_chunk, 0)


def _safe_flags(stats, *, batch, tokens_per_batch, tt):
    tiles_per_batch = stats.shape[0] // batch
    chunk_tot = jnp.min(stats[:, 0], axis=-1).reshape(batch, tokens_per_batch // tt, -1)
    tile_mag = jnp.max(stats[:, 1], axis=(-2, -1)).reshape(batch, tiles_per_batch)
    step_tot = jnp.min(chunk_tot, axis=(0, 2))
    step_mag = jnp.repeat(jnp.max(tile_mag, axis=0), (tokens_per_batch // tt) // tiles_per_batch)
    return jnp.logical_and(step_tot >= -2.0 * SAFE_EXP2, step_mag <= SAFE_MAG).astype(jnp.int32)


def _recurrence(dec, act, stats, hg_norm, gla_norm, *, batch, tokens_per_batch, tt=REC_TILE):
    n = dec.shape[0]
    nt = tokens_per_batch // tt
    pairs = HG_HEADS // 2
    dec3 = dec.reshape(batch, tokens_per_batch, dec.shape[1])
    act3 = act.reshape(batch, tokens_per_batch, act.shape[1])
    safe = _safe_flags(stats, batch=batch, tokens_per_batch=tokens_per_batch, tt=tt)

    def tile_of(name, p, e):
        off = DEC_OFF[name] if name in DEC_OFF else ACT_OFF[name]
        return off + (p if e is None else 2 * p + e)

    def spec(name, e=None):
        return pl.BlockSpec((batch, tt, LANES), lambda p, i, flags: (0, i, tile_of(name, p, e)))

    names = [("q", 0), ("q", 1), ("k", 0), ("k", 1), ("g", 0), ("g", 1), ("v", 0), ("v", 1), ("sg", 0), ("sg", 1),
             ("gq", None), ("gk", None), ("gd", None), ("gv", 0), ("gv", 1), ("gsg", 0), ("gsg", 1)]
    in_specs = [spec(name, e) for name, e in names] + [
        pl.BlockSpec((1, LANES), lambda p, i, flags: (0, 0)),
        pl.BlockSpec((1, LANES), lambda p, i, flags: (0, 0)),
    ]
    operands = [dec3 if name in DEC_OFF else act3 for name, _ in names]
    n_groups = 3 * batch
    out_spec = pl.BlockSpec((2, batch, tt, LANES), lambda p, i, flags: (p, 0, i, 0))
    y_hg, y_gla = pl.pallas_call(
        functools.partial(_rec_kernel, tt=tt, unroll=REC_UNROLL),
        grid_spec=pltpu.PrefetchScalarGridSpec(
            num_scalar_prefetch=1,
            grid=(pairs, nt),
            in_specs=in_specs,
            out_specs=[out_spec, out_spec],
            scratch_shapes=[
                pltpu.VMEM((batch, 3, LANES, LANES), F32),
                pltpu.VMEM((2 * REC_UNROLL * n_groups, REC_CHUNK, LANES), F32),
                pltpu.VMEM((4, REC_CHUNK, LANES), F32),
                pltpu.VMEM((REC_CHUNK, LANES), F32),
            ],
        ),
        out_shape=[jax.ShapeDtypeStruct((HG_HEADS, batch, tokens_per_batch, HG_DV), BF16),
                   jax.ShapeDtypeStruct((GLA_HEADS, batch, tokens_per_batch, GLA_DV), BF16)],
        compiler_params=pltpu.CompilerParams(
            dimension_semantics=("arbitrary", "arbitrary"), vmem_limit_bytes=VMEM_LIMIT["recurrence"]),
        name="recurrence",
    )(safe, *operands, hg_norm.reshape(1, HG_DV), gla_norm.reshape(1, GLA_DV))
    return y_hg.reshape(HG_HEADS, n, HG_DV), y_gla.reshape(GLA_HEADS, n, GLA_DV)


def _merge_ffn_kernel(x_ref, mod_ref, yh_ref, yg_ref, rh_ref, rg_ref, wuh_ref, wug_ref, wo_ref,
                      gain_ref, w_in_ref, w_down_ref, fgain_ref, o_ref, *, sub_mix, sub_ffn, d_ff, bounds, final):
    gate = mod_ref[0, 3 * sub_mix + 2:3 * sub_mix + 3, :]

    def up(y_ref, w_ref):
        y = jnp.concatenate([y_ref[h].astype(BF16) for h in range(y_ref.shape[0])], axis=-1)
        return _dot(y, w_ref[...])

    merged2 = ((1.0 + jnp.tanh(0.5 * rh_ref[...].astype(F32))) * up(yh_ref, wuh_ref)
               + (1.0 + jnp.tanh(0.5 * rg_ref[...].astype(F32))) * up(yg_ref, wug_ref))
    x = x_ref[...] + (0.5 * gate) * _dot(merged2.astype(BF16), wo_ref[...])
    o_ref[...] = _ffn_body(x, mod_ref, gain_ref, w_in_ref, w_down_ref, fgain_ref,
                           sub=sub_ffn, d_ff=d_ff, bounds=bounds, final=final)


def _merge_ffn(x2d, mod3, y_hg, y_gla, r, w_up_hg, w_up_gla, w_out, gain, w_in_b, w_down_b, d_ff, fgain, *,
               sub_mix, sub_ffn, tokens_per_batch, final, tm=TOKEN_TILE):
    n, d = x2d.shape
    tiles_per_batch = tokens_per_batch // tm
    const = dict(pipeline_mode=pl.Buffered(1))

    return pl.pallas_call(
        functools.partial(_merge_ffn_kernel, sub_mix=sub_mix, sub_ffn=sub_ffn, d_ff=d_ff,
                          bounds=_mxu_aligned_bounds(d_ff, 2), final=final),
        grid=(n // tm,),
        in_specs=[
            pl.BlockSpec((tm, d), lambda i: (i, 0)),
            pl.BlockSpec((1, 3 * N_SUB, d), lambda i: (i // tiles_per_batch, 0, 0)),
            pl.BlockSpec((HG_HEADS, tm, HG_DV), lambda i: (0, i, 0)),
            pl.BlockSpec((GLA_HEADS, tm, GLA_DV), lambda i: (0, i, 0)),
            pl.BlockSpec((tm, d), lambda i: (i, 0)),
            pl.BlockSpec((tm, d), lambda i: (i, 1)),
            pl.BlockSpec(w_up_hg.shape, lambda i: (0, 0), **const),
            pl.BlockSpec(w_up_gla.shape, lambda i: (0, 0), **const),
            pl.BlockSpec(w_out.shape, lambda i: (0, 0), **const),
            pl.BlockSpec((1, d), lambda i: (0, 0)),
            pl.BlockSpec(w_in_b.shape, lambda i: (0, 0), **const),
            pl.BlockSpec(w_down_b.shape, lambda i: (0, 0), **const),
            pl.BlockSpec((1, d), lambda i: (0, 0)),
        ],
        out_specs=pl.BlockSpec((tm, d), lambda i: (i, 0)),
        out_shape=jax.ShapeDtypeStruct((n, d), F32),
        compiler_params=pltpu.CompilerParams(dimension_semantics=("arbitrary",), vmem_limit_bytes=VMEM_LIMIT["merge_ffn"]),
        name="merge_ffn",
    )(x2d, mod3, y_hg, y_gla, r, r, w_up_hg, w_up_gla, w_out, gain.reshape(1, d), w_in_b, w_down_b,
      fgain.reshape(1, d))


def kernel(x, c, w_ada, b_ada, norm_gains, ffn1_w_in, ffn1_w_down, w_in_mix, w_gk_up, b_gk_up, lb_logits,
           hg_norm, gla_norm, w_up_hg, w_up_gla, w_out, ffn2_w_in, ffn2_w_down, final_norm):
    batch, seq, d = x.shape
    depth = w_ada.shape[0]

    x2d = x.reshape(batch * seq, d)
    c_pad = jnp.pad(c, ((0, SUBLANES - batch % SUBLANES if batch % SUBLANES else 0), (0, 0)))
    for l in range(depth):
        mod = _ada(c_pad, w_ada[l], b_ada[l])[:batch].reshape(batch, 3 * N_SUB, d)

        x2d, (w_mix_b, w_in2_b, w_down2_b) = _ffn(
            x2d, mod, norm_gains[l, 0], ffn1_w_in, ffn1_w_down, final_norm,
            ((jnp.swapaxes(w_in_mix, 1, 2), 0), (ffn2_w_in, ffn2_w_down.shape[1]), (ffn2_w_down, 0)),
            layer=l, sub=0, tokens_per_batch=seq, final=False)

        wup_pad = jnp.pad(w_gk_up[l], ((0, LANES - GLA_RANK), (0, 0)))
        dec, act, r, stats = _inproj(x2d, mod, norm_gains[l, 1], w_mix_b, w_in_mix.shape[2], lb_logits, wup_pad,
                                     b_gk_up[l], layer=l, sub=1, tokens_per_batch=seq)

        y_hg, y_gla = _recurrence(dec, act, stats, hg_norm[l], gla_norm[l], batch=batch, tokens_per_batch=seq)

        x2d = _merge_ffn(x2d, mod, y_hg, y_gla, r, w_up_hg[l].astype(BF16), w_up_gla[l].astype(BF16),
                         w_out[l].astype(BF16), norm_gains[l, 2], w_in2_b, w_down2_b, ffn2_w_down.shape[1],
                         final_norm, sub_mix=1, sub_ffn=2, tokens_per_batch=seq, final=l == depth - 1)
    return x2d.reshape(batch, seq, d)
```

```python
import functools

import jax
import jax.numpy as jnp
from jax import lax
from jax.experimental import pallas as pl
from jax.experimental.pallas import tpu as pltpu

F32 = jnp.float32
BF16 = jnp.bfloat16

EPS = 1e-6
N_SUB = 3
HG_HEADS = 4
HG_DK = 128
HG_DV = 128
GLA_HEADS = 4
GLA_DK = 64
GLA_DV = 128
GLA_RANK = 16
GLA_NORMALIZER = 16.0

LANES = 128
SUBLANES = 8
MXU_WIDTH = 256
GATE_PRESCALE = 0.5
MIB = 1024 * 1024

TOKEN_TILE = 512
REC_TILE = 1024
ADA_STEPS = 8
VMEM_LIMIT = {"ffn": 52 * MIB, "inproj": 56 * MIB, "recurrence": 60 * MIB, "merge_ffn": 58 * MIB}

REC_CHUNK = 64
PERM_STRIDE = 4
PERM_GROUP = PERM_STRIDE * SUBLANES
N_GROUPS = REC_CHUNK // PERM_GROUP
REC_UNROLL = 4
LOG2E = 1.4426950408889634
SAFE_EXP2 = 100.0
SAFE_MAG = 500.0


def _dot(a, b):
    return jnp.dot(a, b, preferred_element_type=F32)


def _dot_nt(a, b):
    return lax.dot_general(a, b, (((1,), (1,)), ((), ())), preferred_element_type=F32)


def _dot_tn(a, b):
    return lax.dot_general(a, b, (((0,), (0,)), ((), ())), preferred_element_type=F32)


def _silu(v):
    h = 0.5 * v
    return h * jnp.tanh(h) + h


def _norm_mod(x, gain, shift, scale):
    return x * lax.rsqrt(jnp.mean(x * x, axis=-1, keepdims=True) + EPS) * (gain * (1.0 + scale)) + shift


def _ada_kernel(c_ref, w_ref, b_ref, o_ref):
    @pl.when(pl.program_id(0) == 0)
    def _():
        o_ref[...] = jnp.broadcast_to(b_ref[...], o_ref.shape)

    cond = _silu(c_ref[...]).astype(BF16)
    o_ref[...] += _dot(cond, w_ref[...].astype(BF16))


def _ada(c_pad, w, b):
    rows, d = c_pad.shape
    n = w.shape[1]
    kc = d // ADA_STEPS
    return pl.pallas_call(
        _ada_kernel,
        grid=(ADA_STEPS,),
        in_specs=[
            pl.BlockSpec((rows, kc), lambda k: (0, k)),
            pl.BlockSpec((kc, n), lambda k: (k, 0)),
            pl.BlockSpec((1, n), lambda k: (0, 0)),
        ],
        out_specs=pl.BlockSpec((rows, n), lambda k: (0, 0)),
        out_shape=jax.ShapeDtypeStruct((rows, n), F32),
        compiler_params=pltpu.CompilerParams(dimension_semantics=("arbitrary",)),
        name="adaln",
    )(c_pad, w, b.reshape(1, n))


def _mxu_aligned_bounds(width, parts):
    if width % MXU_WIDTH:
        return (0, width)
    tiles = width // MXU_WIDTH
    return tuple(MXU_WIDTH * ((tiles * p + parts - 1) // parts) for p in range(parts)) + (width,)


def _ffn_weight_specs(layer, d, d_ff):
    assert d_ff % MXU_WIDTH == 0
    steps = d_ff // MXU_WIDTH

    def chunk(i):
        return jnp.minimum(i, steps - 1)

    specs = [
        pl.BlockSpec((None, d, MXU_WIDTH), lambda i: (layer, 0, chunk(i))),
        pl.BlockSpec((None, d, MXU_WIDTH), lambda i: (layer, 0, steps + chunk(i))),
        pl.BlockSpec((None, MXU_WIDTH, d), lambda i: (layer, chunk(i), 0)),
    ]
    scratch = [pltpu.VMEM((d, 2 * d_ff), BF16), pltpu.VMEM((d_ff, d), BF16)]
    return steps, specs, scratch


def _stage_ffn_weights(step, wg_ref, wu_ref, wd_ref, w_in_s, w_down_s, d_ff):
    for j in range(d_ff // MXU_WIDTH):
        @pl.when(step == j)
        def _(lo=j * MXU_WIDTH):
            w_in_s[:, lo:lo + MXU_WIDTH] = (GATE_PRESCALE * wg_ref[...]).astype(BF16)
            w_in_s[:, d_ff + lo:d_ff + lo + MXU_WIDTH] = wu_ref[...].astype(BF16)
            w_down_s[lo:lo + MXU_WIDTH, :] = wd_ref[...].astype(BF16)


class _SideCast:
    def __init__(self, weight_all, layer, n_tiles, prescaled_cols=0):
        self.weight_all, self.layer, self.prescaled_cols = weight_all, layer, prescaled_cols
        self.valid_rows, self.cols = weight_all.shape[1:]
        least = -(-(-(-self.valid_rows // n_tiles)) // 16) * 16
        self.rows = next((r for r in range(least, 2 * least + 1, 16) if self.valid_rows % r == 0), least)
        self.n_chunks = -(-self.valid_rows // self.rows)

    def chunk(self, tile):
        return jnp.minimum(tile, self.n_chunks - 1)

    def in_spec(self, tile_of_step):
        return pl.BlockSpec((None, self.rows, self.cols), lambda i: (self.layer, self.chunk(tile_of_step(i)), 0))

    def out_spec(self, tile_of_step):
        return pl.BlockSpec((self.rows, self.cols), lambda i: (self.chunk(tile_of_step(i)), 0))

    def out_shape(self):
        return jax.ShapeDtypeStruct((self.n_chunks * self.rows, self.cols), BF16)

    def cast(self, tile, in_ref, out_ref):
        w = in_ref[...]
        if self.n_chunks * self.rows > self.valid_rows:
            row = self.chunk(tile) * self.rows + lax.broadcasted_iota(jnp.int32, w.shape, 0)
            w = jnp.where(row < self.valid_rows, w, 0.0)
        if self.prescaled_cols:
            col = lax.broadcasted_iota(jnp.int32, (1, self.cols), 1)
            w = w * jnp.where(col < self.prescaled_cols, GATE_PRESCALE, 1.0)
        out_ref[...] = w.astype(BF16)


def _ffn_kernel(*refs, sub, d_ff, bounds, final, w_steps, sides):
    x_ref, mod_ref, gain_ref, wg_ref, wu_ref, wd_ref, fgain_ref = refs[:7]
    side_in = refs[7:7 + len(sides)]
    o_ref = refs[7 + len(sides)]
    side_out = refs[8 + len(sides):8 + 2 * len(sides)]
    w_in_s, w_down_s = refs[8 + 2 * len(sides):]
    step = pl.program_id(0)
    _stage_ffn_weights(step, wg_ref, wu_ref, wd_ref, w_in_s, w_down_s, d_ff)

    @pl.when(step >= w_steps)
    def _():
        o_ref[...] = _ffn_body(x_ref[...], mod_ref, gain_ref, w_in_s, w_down_s, fgain_ref,
                               sub=sub, d_ff=d_ff, bounds=bounds, final=final)
        for side, in_ref, out_ref in zip(sides, side_in, side_out):
            side.cast(step - w_steps, in_ref, out_ref)


def _ffn_body(x, mod_ref, gain_ref, w_in_ref, w_down_ref, fgain_ref, *, sub, d_ff, bounds, final):
    shift = mod_ref[0, 3 * sub + 0:3 * sub + 1, :]
    scale = mod_ref[0, 3 * sub + 1:3 * sub + 2, :]
    gate = mod_ref[0, 3 * sub + 2:3 * sub + 3, :]
    hb = _norm_mod(x, gain_ref[...], shift, scale).astype(BF16)
    acc = None
    for lo, hi in zip(bounds[:-1], bounds[1:]):
        h = _dot(hb, w_in_ref[:, lo:hi])
        u = _dot(hb, w_in_ref[:, d_ff + lo:d_ff + hi])
        act = ((h * jnp.tanh(h) + h) * u).astype(BF16)
        part = _dot(act, w_down_ref[lo:hi, :])
        acc = part if acc is None else acc + part
    xn = x + (0.5 * gate) * acc
    if final:
        xn = xn * lax.rsqrt(jnp.mean(xn * xn, axis=-1, keepdims=True) + EPS) * fgain_ref[...]
    return xn


def _ffn(x2d, mod3, gain, w_in_all, w_down_all, fgain, side_weights, *, layer, sub, tokens_per_batch, final,
         tm=TOKEN_TILE):
    n, d = x2d.shape
    d_ff = w_down_all.shape[1]
    tiles_per_batch = tokens_per_batch // tm
    w_steps, w_specs, w_scratch = _ffn_weight_specs(layer, d, d_ff)
    sides = [_SideCast(w, layer, n // tm, prescaled) for w, prescaled in side_weights]

    def tile(i):
        return jnp.maximum(i - w_steps, 0)

    outs = pl.pallas_call(
        functools.partial(_ffn_kernel, sub=sub, d_ff=d_ff, bounds=_mxu_aligned_bounds(d_ff, 2), final=final,
                          w_steps=w_steps, sides=sides),
        grid=(w_steps + n // tm,),
        in_specs=[
            pl.BlockSpec((tm, d), lambda i: (tile(i), 0)),
            pl.BlockSpec((1, 3 * N_SUB, d), lambda i: (tile(i) // tiles_per_batch, 0, 0)),
            pl.BlockSpec((1, d), lambda i: (0, 0)),
            *w_specs,
            pl.BlockSpec((1, d), lambda i: (0, 0)),
            *[s.in_spec(tile) for s in sides],
        ],
        out_specs=[pl.BlockSpec((tm, d), lambda i: (tile(i), 0)), *[s.out_spec(tile) for s in sides]],
        out_shape=[jax.ShapeDtypeStruct((n, d), F32), *[s.out_shape() for s in sides]],
        scratch_shapes=w_scratch,
        compiler_params=pltpu.CompilerParams(dimension_semantics=("arbitrary",), vmem_limit_bytes=VMEM_LIMIT["ffn"]),
        name="ffn_final" if final else "ffn",
    )(x2d, mod3, gain.reshape(1, d), w_in_all, w_in_all, w_down_all, fgain.reshape(1, d),
      *[w for w, _ in side_weights])
    return outs[0], outs[1:]


DEC_OFF = {"g": 0, "gd": 4}
DEC_TILES = 6
ACT_OFF = {"q": 0, "k": 4, "v": 8, "sg": 12, "gq": 16, "gk": 18, "gv": 20, "gsg": 24}
ACT_TILES = 28


def _inproj_kernel(x_ref, mod_ref, gain_ref, wt_ref, lbl_ref, wup_ref, bup_ref, dec_ref, act_ref, r_ref, stats_ref, *,
                   sub, layer):
    shift = mod_ref[0, 3 * sub + 0:3 * sub + 1, :]
    scale = mod_ref[0, 3 * sub + 1:3 * sub + 2, :]
    hb = _norm_mod(x_ref[...], gain_ref[...], shift, scale).astype(BF16)
    tm = hb.shape[0]
    hg_qk, hg_w = HG_HEADS * HG_DK, HG_HEADS * HG_DV
    gla_k, gla_v = GLA_HEADS * GLA_DK, GLA_HEADS * GLA_DV
    n_hg = 2 * hg_qk + 2 * hg_w
    n_rec = n_hg + 2 * gla_k + 2 * gla_v

    def put(name, val):
        if name in DEC_OFF:
            lo = DEC_OFF[name] * LANES
            dec_ref[:, lo:lo + val.shape[1]] = val
        else:
            lo = ACT_OFF[name] * LANES
            val = val.astype(act_ref.dtype)
            act_ref[:, lo:lo + val.shape[1]] = val
        return val

    def chunk_min(g):
        tot = jnp.sum(g.reshape(tm // REC_CHUNK, REC_CHUNK, g.shape[1]), axis=1)
        return functools.reduce(jnp.minimum, [tot[:, j:j + LANES] for j in range(0, g.shape[1], LANES)])

    def mag_max(v):
        rows = 2 * SUBLANES
        m = jnp.max(jnp.abs(v).reshape(tm // rows, rows, v.shape[1]), axis=0)
        m = functools.reduce(jnp.maximum, [m[:, j:j + LANES] for j in range(0, v.shape[1], LANES)]).astype(F32)
        return jnp.maximum(m[:SUBLANES], m[SUBLANES:])

    ph = _dot_nt(hb, wt_ref[:n_hg, :])
    lbl = lbl_ref[...]
    ex = jnp.exp(lbl - jnp.max(lbl, axis=0, keepdims=True))
    lb = jnp.sum(ex[:layer + 1], axis=0, keepdims=True) / jnp.sum(ex, axis=0, keepdims=True)
    c1 = 0.5 * (1.0 - lb)
    q = _silu(ph[:, :hg_qk])
    p = c1 * jnp.tanh(0.5 * ph[:, hg_qk:2 * hg_qk])
    g = jnp.log2((1.0 - c1) + p)
    q_b = put("q", q)
    put("k", c1 - p)
    put("g", g)
    v_b = put("v", ph[:, 2 * hg_qk:2 * hg_qk + hg_w])
    put("sg", _silu(ph[:, 2 * hg_qk + hg_w:]))

    pg = _dot_nt(hb, wt_ref[n_hg:n_rec, :])
    code = _dot_nt(hb, wt_ref[n_rec:n_rec + LANES, :])
    logits = _dot(code.astype(BF16), wup_ref[...].astype(BF16)) + bup_ref[...]
    gd = (jnp.minimum(logits, 0.0) - jnp.log(1.0 + jnp.exp(-jnp.abs(logits)))) * (LOG2E / GLA_NORMALIZER)
    gq = pg[:, :gla_k] * (GLA_DK ** -0.5)
    gk = pg[:, gla_k:2 * gla_k]
    gq_b = put("gq", gq)
    gk_b = put("gk", gk)
    put("gd", gd)
    gv_b = put("gv", pg[:, 2 * gla_k:2 * gla_k + gla_v])
    put("gsg", _silu(pg[:, 2 * gla_k + gla_v:]))

    r_ref[...] = _dot_nt(hb, wt_ref[n_rec + GLA_RANK:n_rec + GLA_RANK + r_ref.shape[1], :]).astype(r_ref.dtype)
    stats_ref[0, 0] = jnp.minimum(chunk_min(g), chunk_min(gd))
    stats_ref[0, 1] = functools.reduce(jnp.maximum, [mag_max(a) for a in (q_b, v_b, gq_b, gk_b, gv_b)])


def _inproj(x2d, mod3, gain, wt_b, n_channels, lb_logits, wup_pad, bup, *, layer, sub, tokens_per_batch,
            tm=TOKEN_TILE):
    n, d = x2d.shape
    n_rec = 2 * HG_HEADS * HG_DK + 2 * HG_HEADS * HG_DV + 2 * GLA_HEADS * GLA_DK + 2 * GLA_HEADS * GLA_DV
    n_r = n_channels - n_rec - GLA_RANK
    assert n_rec + LANES <= wt_b.shape[0] and tm // REC_CHUNK == SUBLANES
    tiles_per_batch = tokens_per_batch // tm

    return pl.pallas_call(
        functools.partial(_inproj_kernel, sub=sub, layer=layer),
        grid=(n // tm,),
        in_specs=[
            pl.BlockSpec((tm, d), lambda i: (i, 0)),
            pl.BlockSpec((1, 3 * N_SUB, d), lambda i: (i // tiles_per_batch, 0, 0)),
            pl.BlockSpec((1, d), lambda i: (0, 0)),
            pl.BlockSpec(wt_b.shape, lambda i: (0, 0), pipeline_mode=pl.Buffered(1)),
            pl.BlockSpec(lb_logits.shape, lambda i: (0, 0)),
            pl.BlockSpec(wup_pad.shape, lambda i: (0, 0)),
            pl.BlockSpec((1, bup.shape[0]), lambda i: (0, 0)),
        ],
        out_specs=[
            pl.BlockSpec((tm, DEC_TILES * LANES), lambda i: (i, 0)),
            pl.BlockSpec((tm, ACT_TILES * LANES), lambda i: (i, 0)),
            pl.BlockSpec((tm, n_r), lambda i: (i, 0)),
            pl.BlockSpec((1, 2, SUBLANES, LANES), lambda i: (i, 0, 0, 0)),
        ],
        out_shape=[
            jax.ShapeDtypeStruct((n, DEC_TILES * LANES), F32),
            jax.ShapeDtypeStruct((n, ACT_TILES * LANES), BF16),
            jax.ShapeDtypeStruct((n, n_r), BF16),
            jax.ShapeDtypeStruct((n // tm, 2, SUBLANES, LANES), F32),
        ],
        compiler_params=pltpu.CompilerParams(dimension_semantics=("arbitrary",), vmem_limit_bytes=VMEM_LIMIT["inproj"]),
        name="inproj",
    )(x2d, mod3, gain.reshape(1, d), wt_b, lb_logits, wup_pad, bup.reshape(1, -1))


def _load_perm(ref, base):
    return [ref[pl.ds(base + PERM_GROUP * c + a, SUBLANES, stride=PERM_STRIDE), :]
            for c in range(N_GROUPS) for a in range(PERM_STRIDE)]


def _store_perm(ref, base, val):
    for c in range(N_GROUPS):
        for a in range(PERM_STRIDE):
            u = PERM_STRIDE * c + a
            ref[pl.ds(base + PERM_GROUP * c + a, SUBLANES, stride=PERM_STRIDE), :] = (
                val[SUBLANES * u:SUBLANES * (u + 1), :])


def _shift_down(v, d, sub_iota):
    return jnp.where(sub_iota >= d, pltpu.roll(v, d, 0), 0.0)


def _cumsum_perm(g, sub_iota):
    out = []
    carry = None
    for c in range(N_GROUPS):
        pre = [g[PERM_STRIDE * c]]
        for a in range(1, PERM_STRIDE):
            pre.append(pre[-1] + g[PERM_STRIDE * c + a])
        inc = pre[-1]
        inc = inc + _shift_down(inc, 1, sub_iota)
        inc = inc + _shift_down(inc, 2, sub_iota)
        inc = inc + _shift_down(inc, 4, sub_iota)
        exc = _shift_down(inc, 1, sub_iota)
        if carry is not None:
            exc = exc + carry
        out.extend(p + exc for p in pre)
        tot = inc[SUBLANES - 1:SUBLANES, :]
        carry = tot if carry is None else carry + tot
    return jnp.concatenate(out, axis=0), carry


def _finish(o, gain, act_gate):
    return o * lax.rsqrt(jnp.mean(o * o, axis=-1, keepdims=True) + EPS) * gain * act_gate


def _rec_kernel(safe_ref, q0, q1, k0, k1, g0, g1, v0, v1, sg0, sg1, gq, gk, gd, gv0, gv1, gsg0, gsg1,
                hgn_ref, glan_ref, yh_ref, yg_ref, st_s, e_s, fb_s, oi_s, *, tt, unroll):
    nc = tt // REC_CHUNK
    nb = q0.shape[0]

    @pl.when(pl.program_id(1) == 0)
    def _():
        st_s[...] = jnp.zeros_like(st_s)

    safe = safe_ref[pl.program_id(1)] != 0

    sub_iota = lax.broadcasted_iota(jnp.int32, (SUBLANES, LANES), 0)
    lane = lax.broadcasted_iota(jnp.int32, (REC_CHUNK, LANES), 1)
    causal = (lax.broadcasted_iota(jnp.int32, (REC_CHUNK, REC_CHUNK), 1)
              <= lax.broadcasted_iota(jnp.int32, (REC_CHUNK, REC_CHUNK), 0))
    tok_of_row = lax.broadcasted_iota(jnp.int32, (REC_CHUNK, LANES), 0)
    hgn = hgn_ref[...]
    glan = glan_ref[...]

    head_masks = (lane < GLA_DK, lane >= GLA_DK)
    groups = []
    for bi in range(nb):
        groups += [
            ((q0.at[bi], k0.at[bi], g0.at[bi]), st_s.at[bi, 0],
             [(None, v0.at[bi], sg0.at[bi], hgn, yh_ref.at[0, bi])]),
            ((q1.at[bi], k1.at[bi], g1.at[bi]), st_s.at[bi, 1],
             [(None, v1.at[bi], sg1.at[bi], hgn, yh_ref.at[1, bi])]),
            ((gq.at[bi], gk.at[bi], gd.at[bi]), st_s.at[bi, 2],
             [(head_masks[0], gv0.at[bi], gsg0.at[bi], glan, yg_ref.at[0, bi]),
              (head_masks[1], gv1.at[bi], gsg1.at[bi], glan, yg_ref.at[1, bi])]),
        ]

    def natural(ref, base):
        return ref[pl.ds(base, REC_CHUNK), :]

    def decay_logs(g_ref, base):
        return _cumsum_perm(_load_perm(g_ref, base), sub_iota)

    def to_natural(slot, val):
        _store_perm(e_s.at[slot], 0, val)
        return e_s[slot]

    def masked(x, mask):
        return x if mask is None else jnp.where(mask, x, jnp.zeros_like(x))

    def fast_chunk(ci, carry):
        staged = []
        for sub in range(unroll):
            base = pl.multiple_of((ci * unroll + sub) * REC_CHUNK, REC_CHUNK)
            for gi, ((q_ref, k_ref, g_ref), st_view, heads) in enumerate(groups):
                b, b_last = decay_logs(g_ref, base)
                half = 0.5 * b_last
                d = b - half
                em = jnp.exp2(half)
                slot = 2 * (sub * len(groups) + gi)
                qt_b = (natural(q_ref, base).astype(F32) * to_natural(slot, jnp.exp2(d))).astype(BF16)
                kt_b = (natural(k_ref, base).astype(F32) * to_natural(slot + 1, jnp.exp2(-d))).astype(BF16)
                vbs = [natural(v_ref, base) for _, v_ref, _, _, _ in heads]
                q_all = jnp.concatenate([masked(qt_b, m) for m, *_ in heads], axis=0)
                k_all = jnp.concatenate([masked(kt_b, m) for m, *_ in heads], axis=0)
                v_all = jnp.concatenate(vbs, axis=0)
                state_e = st_view[...] * em
                scores = _dot_nt(q_all, kt_b)
                o_inter = _dot_nt(q_all, state_e.astype(BF16))
                st_view[...] = (state_e + _dot_tn(v_all, k_all)) * em
                for h, (_, _, gate_ref, gain, y_view) in enumerate(heads):
                    rows = slice(h * REC_CHUNK, (h + 1) * REC_CHUNK)
                    staged.append((scores[rows], o_inter[rows], vbs[h], base, gate_ref, gain, y_view))
        for scores, o_inter, vb, base, gate_ref, gain, y_view in staged:
            s = jnp.where(causal, scores, 0.0).astype(BF16)
            o = o_inter + _dot(s, vb)
            gate = natural(gate_ref, base).astype(F32)
            y_view[pl.ds(base, REC_CHUNK), :] = _finish(o, gain, gate).astype(y_view.dtype)
        return carry

    def exact_chunk(ci, carry):
        base = pl.multiple_of(ci * REC_CHUNK, REC_CHUNK)
        for (q_ref, k_ref, g_ref), st_view, heads in groups:
            b_perm, b_last = decay_logs(g_ref, base)
            b = to_natural(0, b_perm)
            q_all = natural(q_ref, base).astype(F32)
            k = natural(k_ref, base).astype(F32)
            state = st_view[...]
            new_state = state * jnp.exp2(b_last)
            kd = k * jnp.exp2(b_last - b)
            for qmask, v_ref, gate_ref, gain, y_view in heads:
                q = masked(q_all, qmask)
                vb = natural(v_ref, base)
                gate = natural(gate_ref, base).astype(F32)
                fb_s[0] = b
                fb_s[1] = q
                fb_s[2] = k
                fb_s[3] = vb.astype(F32)

                def row(r, c2):
                    bt = fb_s[0, pl.ds(r, 1), :]
                    qt = fb_s[1, pl.ds(r, 1), :]
                    w = jnp.exp2(jnp.where(tok_of_row <= r, bt - fb_s[0], -jnp.inf))
                    sc = jnp.sum(qt * w * fb_s[2], axis=-1, keepdims=True)
                    oi_s[pl.ds(r, 1), :] = jnp.sum(sc * fb_s[3], axis=0, keepdims=True)
                    return c2

                lax.fori_loop(0, REC_CHUNK, row, 0)
                o = _dot_nt((q * jnp.exp2(b)).astype(BF16), state.astype(BF16)) + oi_s[...]
                new_state = new_state + _dot_tn(vb, masked(kd, qmask).astype(BF16))
                y_view[pl.ds(base, REC_CHUNK), :] = _finish(o, gain, gate).astype(y_view.dtype)
            st_view[...] = new_state
        return carry

    @pl.when(safe)
    def _():
        lax.fori_loop(0, nc // unroll, fast_chunk, 0)

    @pl.when(jnp.logical_not(safe))
    def _():
        lax.fori_loop(0, nc, exact_chunk, 0)


def _safe_flags(stats, *, batch, tokens_per_batch, tt):
    tiles_per_batch = stats.shape[0] // batch
    chunk_tot = jnp.min(stats[:, 0], axis=-1).reshape(batch, tokens_per_batch // tt, -1)
    tile_mag = jnp.max(stats[:, 1], axis=(-2, -1)).reshape(batch, tiles_per_batch)
    step_tot = jnp.min(chunk_tot, axis=(0, 2))
    n_steps = tokens_per_batch // tt
    tile_mag = jnp.max(tile_mag, axis=0)
    if n_steps >= tiles_per_batch:
        step_mag = jnp.repeat(tile_mag, n_steps // tiles_per_batch)
    else:
        step_mag = jnp.max(tile_mag.reshape(n_steps, -1), axis=1)
    return jnp.logical_and(step_tot >= -2.0 * SAFE_EXP2, step_mag <= SAFE_MAG).astype(jnp.int32)


def _recurrence(dec, act, stats, hg_norm, gla_norm, *, batch, tokens_per_batch, tt=REC_TILE):
    n = dec.shape[0]
    nt = tokens_per_batch // tt
    pairs = HG_HEADS // 2
    dec3 = dec.reshape(batch, tokens_per_batch, dec.shape[1])
    act3 = act.reshape(batch, tokens_per_batch, act.shape[1])
    safe = _safe_flags(stats, batch=batch, tokens_per_batch=tokens_per_batch, tt=tt)

    def tile_of(name, p, e):
        off = DEC_OFF[name] if name in DEC_OFF else ACT_OFF[name]
        return off + (p if e is None else 2 * p + e)

    def spec(name, e=None):
        return pl.BlockSpec((batch, tt, LANES), lambda p, i, flags: (0, i, tile_of(name, p, e)))

    names = [("q", 0), ("q", 1), ("k", 0), ("k", 1), ("g", 0), ("g", 1), ("v", 0), ("v", 1), ("sg", 0), ("sg", 1),
             ("gq", None), ("gk", None), ("gd", None), ("gv", 0), ("gv", 1), ("gsg", 0), ("gsg", 1)]
    in_specs = [spec(name, e) for name, e in names] + [
        pl.BlockSpec((1, LANES), lambda p, i, flags: (0, 0)),
        pl.BlockSpec((1, LANES), lambda p, i, flags: (0, 0)),
    ]
    operands = [dec3 if name in DEC_OFF else act3 for name, _ in names]
    n_groups = 3 * batch
    out_spec = pl.BlockSpec((2, batch, tt, LANES), lambda p, i, flags: (p, 0, i, 0))
    y_hg, y_gla = pl.pallas_call(
        functools.partial(_rec_kernel, tt=tt, unroll=REC_UNROLL),
        grid_spec=pltpu.PrefetchScalarGridSpec(
            num_scalar_prefetch=1,
            grid=(pairs, nt),
            in_specs=in_specs,
            out_specs=[out_spec, out_spec],
            scratch_shapes=[
                pltpu.VMEM((batch, 3, LANES, LANES), F32),
                pltpu.VMEM((2 * REC_UNROLL * n_groups, REC_CHUNK, LANES), F32),
                pltpu.VMEM((4, REC_CHUNK, LANES), F32),
                pltpu.VMEM((REC_CHUNK, LANES), F32),
            ],
        ),
        out_shape=[jax.ShapeDtypeStruct((HG_HEADS, batch, tokens_per_batch, HG_DV), BF16),
                   jax.ShapeDtypeStruct((GLA_HEADS, batch, tokens_per_batch, GLA_DV), BF16)],
        compiler_params=pltpu.CompilerParams(
            dimension_semantics=("arbitrary", "arbitrary"), vmem_limit_bytes=VMEM_LIMIT["recurrence"]),
        name="recurrence",
    )(safe, *operands, hg_norm.reshape(1, HG_DV), gla_norm.reshape(1, GLA_DV))
    return y_hg.reshape(HG_HEADS, n, HG_DV), y_gla.reshape(GLA_HEADS, n, GLA_DV)


def _merge_ffn_kernel(x_ref, mod_ref, yh_ref, yg_ref, rh_ref, rg_ref, wuh_ref, wug_ref, wo_ref,
                      gain_ref, w_in_ref, w_down_ref, fgain_ref, o_ref, *, sub_mix, sub_ffn, d_ff, bounds, final):
    gate = mod_ref[0, 3 * sub_mix + 2:3 * sub_mix + 3, :]

    def up(y_ref, w_ref):
        y = jnp.concatenate([y_ref[h].astype(BF16) for h in range(y_ref.shape[0])], axis=-1)
        return _dot(y, w_ref[...])

    merged2 = ((1.0 + jnp.tanh(0.5 * rh_ref[...].astype(F32))) * up(yh_ref, wuh_ref)
               + (1.0 + jnp.tanh(0.5 * rg_ref[...].astype(F32))) * up(yg_ref, wug_ref))
    x = x_ref[...] + (0.5 * gate) * _dot(merged2.astype(BF16), wo_ref[...])
    o_ref[...] = _ffn_body(x, mod_ref, gain_ref, w_in_ref, w_down_ref, fgain_ref,
                           sub=sub_ffn, d_ff=d_ff, bounds=bounds, final=final)


def _merge_ffn(x2d, mod3, y_hg, y_gla, r, w_up_hg, w_up_gla, w_out, gain, w_in_b, w_down_b, d_ff, fgain, *,
               sub_mix, sub_ffn, tokens_per_batch, final, tm=TOKEN_TILE):
    n, d = x2d.shape
    tiles_per_batch = tokens_per_batch // tm
    const = dict(pipeline_mode=pl.Buffered(1))

    return pl.pallas_call(
        functools.partial(_merge_ffn_kernel, sub_mix=sub_mix, sub_ffn=sub_ffn, d_ff=d_ff,
                          bounds=_mxu_aligned_bounds(d_ff, 2), final=final),
        grid=(n // tm,),
        in_specs=[
            pl.BlockSpec((tm, d), lambda i: (i, 0)),
            pl.BlockSpec((1, 3 * N_SUB, d), lambda i: (i // tiles_per_batch, 0, 0)),
            pl.BlockSpec((HG_HEADS, tm, HG_DV), lambda i: (0, i, 0)),
            pl.BlockSpec((GLA_HEADS, tm, GLA_DV), lambda i: (0, i, 0)),
            pl.BlockSpec((tm, d), lambda i: (i, 0)),
            pl.BlockSpec((tm, d), lambda i: (i, 1)),
            pl.BlockSpec(w_up_hg.shape, lambda i: (0, 0), **const),
            pl.BlockSpec(w_up_gla.shape, lambda i: (0, 0), **const),
            pl.BlockSpec(w_out.shape, lambda i: (0, 0), **const),
            pl.BlockSpec((1, d), lambda i: (0, 0)),
            pl.BlockSpec(w_in_b.shape, lambda i: (0, 0), **const),
            pl.BlockSpec(w_down_b.shape, lambda i: (0, 0), **const),
            pl.BlockSpec((1, d), lambda i: (0, 0)),
        ],
        out_specs=pl.BlockSpec((tm, d), lambda i: (i, 0)),
        out_shape=jax.ShapeDtypeStruct((n, d), F32),
        compiler_params=pltpu.CompilerParams(dimension_semantics=("arbitrary",),
                                             vmem_limit_bytes=VMEM_LIMIT["merge_ffn"]),
        name="merge_ffn",
    )(x2d, mod3, y_hg, y_gla, r, r, w_up_hg, w_up_gla, w_out, gain.reshape(1, d), w_in_b, w_down_b,
      fgain.reshape(1, d))


def kernel(x, c, w_ada, b_ada, norm_gains, ffn1_w_in, ffn1_w_down, w_in_mix, w_gk_up, b_gk_up, lb_logits,
           hg_norm, gla_norm, w_up_hg, w_up_gla, w_out, ffn2_w_in, ffn2_w_down, final_norm):
    batch, seq, d = x.shape
    depth = w_ada.shape[0]

    x2d = x.reshape(batch * seq, d)
    c_pad = jnp.pad(c, ((0, SUBLANES - batch % SUBLANES if batch % SUBLANES else 0), (0, 0)))
    for l in range(depth):
        mod = _ada(c_pad, w_ada[l], b_ada[l])[:batch].reshape(batch, 3 * N_SUB, d)

        x2d, (w_mix_b, w_in2_b, w_down2_b) = _ffn(
            x2d, mod, norm_gains[l, 0], ffn1_w_in, ffn1_w_down, final_norm,
            ((jnp.swapaxes(w_in_mix, 1, 2), 0), (ffn2_w_in, ffn2_w_down.shape[1]), (ffn2_w_down, 0)),
            layer=l, sub=0, tokens_per_batch=seq, final=False)

        wup_pad = jnp.pad(w_gk_up[l], ((0, LANES - GLA_RANK), (0, 0)))
        dec, act, r, stats = _inproj(x2d, mod, norm_gains[l, 1], w_mix_b, w_in_mix.shape[2], lb_logits, wup_pad,
                                     b_gk_up[l], layer=l, sub=1, tokens_per_batch=seq)

        y_hg, y_gla = _recurrence(dec, act, stats, hg_norm[l], gla_norm[l], batch=batch, tokens_per_batch=seq)

        x2d = _merge_ffn(x2d, mod, y_hg, y_gla, r, w_up_hg[l].astype(BF16), w_up_gla[l].astype(BF16),
                         w_out[l].astype(BF16), norm_gains[l, 2], w_in2_b, w_down2_b, ffn2_w_down.shape[1],
                         final_norm, sub_mix=1, sub_ffn=2, tokens_per_batch=seq, final=l == depth - 1)
    return x2d.reshape(batch, seq, d)
```

```python
import functools

import jax
import jax.numpy as jnp
from jax import lax
from jax.experimental import pallas as pl
from jax.experimental.pallas import tpu as pltpu

F32 = jnp.float32
BF16 = jnp.bfloat16

EPS = 1e-6
N_SUB = 3
HG_HEADS = 4
HG_DK = 128
HG_DV = 128
GLA_HEADS = 4
GLA_DK = 64
GLA_DV = 128
GLA_RANK = 16
GLA_NORMALIZER = 16.0

LANES = 128
SUBLANES = 8
MXU_WIDTH = 256
GATE_PRESCALE = 0.5
MIB = 1024 * 1024

TOKEN_TILE = 512
REC_TILE = 512
ADA_STEPS = 8
VMEM_LIMIT = {"ffn": 52 * MIB, "inproj": 56 * MIB, "recurrence": 48 * MIB, "merge_ffn": 58 * MIB}

REC_CHUNK = 64
PERM_STRIDE = 4
PERM_GROUP = PERM_STRIDE * SUBLANES
N_GROUPS = REC_CHUNK // PERM_GROUP
REC_UNROLL = 8
LOG2E = 1.4426950408889634
SAFE_EXP2 = 100.0
SAFE_MAG = 500.0


def _dot(a, b):
    return jnp.dot(a, b, preferred_element_type=F32)


def _dot_nt(a, b):
    return lax.dot_general(a, b, (((1,), (1,)), ((), ())), preferred_element_type=F32)


def _dot_tn(a, b):
    return lax.dot_general(a, b, (((0,), (0,)), ((), ())), preferred_element_type=F32)


def _silu(v):
    h = 0.5 * v
    return h * jnp.tanh(h) + h


def _norm_mod(x, gain, shift, scale):
    return x * lax.rsqrt(jnp.mean(x * x, axis=-1, keepdims=True) + EPS) * (gain * (1.0 + scale)) + shift


def _ada_kernel(c_ref, w_ref, b_ref, o_ref):
    @pl.when(pl.program_id(0) == 0)
    def _():
        o_ref[...] = jnp.broadcast_to(b_ref[...], o_ref.shape)

    cond = _silu(c_ref[...]).astype(BF16)
    o_ref[...] += _dot(cond, w_ref[...].astype(BF16))


def _ada(c_pad, w, b):
    rows, d = c_pad.shape
    n = w.shape[1]
    kc = d // ADA_STEPS
    return pl.pallas_call(
        _ada_kernel,
        grid=(ADA_STEPS,),
        in_specs=[
            pl.BlockSpec((rows, kc), lambda k: (0, k)),
            pl.BlockSpec((kc, n), lambda k: (k, 0)),
            pl.BlockSpec((1, n), lambda k: (0, 0)),
        ],
        out_specs=pl.BlockSpec((rows, n), lambda k: (0, 0)),
        out_shape=jax.ShapeDtypeStruct((rows, n), F32),
        compiler_params=pltpu.CompilerParams(dimension_semantics=("arbitrary",)),
        name="adaln",
    )(c_pad, w, b.reshape(1, n))


def _mxu_aligned_bounds(width, parts):
    if width % MXU_WIDTH:
        return (0, width)
    tiles = width // MXU_WIDTH
    return tuple(MXU_WIDTH * ((tiles * p + parts - 1) // parts) for p in range(parts)) + (width,)


def _ffn_weight_specs(layer, d, d_ff):
    assert d_ff % MXU_WIDTH == 0
    steps = d_ff // MXU_WIDTH

    def chunk(i):
        return jnp.minimum(i, steps - 1)

    specs = [
        pl.BlockSpec((None, d, MXU_WIDTH), lambda i: (layer, 0, chunk(i))),
        pl.BlockSpec((None, d, MXU_WIDTH), lambda i: (layer, 0, steps + chunk(i))),
        pl.BlockSpec((None, MXU_WIDTH, d), lambda i: (layer, chunk(i), 0)),
    ]
    scratch = [pltpu.VMEM((d, 2 * d_ff), BF16), pltpu.VMEM((d_ff, d), BF16)]
    return steps, specs, scratch


def _stage_ffn_weights(step, wg_ref, wu_ref, wd_ref, w_in_s, w_down_s, d_ff):
    for j in range(d_ff // MXU_WIDTH):
        @pl.when(step == j)
        def _(lo=j * MXU_WIDTH):
            w_in_s[:, lo:lo + MXU_WIDTH] = (GATE_PRESCALE * wg_ref[...]).astype(BF16)
            w_in_s[:, d_ff + lo:d_ff + lo + MXU_WIDTH] = wu_ref[...].astype(BF16)
            w_down_s[lo:lo + MXU_WIDTH, :] = wd_ref[...].astype(BF16)


class _SideCast:
    def __init__(self, weight_all, layer, n_tiles, prescaled_cols=0):
        self.weight_all, self.layer, self.prescaled_cols = weight_all, layer, prescaled_cols
        self.valid_rows, self.cols = weight_all.shape[1:]
        least = -(-(-(-self.valid_rows // n_tiles)) // 16) * 16
        self.rows = next((r for r in range(least, 2 * least + 1, 16) if self.valid_rows % r == 0), least)
        self.n_chunks = -(-self.valid_rows // self.rows)

    def chunk(self, tile):
        return jnp.minimum(tile, self.n_chunks - 1)

    def in_spec(self, tile_of_step):
        return pl.BlockSpec((None, self.rows, self.cols), lambda i: (self.layer, self.chunk(tile_of_step(i)), 0))

    def out_spec(self, tile_of_step):
        return pl.BlockSpec((self.rows, self.cols), lambda i: (self.chunk(tile_of_step(i)), 0))

    def out_shape(self):
        return jax.ShapeDtypeStruct((self.n_chunks * self.rows, self.cols), BF16)

    def cast(self, tile, in_ref, out_ref):
        w = in_ref[...]
        if self.n_chunks * self.rows > self.valid_rows:
            row = self.chunk(tile) * self.rows + lax.broadcasted_iota(jnp.int32, w.shape, 0)
            w = jnp.where(row < self.valid_rows, w, 0.0)
        if self.prescaled_cols:
            col = lax.broadcasted_iota(jnp.int32, (1, self.cols), 1)
            w = w * jnp.where(col < self.prescaled_cols, GATE_PRESCALE, 1.0)
        out_ref[...] = w.astype(BF16)


def _ffn_kernel(*refs, sub, d_ff, bounds, final, w_steps, sides):
    x_ref, mod_ref, gain_ref, wg_ref, wu_ref, wd_ref, fgain_ref = refs[:7]
    side_in = refs[7:7 + len(sides)]
    o_ref = refs[7 + len(sides)]
    side_out = refs[8 + len(sides):8 + 2 * len(sides)]
    w_in_s, w_down_s = refs[8 + 2 * len(sides):]
    step = pl.program_id(0)
    _stage_ffn_weights(step, wg_ref, wu_ref, wd_ref, w_in_s, w_down_s, d_ff)

    @pl.when(step >= w_steps)
    def _():
        o_ref[...] = _ffn_body(x_ref, mod_ref, gain_ref, w_in_s, w_down_s, fgain_ref,
                               sub=sub, d_ff=d_ff, bounds=bounds, final=final)
        for side, in_ref, out_ref in zip(sides, side_in, side_out):
            side.cast(step - w_steps, in_ref, out_ref)


def _ffn_body(x_src, mod_ref, gain_ref, w_in_ref, w_down_ref, fgain_ref, *, sub, d_ff, bounds, final):
    shift = mod_ref[0, 3 * sub + 0:3 * sub + 1, :]
    scale = mod_ref[0, 3 * sub + 1:3 * sub + 2, :]
    gate = mod_ref[0, 3 * sub + 2:3 * sub + 3, :]
    hb = _norm_mod(x_src[...], gain_ref[...], shift, scale).astype(BF16)
    acc = None
    for lo, hi in zip(bounds[:-1], bounds[1:]):
        h = _dot(hb, w_in_ref[:, lo:hi])
        u = _dot(hb, w_in_ref[:, d_ff + lo:d_ff + hi])
        act = ((h * jnp.tanh(h) + h) * u).astype(BF16)
        part = _dot(act, w_down_ref[lo:hi, :])
        acc = part if acc is None else acc + part
    xn = x_src[...] + (0.5 * gate) * acc
    if final:
        xn = xn * lax.rsqrt(jnp.mean(xn * xn, axis=-1, keepdims=True) + EPS) * fgain_ref[...]
    return xn


def _ffn(x2d, mod3, gain, w_in_all, w_down_all, fgain, side_weights, *, layer, sub, tokens_per_batch, final,
         tm=TOKEN_TILE):
    n, d = x2d.shape
    d_ff = w_down_all.shape[1]
    tiles_per_batch = tokens_per_batch // tm
    w_steps, w_specs, w_scratch = _ffn_weight_specs(layer, d, d_ff)
    sides = [_SideCast(w, layer, n // tm, prescaled) for w, prescaled in side_weights]

    def tile(i):
        return jnp.maximum(i - w_steps, 0)

    outs = pl.pallas_call(
        functools.partial(_ffn_kernel, sub=sub, d_ff=d_ff, bounds=_mxu_aligned_bounds(d_ff, 2), final=final,
                          w_steps=w_steps, sides=sides),
        grid=(w_steps + n // tm,),
        in_specs=[
            pl.BlockSpec((tm, d), lambda i: (tile(i), 0)),
            pl.BlockSpec((1, 3 * N_SUB, d), lambda i: (tile(i) // tiles_per_batch, 0, 0)),
            pl.BlockSpec((1, d), lambda i: (0, 0)),
            *w_specs,
            pl.BlockSpec((1, d), lambda i: (0, 0)),
            *[s.in_spec(tile) for s in sides],
        ],
        out_specs=[pl.BlockSpec((tm, d), lambda i: (tile(i), 0)), *[s.out_spec(tile) for s in sides]],
        out_shape=[jax.ShapeDtypeStruct((n, d), F32), *[s.out_shape() for s in sides]],
        scratch_shapes=w_scratch,
        compiler_params=pltpu.CompilerParams(dimension_semantics=("arbitrary",), vmem_limit_bytes=VMEM_LIMIT["ffn"]),
        name="ffn_final" if final else "ffn",
    )(x2d, mod3, gain.reshape(1, d), w_in_all, w_in_all, w_down_all, fgain.reshape(1, d),
      *[w for w, _ in side_weights])
    return outs[0], outs[1:]


DEC_OFF = {"g": 0, "gd": 4}
DEC_TILES = 6
ACT_OFF = {"q": 0, "k": 4, "v": 8, "sg": 12, "gq": 16, "gk": 18, "gv": 20, "gsg": 24}
ACT_TILES = 28


def _inproj_kernel(x_ref, mod_ref, gain_ref, wt_ref, lbl_ref, wup_ref, bup_ref, dec_ref, act_ref, r_ref, stats_ref, *,
                   sub, layer):
    shift = mod_ref[0, 3 * sub + 0:3 * sub + 1, :]
    scale = mod_ref[0, 3 * sub + 1:3 * sub + 2, :]
    hb = _norm_mod(x_ref[...], gain_ref[...], shift, scale).astype(BF16)
    tm = hb.shape[0]
    hg_qk, hg_w = HG_HEADS * HG_DK, HG_HEADS * HG_DV
    gla_k, gla_v = GLA_HEADS * GLA_DK, GLA_HEADS * GLA_DV
    n_hg = 2 * hg_qk + 2 * hg_w
    n_rec = n_hg + 2 * gla_k + 2 * gla_v

    def put(name, val):
        if name in DEC_OFF:
            lo = DEC_OFF[name] * LANES
            dec_ref[:, lo:lo + val.shape[1]] = val
        else:
            lo = ACT_OFF[name] * LANES
            val = val.astype(act_ref.dtype)
            act_ref[:, lo:lo + val.shape[1]] = val
        return val

    def chunk_min(g):
        tot = jnp.sum(g.reshape(tm // REC_CHUNK, REC_CHUNK, g.shape[1]), axis=1)
        return functools.reduce(jnp.minimum, [tot[:, j:j + LANES] for j in range(0, g.shape[1], LANES)])

    def mag_max(v):
        rows = 2 * SUBLANES
        m = jnp.max(jnp.abs(v).reshape(tm // rows, rows, v.shape[1]), axis=0)
        m = functools.reduce(jnp.maximum, [m[:, j:j + LANES] for j in range(0, v.shape[1], LANES)]).astype(F32)
        return jnp.maximum(m[:SUBLANES], m[SUBLANES:])

    ph = _dot_nt(hb, wt_ref[:n_hg, :])
    lbl = lbl_ref[...]
    ex = jnp.exp(lbl - jnp.max(lbl, axis=0, keepdims=True))
    lb = jnp.sum(ex[:layer + 1], axis=0, keepdims=True) / jnp.sum(ex, axis=0, keepdims=True)
    c1 = 0.5 * (1.0 - lb)
    q = _silu(ph[:, :hg_qk])
    p = c1 * jnp.tanh(0.5 * ph[:, hg_qk:2 * hg_qk])
    g = jnp.log2((1.0 - c1) + p)
    q_b = put("q", q)
    put("k", c1 - p)
    put("g", g)
    v_b = put("v", ph[:, 2 * hg_qk:2 * hg_qk + hg_w])
    put("sg", _silu(ph[:, 2 * hg_qk + hg_w:]))

    pg = _dot_nt(hb, wt_ref[n_hg:n_rec, :])
    code = _dot_nt(hb, wt_ref[n_rec:n_rec + LANES, :])
    logits = _dot(code.astype(BF16), wup_ref[...].astype(BF16)) + bup_ref[...]
    gd = (jnp.minimum(logits, 0.0) - jnp.log(1.0 + jnp.exp(-jnp.abs(logits)))) * (LOG2E / GLA_NORMALIZER)
    gq = pg[:, :gla_k] * (GLA_DK ** -0.5)
    gk = pg[:, gla_k:2 * gla_k]
    gq_b = put("gq", gq)
    gk_b = put("gk", gk)
    put("gd", gd)
    gv_b = put("gv", pg[:, 2 * gla_k:2 * gla_k + gla_v])
    put("gsg", _silu(pg[:, 2 * gla_k + gla_v:]))

    r_ref[...] = _dot_nt(hb, wt_ref[n_rec + GLA_RANK:n_rec + GLA_RANK + r_ref.shape[1], :]).astype(r_ref.dtype)
    stats_ref[0, 0] = jnp.minimum(chunk_min(g), chunk_min(gd))
    stats_ref[0, 1] = functools.reduce(jnp.maximum, [mag_max(a) for a in (q_b, v_b, gq_b, gk_b, gv_b)])


def _inproj(x2d, mod3, gain, wt_b, n_channels, lb_logits, wup_pad, bup, *, layer, sub, tokens_per_batch,
            tm=TOKEN_TILE):
    n, d = x2d.shape
    n_rec = 2 * HG_HEADS * HG_DK + 2 * HG_HEADS * HG_DV + 2 * GLA_HEADS * GLA_DK + 2 * GLA_HEADS * GLA_DV
    n_r = n_channels - n_rec - GLA_RANK
    assert n_rec + LANES <= wt_b.shape[0] and tm // REC_CHUNK == SUBLANES
    tiles_per_batch = tokens_per_batch // tm

    return pl.pallas_call(
        functools.partial(_inproj_kernel, sub=sub, layer=layer),
        grid=(n // tm,),
        in_specs=[
            pl.BlockSpec((tm, d), lambda i: (i, 0)),
            pl.BlockSpec((1, 3 * N_SUB, d), lambda i: (i // tiles_per_batch, 0, 0)),
            pl.BlockSpec((1, d), lambda i: (0, 0)),
            pl.BlockSpec(wt_b.shape, lambda i: (0, 0), pipeline_mode=pl.Buffered(1)),
            pl.BlockSpec(lb_logits.shape, lambda i: (0, 0)),
            pl.BlockSpec(wup_pad.shape, lambda i: (0, 0)),
            pl.BlockSpec((1, bup.shape[0]), lambda i: (0, 0)),
        ],
        out_specs=[
            pl.BlockSpec((tm, DEC_TILES * LANES), lambda i: (i, 0)),
            pl.BlockSpec((tm, ACT_TILES * LANES), lambda i: (i, 0)),
            pl.BlockSpec((tm, n_r), lambda i: (i, 0)),
            pl.BlockSpec((1, 2, SUBLANES, LANES), lambda i: (i, 0, 0, 0)),
        ],
        out_shape=[
            jax.ShapeDtypeStruct((n, DEC_TILES * LANES), F32),
            jax.ShapeDtypeStruct((n, ACT_TILES * LANES), BF16),
            jax.ShapeDtypeStruct((n, n_r), BF16),
            jax.ShapeDtypeStruct((n // tm, 2, SUBLANES, LANES), F32),
        ],
        compiler_params=pltpu.CompilerParams(dimension_semantics=("arbitrary",), vmem_limit_bytes=VMEM_LIMIT["inproj"]),
        name="inproj",
    )(x2d, mod3, gain.reshape(1, d), wt_b, lb_logits, wup_pad, bup.reshape(1, -1))


def _load_perm(ref, base):
    return [ref[pl.ds(base + PERM_GROUP * c + a, SUBLANES, stride=PERM_STRIDE), :]
            for c in range(N_GROUPS) for a in range(PERM_STRIDE)]


def _store_perm(ref, base, val):
    for c in range(N_GROUPS):
        for a in range(PERM_STRIDE):
            u = PERM_STRIDE * c + a
            ref[pl.ds(base + PERM_GROUP * c + a, SUBLANES, stride=PERM_STRIDE), :] = (
                val[SUBLANES * u:SUBLANES * (u + 1), :])


def _shift_down(v, d, sub_iota):
    return jnp.where(sub_iota >= d, pltpu.roll(v, d, 0), 0.0)


def _cumsum_perm(g, sub_iota):
    out = []
    carry = None
    for c in range(N_GROUPS):
        pre = [g[PERM_STRIDE * c]]
        for a in range(1, PERM_STRIDE):
            pre.append(pre[-1] + g[PERM_STRIDE * c + a])
        inc = pre[-1]
        inc = inc + _shift_down(inc, 1, sub_iota)
        inc = inc + _shift_down(inc, 2, sub_iota)
        inc = inc + _shift_down(inc, 4, sub_iota)
        exc = _shift_down(inc, 1, sub_iota)
        if carry is not None:
            exc = exc + carry
        out.extend(p + exc for p in pre)
        tot = inc[SUBLANES - 1:SUBLANES, :]
        carry = tot if carry is None else carry + tot
    return jnp.concatenate(out, axis=0), carry


def _finish(o, gain, act_gate):
    return o * lax.rsqrt(jnp.mean(o * o, axis=-1, keepdims=True) + EPS) * gain * act_gate


def _rec_kernel(safe_ref, q01, k01, g0, g1, v01, sg01, gq, gk, gd, gv01, gsg01,
                hgn_ref, glan_ref, yh_ref, yg_ref, st_s, e_s, fb_s, oi_s, *, tt, unroll):
    nc = tt // REC_CHUNK
    nb = q01.shape[0]

    @pl.when(pl.program_id(1) == 0)
    def _():
        st_s[...] = jnp.zeros_like(st_s)

    safe = safe_ref[pl.program_id(1)] != 0

    sub_iota = lax.broadcasted_iota(jnp.int32, (SUBLANES, LANES), 0)
    lane = lax.broadcasted_iota(jnp.int32, (REC_CHUNK, LANES), 1)
    causal = (lax.broadcasted_iota(jnp.int32, (REC_CHUNK, REC_CHUNK), 1)
              <= lax.broadcasted_iota(jnp.int32, (REC_CHUNK, REC_CHUNK), 0))
    tok_of_row = lax.broadcasted_iota(jnp.int32, (REC_CHUNK, LANES), 0)
    hgn = hgn_ref[...]
    glan = glan_ref[...]

    head_masks = (lane < GLA_DK, lane >= GLA_DK)
    groups = []
    for bi in range(nb):
        groups += [
            (((q01.at[bi], 0), (k01.at[bi], 0), g0.at[bi]), st_s.at[bi, 0],
             [(None, (v01.at[bi], 0), (sg01.at[bi], 0), hgn, yh_ref.at[0, bi])]),
            (((q01.at[bi], LANES), (k01.at[bi], LANES), g1.at[bi]), st_s.at[bi, 1],
             [(None, (v01.at[bi], LANES), (sg01.at[bi], LANES), hgn, yh_ref.at[1, bi])]),
            (((gq.at[bi], 0), (gk.at[bi], 0), gd.at[bi]), st_s.at[bi, 2],
             [(head_masks[0], (gv01.at[bi], 0), (gsg01.at[bi], 0), glan, yg_ref.at[0, bi]),
              (head_masks[1], (gv01.at[bi], LANES), (gsg01.at[bi], LANES), glan, yg_ref.at[1, bi])]),
        ]

    def natural(src, base):
        ref, lo = src
        return ref[pl.ds(base, REC_CHUNK), lo:lo + LANES]

    def decay_logs(g_ref, base):
        return _cumsum_perm(_load_perm(g_ref, base), sub_iota)

    def to_natural(slot, val):
        _store_perm(e_s.at[slot], 0, val)
        return e_s[slot]

    def masked(x, mask):
        return x if mask is None else jnp.where(mask, x, jnp.zeros_like(x))

    def fast_chunk(ci, carry):
        staged = []
        for sub in range(unroll):
            base = pl.multiple_of((ci * unroll + sub) * REC_CHUNK, REC_CHUNK)
            for gi, ((q_ref, k_ref, g_ref), st_view, heads) in enumerate(groups):
                b, b_last = decay_logs(g_ref, base)
                half = 0.5 * b_last
                d = b - half
                em = jnp.exp2(half)
                slot = 2 * (sub * len(groups) + gi)
                qt_b = (natural(q_ref, base).astype(F32) * to_natural(slot, jnp.exp2(d))).astype(BF16)
                kt_b = (natural(k_ref, base).astype(F32) * to_natural(slot + 1, jnp.exp2(-d))).astype(BF16)
                vbs = [natural(v_ref, base) for _, v_ref, _, _, _ in heads]
                q_all = jnp.concatenate([masked(qt_b, m) for m, *_ in heads], axis=0)
                k_all = jnp.concatenate([masked(kt_b, m) for m, *_ in heads], axis=0)
                v_all = jnp.concatenate(vbs, axis=0)
                state_e = st_view[...] * em
                scores = _dot_nt(q_all, kt_b)
                o_inter = _dot_nt(q_all, state_e.astype(BF16))
                st_view[...] = (state_e + _dot_tn(v_all, k_all)) * em
                for h, (_, _, gate_ref, gain, y_view) in enumerate(heads):
                    rows = slice(h * REC_CHUNK, (h + 1) * REC_CHUNK)
                    staged.append((scores[rows], o_inter[rows], vbs[h], base, gate_ref, gain, y_view))
        for scores, o_inter, vb, base, gate_ref, gain, y_view in staged:
            s = jnp.where(causal, scores, 0.0).astype(BF16)
            o = o_inter + _dot(s, vb)
            gate = natural(gate_ref, base).astype(F32)
            y_view[pl.ds(base, REC_CHUNK), :] = _finish(o, gain, gate).astype(y_view.dtype)
        return carry

    def exact_chunk(ci, carry):
        base = pl.multiple_of(ci * REC_CHUNK, REC_CHUNK)
        for (q_ref, k_ref, g_ref), st_view, heads in groups:
            b_perm, b_last = decay_logs(g_ref, base)
            b = to_natural(0, b_perm)
            q_all = natural(q_ref, base).astype(F32)
            k = natural(k_ref, base).astype(F32)
            state = st_view[...]
            new_state = state * jnp.exp2(b_last)
            kd = k * jnp.exp2(b_last - b)
            for qmask, v_ref, gate_ref, gain, y_view in heads:
                q = masked(q_all, qmask)
                vb = natural(v_ref, base)
                gate = natural(gate_ref, base).astype(F32)
                fb_s[0] = b
                fb_s[1] = q
                fb_s[2] = k
                fb_s[3] = vb.astype(F32)

                def row(r, c2):
                    bt = fb_s[0, pl.ds(r, 1), :]
                    qt = fb_s[1, pl.ds(r, 1), :]
                    w = jnp.exp2(jnp.where(tok_of_row <= r, bt - fb_s[0], -jnp.inf))
                    sc = jnp.sum(qt * w * fb_s[2], axis=-1, keepdims=True)
                    oi_s[pl.ds(r, 1), :] = jnp.sum(sc * fb_s[3], axis=0, keepdims=True)
                    return c2

                lax.fori_loop(0, REC_CHUNK, row, 0)
                o = _dot_nt((q * jnp.exp2(b)).astype(BF16), state.astype(BF16)) + oi_s[...]
                new_state = new_state + _dot_tn(vb, masked(kd, qmask).astype(BF16))
                y_view[pl.ds(base, REC_CHUNK), :] = _finish(o, gain, gate).astype(y_view.dtype)
            st_view[...] = new_state
        return carry

    @pl.when(safe)
    def _():
        lax.fori_loop(0, nc // unroll, fast_chunk, 0)

    @pl.when(jnp.logical_not(safe))
    def _():
        lax.fori_loop(0, nc, exact_chunk, 0)


def _safe_flags(stats, *, batch, tokens_per_batch, tt):
    tiles_per_batch = stats.shape[0] // batch
    chunk_tot = jnp.min(stats[:, 0], axis=-1).reshape(batch, tokens_per_batch // tt, -1)
    tile_mag = jnp.max(stats[:, 1], axis=(-2, -1)).reshape(batch, tiles_per_batch)
    step_tot = jnp.min(chunk_tot, axis=(0, 2))
    step_mag = jnp.repeat(jnp.max(tile_mag, axis=0), (tokens_per_batch // tt) // tiles_per_batch)
    return jnp.logical_and(step_tot >= -2.0 * SAFE_EXP2, step_mag <= SAFE_MAG).astype(jnp.int32)


def _recurrence(dec, act, stats, hg_norm, gla_norm, *, batch, tokens_per_batch, tt=REC_TILE):
    n = dec.shape[0]
    nt = tokens_per_batch // tt
    pairs = HG_HEADS // 2
    dec3 = dec.reshape(batch, tokens_per_batch, dec.shape[1])
    act3 = act.reshape(batch, tokens_per_batch, act.shape[1])
    safe = _safe_flags(stats, batch=batch, tokens_per_batch=tokens_per_batch, tt=tt)

    def tile_of(name, p, e):
        off = DEC_OFF[name] if name in DEC_OFF else ACT_OFF[name]
        return off + (p if e is None else 2 * p + e)

    def spec(name, e=None):
        if e == "both":
            return pl.BlockSpec((batch, tt, 2 * LANES), lambda p, i, flags: (0, i, tile_of(name, p, 0) // 2))
        return pl.BlockSpec((batch, tt, LANES), lambda p, i, flags: (0, i, tile_of(name, p, e)))

    assert all(ACT_OFF[name] % 2 == 0 for name in ("q", "k", "v", "sg", "gv", "gsg"))
    names = [("q", "both"), ("k", "both"), ("g", 0), ("g", 1), ("v", "both"), ("sg", "both"),
             ("gq", None), ("gk", None), ("gd", None), ("gv", "both"), ("gsg", "both")]
    in_specs = [spec(name, e) for name, e in names] + [
        pl.BlockSpec((1, LANES), lambda p, i, flags: (0, 0)),
        pl.BlockSpec((1, LANES), lambda p, i, flags: (0, 0)),
    ]
    operands = [dec3 if name in DEC_OFF else act3 for name, _ in names]
    n_groups = 3 * batch
    out_spec = pl.BlockSpec((2, batch, tt, LANES), lambda p, i, flags: (p, 0, i, 0))
    y_hg, y_gla = pl.pallas_call(
        functools.partial(_rec_kernel, tt=tt, unroll=REC_UNROLL),
        grid_spec=pltpu.PrefetchScalarGridSpec(
            num_scalar_prefetch=1,
            grid=(pairs, nt),
            in_specs=in_specs,
            out_specs=[out_spec, out_spec],
            scratch_shapes=[
                pltpu.VMEM((batch, 3, LANES, LANES), F32),
                pltpu.VMEM((2 * REC_UNROLL * n_groups, REC_CHUNK, LANES), F32),
                pltpu.VMEM((4, REC_CHUNK, LANES), F32),
                pltpu.VMEM((REC_CHUNK, LANES), F32),
            ],
        ),
        out_shape=[jax.ShapeDtypeStruct((HG_HEADS, batch, tokens_per_batch, HG_DV), BF16),
                   jax.ShapeDtypeStruct((GLA_HEADS, batch, tokens_per_batch, GLA_DV), BF16)],
        compiler_params=pltpu.CompilerParams(
            dimension_semantics=("arbitrary", "arbitrary"), vmem_limit_bytes=VMEM_LIMIT["recurrence"]),
        name="recurrence",
    )(safe, *operands, hg_norm.reshape(1, HG_DV), gla_norm.reshape(1, GLA_DV))
    return y_hg.reshape(HG_HEADS, n, HG_DV), y_gla.reshape(GLA_HEADS, n, GLA_DV)


def _merge_ffn_kernel(x_ref, mod_ref, yh_ref, yg_ref, rh_ref, rg_ref, wuh_ref, wug_ref, wo_ref,
                      gain_ref, w_in_ref, w_down_ref, fgain_ref, o_ref, *, sub_mix, sub_ffn, d_ff, bounds, final):
    gate = mod_ref[0, 3 * sub_mix + 2:3 * sub_mix + 3, :]

    def up(y_ref, w_ref):
        y = jnp.concatenate([y_ref[h].astype(BF16) for h in range(y_ref.shape[0])], axis=-1)
        return _dot(y, w_ref[...])

    merged2 = ((1.0 + jnp.tanh(0.5 * rh_ref[...].astype(F32))) * up(yh_ref, wuh_ref)
               + (1.0 + jnp.tanh(0.5 * rg_ref[...].astype(F32))) * up(yg_ref, wug_ref))
    o_ref[...] = x_ref[...] + (0.5 * gate) * _dot(merged2.astype(BF16), wo_ref[...])
    o_ref[...] = _ffn_body(o_ref, mod_ref, gain_ref, w_in_ref, w_down_ref, fgain_ref,
                           sub=sub_ffn, d_ff=d_ff, bounds=bounds, final=final)


def _merge_ffn(x2d, mod3, y_hg, y_gla, r, w_up_hg, w_up_gla, w_out, gain, w_in_b, w_down_b, d_ff, fgain, *,
               sub_mix, sub_ffn, tokens_per_batch, final, tm=TOKEN_TILE):
    n, d = x2d.shape
    tiles_per_batch = tokens_per_batch // tm
    const = dict(pipeline_mode=pl.Buffered(1))

    return pl.pallas_call(
        functools.partial(_merge_ffn_kernel, sub_mix=sub_mix, sub_ffn=sub_ffn, d_ff=d_ff,
                          bounds=_mxu_aligned_bounds(d_ff, 2), final=final),
        grid=(n // tm,),
        in_specs=[
            pl.BlockSpec((tm, d), lambda i: (i, 0)),
            pl.BlockSpec((1, 3 * N_SUB, d), lambda i: (i // tiles_per_batch, 0, 0)),
            pl.BlockSpec((HG_HEADS, tm, HG_DV), lambda i: (0, i, 0)),
            pl.BlockSpec((GLA_HEADS, tm, GLA_DV), lambda i: (0, i, 0)),
            pl.BlockSpec((tm, d), lambda i: (i, 0)),
            pl.BlockSpec((tm, d), lambda i: (i, 1)),
            pl.BlockSpec(w_up_hg.shape, lambda i: (0, 0), **const),
            pl.BlockSpec(w_up_gla.shape, lambda i: (0, 0), **const),
            pl.BlockSpec(w_out.shape, lambda i: (0, 0), **const),
            pl.BlockSpec((1, d), lambda i: (0, 0)),
            pl.BlockSpec(w_in_b.shape, lambda i: (0, 0), **const),
            pl.BlockSpec(w_down_b.shape, lambda i: (0, 0), **const),
            pl.BlockSpec((1, d), lambda i: (0, 0)),
        ],
        out_specs=pl.BlockSpec((tm, d), lambda i: (i, 0)),
        out_shape=jax.ShapeDtypeStruct((n, d), F32),
        compiler_params=pltpu.CompilerParams(dimension_semantics=("arbitrary",),
                                             vmem_limit_bytes=VMEM_LIMIT["merge_ffn"]),
        name="merge_ffn",
    )(x2d, mod3, y_hg, y_gla, r, r, w_up_hg, w_up_gla, w_out, gain.reshape(1, d), w_in_b, w_down_b,
      fgain.reshape(1, d))


def kernel(x, c, w_ada, b_ada, norm_gains, ffn1_w_in, ffn1_w_down, w_in_mix, w_gk_up, b_gk_up, lb_logits,
           hg_norm, gla_norm, w_up_hg, w_up_gla, w_out, ffn2_w_in, ffn2_w_down, final_norm):
    batch, seq, d = x.shape
    depth = w_ada.shape[0]

    x2d = x.reshape(batch * seq, d)
    c_pad = jnp.pad(c, ((0, SUBLANES - batch % SUBLANES if batch % SUBLANES else 0), (0, 0)))
    for l in range(depth):
        mod = _ada(c_pad, w_ada[l], b_ada[l])[:batch].reshape(batch, 3 * N_SUB, d)

        x2d, (w_mix_b, w_in2_b, w_down2_b) = _ffn(
            x2d, mod, norm_gains[l, 0], ffn1_w_in, ffn1_w_down, final_norm,
            ((jnp.swapaxes(w_in_mix, 1, 2), 0), (ffn2_w_in, ffn2_w_down.shape[1]), (ffn2_w_down, 0)),
            layer=l, sub=0, tokens_per_batch=seq, final=False)

        wup_pad = jnp.pad(w_gk_up[l], ((0, LANES - GLA_RANK), (0, 0)))
        dec, act, r, stats = _inproj(x2d, mod, norm_gains[l, 1], w_mix_b, w_in_mix.shape[2], lb_logits, wup_pad,
                                     b_gk_up[l], layer=l, sub=1, tokens_per_batch=seq)

        y_hg, y_gla = _recurrence(dec, act, stats, hg_norm[l], gla_norm[l], batch=batch, tokens_per_batch=seq)

        x2d = _merge_ffn(x2d, mod, y_hg, y_gla, r, w_up_hg[l].astype(BF16), w_up_gla[l].astype(BF16),
                         w_out[l].astype(BF16), norm_gains[l, 2], w_in2_b, w_down2_b, ffn2_w_down.shape[1],
                         final_norm, sub_mix=1, sub_ffn=2, tokens_per_batch=seq, final=l == depth - 1)
    return x2d.reshape(batch, seq, d)
```
